```python
import jax, jax.numpy as jnp
from jax import lax
import numpy as np

D_MODEL = 2048
BATCH = 4
SEQ = 2048
DEPTH = 2
DEC_BATCH = 128
DEC_SEQ = 4
PAST_LEN = 16384
PAGE_SIZE = 128

MIX_WIDTH = D_MODEL
BRANCH = MIX_WIDTH // 4
CHUNK = 128
EPS = 1e-6
ML_HEADS = 4
ML_DK = BRANCH // ML_HEADS
ML_DV = BRANCH // ML_HEADS
NEG_BIG = -1e30
SSD_HEADDIM = 64
SSD_HEADS = BRANCH // SSD_HEADDIM
SSD_GROUPS = 2
SSD_STATE = 128
SSD_CONV = 4
SSD_CONV_DIM = BRANCH + 2 * SSD_GROUPS * SSD_STATE
SC_WIDTH = 3
RG_BLOCKS = 8
RG_BLOCK_DIM = BRANCH // RG_BLOCKS
RG_CONV = 4
RG_C = 8.0

SPLIT_SIZES = (BRANCH, BRANCH, BRANCH, BRANCH, BRANCH, ML_HEADS, ML_HEADS,
               BRANCH, SSD_CONV_DIM, SSD_HEADS,
               BRANCH, BRANCH, BRANCH, BRANCH,
               BRANCH, BRANCH)
PROJ_OUT = sum(SPLIT_SIZES)

kernel_name = "hybrid_mlstm_ssd_shortconv_rglru_step"


def rms_norm(x, w):
    x32 = x.astype(jnp.float32)
    y = x32 * lax.rsqrt(jnp.mean(x32 * x32, axis=-1, keepdims=True) + EPS)
    return (y * w.astype(jnp.float32)).astype(x.dtype)


def split_columns(proj):
    idx, acc = [], 0
    for s in SPLIT_SIZES[:-1]:
        acc += s
        idx.append(acc)
    return jnp.split(proj, idx, axis=-1)


def pick_chunk(t):
    return CHUNK if t % CHUNK == 0 else t


def to_chunks(a, chunk):
    bsz, t = a.shape[:2]
    a = a.reshape((bsz, t // chunk, chunk) + a.shape[2:])
    return jnp.moveaxis(jnp.moveaxis(a, 1, 0), 3, 2)


def from_chunks(a):
    nc, bsz, h, l, d = a.shape
    return jnp.transpose(a, (1, 0, 3, 2, 4)).reshape(bsz, nc * l, h, d)


def causal_conv(u, buf, w, b=None):
    width = w.shape[0]
    t = u.shape[1]
    up = jnp.concatenate([buf.astype(u.dtype), u], axis=1)
    y = up[:, 0:t] * w[0]
    for j in range(1, width):
        y = y + up[:, j:j + t] * w[j]
    if b is not None:
        y = y + b
    return y, up[:, t:]


def mlstm_chunkwise(q, k, v, i_pre, f_pre, c0, n0, m0, chunk):
    lf = jax.nn.log_sigmoid(f_pre)
    causal = jnp.tril(jnp.ones((chunk, chunk), dtype=bool))
    xs = (to_chunks(q, chunk), to_chunks(k, chunk), to_chunks(v, chunk),
          to_chunks(i_pre, chunk), to_chunks(lf, chunk))

    def step(carry, inp):
        c, n, m = carry
        qc, kc, vc, ic, lfc = inp
        b = jnp.cumsum(lfc, axis=-1)
        dmat = b[..., :, None] - b[..., None, :] + ic[..., None, :]
        dmat = jnp.where(causal, dmat, -jnp.inf)
        inter = b + m[..., None]
        m_t = jnp.maximum(inter, jnp.max(dmat, axis=-1))
        w_intra = jnp.exp(dmat - m_t[..., None])
        w_inter = jnp.exp(inter - m_t)
        s = jnp.einsum('bhld,bhsd->bhls', qc, kc) * w_intra
        num = (jnp.einsum('bhls,bhsv->bhlv', s, vc)
               + w_inter[..., None] * jnp.einsum('bhld,bhdv->bhlv', qc, c))
        den = jnp.sum(s, axis=-1) + w_inter * jnp.einsum('bhld,bhd->bhl', qc, n)
        h = num / jnp.maximum(jnp.abs(den), jnp.exp(-m_t))[..., None]
        b_last = b[..., -1]
        g = b_last[..., None] - b + ic
        inter_end = b_last + m
        m_new = jnp.maximum(inter_end, jnp.max(g, axis=-1))
        wg = jnp.exp(g - m_new[..., None])
        we = jnp.exp(inter_end - m_new)
        c_new = we[..., None, None] * c + jnp.einsum('bhl,bhld,bhlv->bhdv', wg, kc, vc)
        n_new = we[..., None] * n + jnp.einsum('bhl,bhld->bhd', wg, kc)
        return (c_new, n_new, m_new), h

    (c1, n1, m1), h = lax.scan(step, (c0, n0, m0), xs)
    return from_chunks(h), c1, n1, m1


def ssd_chunkwise(x, dt, a_coef, bmat, cmat, s0, chunk):
    da = dt * a_coef
    causal = jnp.tril(jnp.ones((chunk, chunk), dtype=bool))
    xs = (to_chunks(x, chunk), to_chunks(dt, chunk), to_chunks(da, chunk),
          to_chunks(bmat, chunk), to_chunks(cmat, chunk))

    def step(s, inp):
        xc, dtc, dac, bc, cc = inp
        acs = jnp.cumsum(dac, axis=-1)
        seg = jnp.where(causal, acs[..., :, None] - acs[..., None, :], -jnp.inf)
        scores = jnp.einsum('bhln,bhsn->bhls', cc, bc) * jnp.exp(seg) * dtc[..., None, :]
        y = (jnp.einsum('bhls,bhsp->bhlp', scores, xc)
             + jnp.exp(acs)[..., None] * jnp.einsum('bhln,bhpn->bhlp', cc, s))
        a_last = acs[..., -1]
        w = jnp.exp(a_last[..., None] - acs) * dtc
        s_new = jnp.exp(a_last)[..., None, None] * s + jnp.einsum('bhl,bhlp,bhln->bhpn', w, xc, bc)
        return s_new, y

    s1, y = lax.scan(step, s0, xs)
    return from_chunks(y), s1


def rglru_scan(xr, r, ig, lam, h0):
    log_a = -RG_C * r * jax.nn.softplus(-lam)
    a = jnp.exp(log_a)
    bterm = jnp.sqrt(-jnp.expm1(2.0 * log_a)) * (ig * xr)
    bterm = bterm.at[:, 0].add(a[:, 0] * h0)

    def combine(e1, e2):
        a1, b1 = e1
        a2, b2 = e2
        return a1 * a2, a2 * b1 + b2

    _, h = lax.associative_scan(combine, (a, bterm), axis=1)
    return h, h[:, -1]


def mixer_layer(x, state, p, chunk):
    c0, n0, m0, s0, ssd_buf, sc_buf, h0, rg_buf = state
    (norm_w, w_in, ml_i_bias, ml_f_bias, ml_norm_w, ssd_conv_w, ssd_conv_b, ssd_dt_bias,
     ssd_A_log, ssd_D, ssd_norm_w, sc_conv_w, rg_conv_w, rg_conv_b, rg_wa, rg_ba, rg_wx,
     rg_bx, rg_lambda, w_out) = p
    f32 = jnp.float32
    bsz, t, _ = x.shape
    xn = rms_norm(x, norm_w)
    proj = xn @ w_in
    (ml_q, ml_k, ml_v, ml_o, ml_z, ml_i, ml_f, ssd_z, ssd_xbc, ssd_dt,
     sc_b, sc_c, sc_h, sc_z, rg_x, rg_z) = split_columns(proj)

    q = ml_q.reshape(bsz, t, ML_HEADS, ML_DK).astype(f32)
    k = ml_k.reshape(bsz, t, ML_HEADS, ML_DK).astype(f32) * (ML_DK ** -0.5)
    v = ml_v.reshape(bsz, t, ML_HEADS, ML_DV).astype(f32)
    i_pre = (ml_i + ml_i_bias).astype(f32)
    f_pre = (ml_f + ml_f_bias).astype(f32)
    h_ml, c1, n1, m1 = mlstm_chunkwise(q, k, v, i_pre, f_pre, c0.astype(f32), n0.astype(f32),
                                       m0.astype(f32), chunk)
    mu = jnp.mean(h_ml, axis=-1, keepdims=True)
    var = jnp.mean(jnp.square(h_ml - mu), axis=-1, keepdims=True)
    h_ml = (h_ml - mu) * lax.rsqrt(var + EPS) * ml_norm_w.astype(f32).reshape(ML_HEADS, ML_DV)
    h_ml = h_ml.reshape(bsz, t, BRANCH)
    y_ml = (jax.nn.sigmoid(ml_o.astype(f32)) * h_ml * jax.nn.silu(ml_z.astype(f32))).astype(x.dtype)

    xbc, ssd_buf1 = causal_conv(ssd_xbc, ssd_buf, ssd_conv_w, ssd_conv_b)
    xbc = jax.nn.silu(xbc.astype(f32))
    xs_, bm, cm = jnp.split(xbc, [BRANCH, BRANCH + SSD_GROUPS * SSD_STATE], axis=-1)
    xs_ = xs_.reshape(bsz, t, SSD_HEADS, SSD_HEADDIM)
    rep = SSD_HEADS // SSD_GROUPS
    bm = jnp.repeat(bm.reshape(bsz, t, SSD_GROUPS, SSD_STATE), rep, axis=2)
    cm = jnp.repeat(cm.reshape(bsz, t, SSD_GROUPS, SSD_STATE), rep, axis=2)
    dt = jax.nn.softplus((ssd_dt + ssd_dt_bias).astype(f32))
    a_coef = -jnp.exp(ssd_A_log.astype(f32))
    y_ssd, s1 = ssd_chunkwise(xs_, dt, a_coef, bm, cm, s0.astype(f32), chunk)
    y_ssd = y_ssd + ssd_D.astype(f32)[:, None] * xs_
    y_ssd = y_ssd.reshape(bsz, t, BRANCH) * jax.nn.silu(ssd_z.astype(f32))
    yg = y_ssd.reshape(bsz, t, SSD_GROUPS, BRANCH // SSD_GROUPS)
    yg = yg * lax.rsqrt(jnp.mean(yg * yg, axis=-1, keepdims=True) + EPS)
    y_ssd = (yg.reshape(bsz, t, BRANCH) * ssd_norm_w.astype(f32)).astype(x.dtype)

    u = sc_c * sc_h
    cu, sc_buf1 = causal_conv(u, sc_buf, sc_conv_w)
    y_sc = (sc_b * cu * jax.nn.silu(sc_z)).astype(x.dtype)

    xr, rg_buf1 = causal_conv(rg_x, rg_buf, rg_conv_w, rg_conv_b)
    xr32 = xr.astype(f32)
    xb = xr32.reshape(bsz, t, RG_BLOCKS, RG_BLOCK_DIM)
    r = jax.nn.sigmoid(jnp.einsum('btnd,nde->btne', xb, rg_wa.astype(f32)).reshape(bsz, t, BRANCH)
                       + rg_ba.astype(f32))
    ig = jax.nn.sigmoid(jnp.einsum('btnd,nde->btne', xb, rg_wx.astype(f32)).reshape(bsz, t, BRANCH)
                        + rg_bx.astype(f32))
    h_rg, h1 = rglru_scan(xr32, r, ig, rg_lambda.astype(f32), h0.astype(f32))
    y_rg = (h_rg * jax.nn.silu(rg_z.astype(f32))).astype(x.dtype)

    mix = jnp.concatenate([y_ml, y_ssd, y_sc, y_rg], axis=-1)
    y = x + mix @ w_out
    dtp = x.dtype
    new_state = (c1.astype(dtp), n1.astype(dtp), m1.astype(dtp), s1.astype(dtp),
                 ssd_buf1.astype(dtp), sc_buf1.astype(dtp), h1.astype(dtp), rg_buf1.astype(dtp))
    return y, new_state


def setup_inputs(seed: int = 0) -> dict:
    key = jax.random.key(seed)
    ks = jax.random.split(key, 40)
    f32 = jnp.float32
    nrm = lambda k, shape, s: jax.random.normal(k, shape, f32) * s
    L = DEPTH
    u_dt = jax.random.uniform(ks[20], (L, SSD_HEADS), f32, np.log(1e-3), np.log(1e-1))
    dt0 = jnp.exp(u_dt)
    u_a = jax.random.uniform(ks[25], (L, BRANCH), f32, 0.9, 0.999)
    return {
        "x_prompt": nrm(ks[0], (BATCH, SEQ, D_MODEL), 1.0),
        "x_sample": nrm(ks[1], (DEC_BATCH, DEC_SEQ, D_MODEL), 1.0),
        "state_mlstm_C": nrm(ks[2], (L, DEC_BATCH, ML_HEADS, ML_DK, ML_DV), 0.1),
        "state_mlstm_n": nrm(ks[3], (L, DEC_BATCH, ML_HEADS, ML_DK), 0.1),
        "state_mlstm_m": nrm(ks[4], (L, DEC_BATCH, ML_HEADS), 0.5),
        "state_ssd": nrm(ks[5], (L, DEC_BATCH, SSD_HEADS, SSD_HEADDIM, SSD_STATE), 0.1),
        "state_ssd_conv": nrm(ks[6], (L, DEC_BATCH, SSD_CONV - 1, SSD_CONV_DIM), 1.0),
        "state_sconv_conv": nrm(ks[7], (L, DEC_BATCH, SC_WIDTH - 1, BRANCH), 1.0),
        "state_rglru_h": nrm(ks[8], (L, DEC_BATCH, BRANCH), 0.5),
        "state_rglru_conv": nrm(ks[9], (L, DEC_BATCH, RG_CONV - 1, BRANCH), 1.0),
        "norm_w": 1.0 + nrm(ks[10], (L, D_MODEL), 0.02),
        "w_in": nrm(ks[11], (L, D_MODEL, PROJ_OUT), D_MODEL ** -0.5),
        "ml_i_bias": nrm(ks[12], (L, ML_HEADS), 0.1),
        "ml_f_bias": 3.0 + nrm(ks[13], (L, ML_HEADS), 0.5),
        "ml_norm_w": 1.0 + nrm(ks[14], (L, BRANCH), 0.02),
        "ssd_conv_w": nrm(ks[15], (L, SSD_CONV, SSD_CONV_DIM), SSD_CONV ** -0.5),
        "ssd_conv_b": nrm(ks[16], (L, SSD_CONV_DIM), 0.02),
        "ssd_dt_bias": dt0 + jnp.log(-jnp.expm1(-dt0)),
        "ssd_A_log": jnp.log(jax.random.uniform(ks[17], (L, SSD_HEADS), f32, 1.0, 16.0)),
        "ssd_D": 1.0 + nrm(ks[18], (L, SSD_HEADS), 0.1),
        "ssd_norm_w": 1.0 + nrm(ks[19], (L, BRANCH), 0.02),
        "sc_conv_w": nrm(ks[21], (L, SC_WIDTH, BRANCH), SC_WIDTH ** -0.5),
        "rg_conv_w": nrm(ks[22], (L, RG_CONV, BRANCH), RG_CONV ** -0.5),
        "rg_conv_b": nrm(ks[23], (L, BRANCH), 0.02),
        "rg_wa": nrm(ks[24], (L, RG_BLOCKS, RG_BLOCK_DIM, RG_BLOCK_DIM), RG_BLOCK_DIM ** -0.5),
        "rg_ba": nrm(ks[26], (L, BRANCH), 0.02),
        "rg_wx": nrm(ks[27], (L, RG_BLOCKS, RG_BLOCK_DIM, RG_BLOCK_DIM), RG_BLOCK_DIM ** -0.5),
        "rg_bx": nrm(ks[28], (L, BRANCH), 0.02),
        "rg_lambda": jnp.log(u_a / (1.0 - u_a)),
        "w_out": nrm(ks[29], (L, MIX_WIDTH, D_MODEL), MIX_WIDTH ** -0.5),
        "final_norm_w": 1.0 + nrm(ks[30], (D_MODEL,), 0.02),
    }


def stack_state(states, i):
    return jnp.stack([st[i] for st in states], axis=0)


def reference(x_prompt, x_sample, state_mlstm_C, state_mlstm_n, state_mlstm_m, state_ssd,
              state_ssd_conv, state_sconv_conv, state_rglru_h, state_rglru_conv,
              norm_w, w_in, ml_i_bias, ml_f_bias, ml_norm_w, ssd_conv_w, ssd_conv_b,
              ssd_dt_bias, ssd_A_log, ssd_D, ssd_norm_w, sc_conv_w, rg_conv_w, rg_conv_b,
              rg_wa, rg_ba, rg_wx, rg_bx, rg_lambda, w_out, final_norm_w):
    f32 = jnp.float32
    bp = x_prompt.shape[0]
    dtp = x_prompt.dtype
    init_p = (jnp.zeros((bp, ML_HEADS, ML_DK, ML_DV), f32),
              jnp.zeros((bp, ML_HEADS, ML_DK), f32),
              jnp.full((bp, ML_HEADS), NEG_BIG, f32),
              jnp.zeros((bp, SSD_HEADS, SSD_HEADDIM, SSD_STATE), f32),
              jnp.zeros((bp, SSD_CONV - 1, SSD_CONV_DIM), dtp),
              jnp.zeros((bp, SC_WIDTH - 1, BRANCH), dtp),
              jnp.zeros((bp, BRANCH), f32),
              jnp.zeros((bp, RG_CONV - 1, BRANCH), dtp))
    chunk_p = pick_chunk(x_prompt.shape[1])
    chunk_s = pick_chunk(x_sample.shape[1])
    yp, ys = x_prompt, x_sample
    new_p, new_s = [], []
    for l in range(DEPTH):
        p = (norm_w[l], w_in[l], ml_i_bias[l], ml_f_bias[l], ml_norm_w[l], ssd_conv_w[l],
             ssd_conv_b[l], ssd_dt_bias[l], ssd_A_log[l], ssd_D[l], ssd_norm_w[l], sc_conv_w[l],
             rg_conv_w[l], rg_conv_b[l], rg_wa[l], rg_ba[l], rg_wx[l], rg_bx[l], rg_lambda[l],
             w_out[l])
        state_s = (state_mlstm_C[l], state_mlstm_n[l], state_mlstm_m[l], state_ssd[l],
                   state_ssd_conv[l], state_sconv_conv[l], state_rglru_h[l], state_rglru_conv[l])
        yp, st_p = mixer_layer(yp, init_p, p, chunk_p)
        ys, st_s = mixer_layer(ys, state_s, p, chunk_s)
        new_p.append(st_p)
        new_s.append(st_s)
    y_prompt = rms_norm(yp, final_norm_w)
    y_sample = rms_norm(ys, final_norm_w)
    p_C = stack_state(new_p, 0)
    p_n = stack_state(new_p, 1)
    p_m = stack_state(new_p, 2)
    p_ssd = stack_state(new_p, 3)
    p_ssd_conv = stack_state(new_p, 4)
    p_sc_conv = stack_state(new_p, 5)
    p_rg_h = stack_state(new_p, 6)
    p_rg_conv = stack_state(new_p, 7)
    s_C = stack_state(new_s, 0)
    s_n = stack_state(new_s, 1)
    s_m = stack_state(new_s, 2)
    s_ssd = stack_state(new_s, 3)
    s_ssd_conv = stack_state(new_s, 4)
    s_sc_conv = stack_state(new_s, 5)
    s_rg_h = stack_state(new_s, 6)
    s_rg_conv = stack_state(new_s, 7)
    return (y_prompt, y_sample,
            p_C, p_n, p_m, p_ssd, p_ssd_conv, p_sc_conv, p_rg_h, p_rg_conv,
            s_C, s_n, s_m, s_ssd, s_ssd_conv, s_sc_conv, s_rg_h, s_rg_conv)
```

```python
import functools

import jax
import jax.numpy as jnp
from jax import lax
from jax.experimental import pallas as pl
from jax.experimental.pallas import tpu as pltpu

F32 = jnp.float32
BF16 = jnp.bfloat16

D_MODEL = 2048
DEPTH = 2
BRANCH = 512
CHUNK = 128
EPS = 1e-6
ML_HEADS = 4
ML_D = 128
NEG_BIG = -1e30
SSD_HEADS = 8
SSD_P = 64
SSD_GROUPS = 2
SSD_N = 128
SSD_CONV = 4
SSD_CONV_DIM = 1024
SC_WIDTH = 3
RG_CONV = 4
RG_C = 8.0
RG_BLOCKS = 8

C_Q, C_K, C_V, C_O, C_Z = 0, 512, 1024, 1536, 2048
C_SSD_Z, C_XBC = 2560, 3072
C_SC_B, C_SC_C, C_SC_H, C_SC_Z = 4096, 4608, 5120, 5632
C_RG_X, C_RG_Z = 6144, 6656
C_SMALL = 7168
PROJ_W = 7296
LANES = 128
CONV_BASE = 8

VMEM_LIMIT = 56 * 1024 * 1024

NT_DIMS = (((1,), (1,)), ((), ()))
TN_DIMS = (((0,), (0,)), ((), ()))


def _softplus(x):
    return jnp.maximum(x, 0.0) + jnp.log1p(jnp.exp(-jnp.abs(x)))


def _silu(x):
    return x * jax.nn.sigmoid(x)


def _dot(a, b):
    return jnp.dot(a, b, preferred_element_type=F32)


def _dot_nt(a, b):
    return lax.dot_general(a, b, NT_DIMS, preferred_element_type=F32)


def _dot_tn(a, b):
    return lax.dot_general(a, b, TN_DIMS, preferred_element_type=F32)


def _inproj_kernel(x_ref, nw_ref, w_ref, o_ref, xn_ref):
    @pl.when(pl.program_id(1) == 0)
    def _():
        x = x_ref[...]
        y = x * lax.rsqrt(jnp.mean(x * x, axis=-1, keepdims=True) + EPS)
        xn_ref[...] = (y * nw_ref[...]).astype(BF16)

    o_ref[...] = _dot(xn_ref[...], w_ref[...])


def _inproj(x, norm_w, w, tm):
    m = x.shape[0]
    tn = PROJ_W // 3
    return pl.pallas_call(
        _inproj_kernel,
        grid=(m // tm, PROJ_W // tn),
        in_specs=[
            pl.BlockSpec((tm, D_MODEL), lambda i, j: (i, 0)),
            pl.BlockSpec((1, D_MODEL), lambda i, j: (0, 0)),
            pl.BlockSpec((D_MODEL, tn), lambda i, j: (0, j)),
        ],
        out_specs=pl.BlockSpec((tm, tn), lambda i, j: (i, j)),
        out_shape=jax.ShapeDtypeStruct((m, PROJ_W), F32),
        scratch_shapes=[pltpu.VMEM((tm, D_MODEL), BF16)],
        compiler_params=pltpu.CompilerParams(
            dimension_semantics=("arbitrary", "arbitrary"), vmem_limit_bytes=VMEM_LIMIT),
        name="inproj",
    )(x, norm_w, w)


def _outproj_kernel(final, mix_ref, x_ref, w_ref, fnw_ref, o_ref):
    y = x_ref[...] + _dot(mix_ref[...].astype(BF16), w_ref[...])
    if final:
        y = y * lax.rsqrt(jnp.mean(y * y, axis=-1, keepdims=True) + EPS) * fnw_ref[...]
    o_ref[...] = y


def _outproj(mix, x, w, fnw, final, tm):
    m = x.shape[0]
    return pl.pallas_call(
        functools.partial(_outproj_kernel, final),
        grid=(m // tm,),
        in_specs=[
            pl.BlockSpec((tm, D_MODEL), lambda i: (i, 0)),
            pl.BlockSpec((tm, D_MODEL), lambda i: (i, 0)),
            pl.BlockSpec((D_MODEL, D_MODEL), lambda i: (0, 0)),
            pl.BlockSpec((1, D_MODEL), lambda i: (0, 0)),
        ],
        out_specs=pl.BlockSpec((tm, D_MODEL), lambda i: (i, 0)),
        out_shape=jax.ShapeDtypeStruct((m, D_MODEL), F32),
        compiler_params=pltpu.CompilerParams(
            dimension_semantics=("arbitrary",), vmem_limit_bytes=VMEM_LIMIT),
        name="outproj",
    )(mix, x, w, fnw)


def _causal_conv(ext_ref, u, w_ref, width, rows):
    lo = CONV_BASE - (width - 1)
    ext_ref[CONV_BASE:CONV_BASE + rows, :] = u
    acc = ext_ref[lo:lo + rows, :] * w_ref[0:1, :]
    for j in range(1, width):
        acc = acc + ext_ref[lo + j:lo + j + rows, :] * w_ref[j:j + 1, :]
    tail = ext_ref[lo + rows:CONV_BASE + rows, :]
    ext_ref[lo:CONV_BASE, :] = tail
    return acc, tail


def _linear_scan(a, b, h0, rows):
    if rows < 8:
        h = h0
        out = []
        for t in range(rows):
            h = a[t:t + 1, :] * h + b[t:t + 1, :]
            out.append(h)
        return jnp.concatenate(out, axis=0)
    row = lax.broadcasted_iota(jnp.int32, a.shape, 0)
    d = 1
    while d < rows:
        keep = row >= d
        a_prev = jnp.where(keep, pltpu.roll(a, d, 0), 1.0)
        b_prev = jnp.where(keep, pltpu.roll(b, d, 0), 0.0)
        b = a * b_prev + b
        a = a * a_prev
        d *= 2
    return b + a * h0


def _mixer_kernel(rows,
                  proj_ref, c_in, n_in, m_in, s_in, ssdcv_in, sccv_in, rgh_in, rgcv_in,
                  bias_ref, alog_ref, mlnw_ref, ssdcw_ref, ssdcb_ref, drow_ref, ssdnw_ref,
                  sccw_ref, rgcw_ref, rgcb_ref, rgw_ref, rgba_ref, rgbx_ref, rglam_ref,
                  mix_ref, c_out, n_out, m_out, s_out, ssdcv_out, sccv_out, rgh_out, rgcv_out,
                  ssd_ext, sc_ext, rg_ext, yssd_ref):
    L = rows

    @pl.when(pl.program_id(1) == 0)
    def _load_state():
        c_out[...] = c_in[...]
        n_out[...] = n_in[...]
        m_out[...] = m_in[...]
        s_out[...] = s_in[...]
        rgh_out[...] = rgh_in[...]
        ssd_ext[CONV_BASE - (SSD_CONV - 1):CONV_BASE, :] = ssdcv_in[0, 0]
        sc_ext[CONV_BASE - (SC_WIDTH - 1):CONV_BASE, :] = sccv_in[0, 0]
        rg_ext[CONV_BASE - (RG_CONV - 1):CONV_BASE, :] = rgcv_in[0, 0]

    def pcols(start, width):
        return proj_ref[0, :, start:start + width]

    small = pcols(C_SMALL, LANES) + bias_ref[...]
    lane = lax.broadcasted_iota(jnp.int32, (L, LANES), 1)
    sp = _softplus(small)
    logsig = -_softplus(-small)
    a_coef = -jnp.exp(alog_ref[...])
    z = jnp.where((lane >= 4) & (lane < 8), logsig,
                  jnp.where((lane >= 8) & (lane < 16), sp * a_coef, 0.0))
    rr = lax.broadcasted_iota(jnp.int32, (L, L), 0)
    cc = lax.broadcasted_iota(jnp.int32, (L, L), 1)
    causal = cc <= rr
    tril = causal.astype(F32)
    cum = jnp.dot(tril, z, precision=lax.Precision.HIGHEST, preferred_element_type=F32)
    packed = jnp.where(lane < 4, small,
                       jnp.where(lane < 16, cum, jnp.where(lane < 24, sp, 0.0)))
    er = lax.broadcasted_iota(jnp.int32, (LANES, LANES), 0)
    ec = lax.broadcasted_iota(jnp.int32, (LANES, LANES), 1)
    eye = (er == ec).astype(F32)
    packed_t = lax.dot_general(eye, packed, NT_DIMS, precision=lax.Precision.HIGHEST,
                               preferred_element_type=F32)

    m_row = m_out[0, 0]
    lane1 = lax.broadcasted_iota(jnp.int32, (1, LANES), 1)
    m_row_new = m_row
    for h in range(ML_HEADS):
        q = pcols(C_Q + h * ML_D, ML_D)
        k = pcols(C_K + h * ML_D, ML_D) * (ML_D ** -0.5)
        v = pcols(C_V + h * ML_D, ML_D)
        qb, kb, vb = q.astype(BF16), k.astype(BF16), v.astype(BF16)
        b_col = cum[:, 4 + h:5 + h]
        i_col = small[:, h:h + 1]
        b_row = packed_t[4 + h:5 + h, :]
        i_row = packed_t[h:h + 1, :]
        m_h = m_row[:, 4 + h:5 + h]
        dmat = jnp.where(causal, b_col - b_row + i_row, -jnp.inf)
        inter = b_col + m_h
        m_t = jnp.maximum(inter, jnp.max(dmat, axis=1, keepdims=True))
        w_intra = jnp.exp(dmat - m_t)
        w_inter = jnp.exp(inter - m_t)
        s = _dot_nt(qb, kb) * w_intra
        c_h = c_out[0, 0, h]
        n_h = n_out[0, 0, h:h + 1, :]
        num = _dot(s.astype(BF16), vb) + w_inter * _dot(qb, c_h.astype(BF16))
        den = jnp.sum(s, axis=1, keepdims=True) + w_inter * jnp.sum(q * n_h, axis=1, keepdims=True)
        hh = num / jnp.maximum(jnp.abs(den), jnp.exp(-m_t))
        b_last = cum[L - 1:L, 4 + h:5 + h]
        g = b_last - b_col + i_col
        inter_end = b_last + m_h
        m_new = jnp.maximum(inter_end, jnp.max(g, axis=0, keepdims=True))
        wg = jnp.exp(g - m_new)
        we = jnp.exp(inter_end - m_new)
        kw = k * wg
        c_out[0, 0, h] = we * c_h + _dot_tn(kw.astype(BF16), vb)
        n_out[0, 0, h:h + 1, :] = we * n_h + jnp.sum(kw, axis=0, keepdims=True)
        m_row_new = jnp.where(lane1 == 4 + h, m_new, m_row_new)
        mu = jnp.mean(hh, axis=-1, keepdims=True)
        hc = hh - mu
        var = jnp.mean(hc * hc, axis=-1, keepdims=True)
        hn = hc * lax.rsqrt(var + EPS) * mlnw_ref[:, h * ML_D:(h + 1) * ML_D]
        o = pcols(C_O + h * ML_D, ML_D)
        zg = pcols(C_Z + h * ML_D, ML_D)
        y_ml = jax.nn.sigmoid(o) * hn * _silu(zg)
        mix_ref[0, :, h * ML_D:(h + 1) * ML_D] = y_ml.astype(mix_ref.dtype)
    m_out[0, 0] = m_row_new

    conv, tail = _causal_conv(ssd_ext, pcols(C_XBC, SSD_CONV_DIM), ssdcw_ref, SSD_CONV, L)
    ssdcv_out[0, 0] = tail
    xbc = _silu(conv + ssdcb_ref[...])
    xs_all = xbc[:, 0:BRANCH]
    cb = []
    bm_b, cm_b = [], []
    for g in range(SSD_GROUPS):
        bm_g = xbc[:, BRANCH + g * SSD_N:BRANCH + (g + 1) * SSD_N].astype(BF16)
        cm_g = xbc[:, BRANCH + (SSD_GROUPS + g) * SSD_N:BRANCH + (SSD_GROUPS + g + 1) * SSD_N].astype(BF16)
        bm_b.append(bm_g)
        cm_b.append(cm_g)
        cb.append(_dot_nt(cm_g, bm_g))
    rep = SSD_HEADS // SSD_GROUPS
    for h in range(SSD_HEADS):
        g = h // rep
        acs_col = cum[:, 8 + h:9 + h]
        acs_row = packed_t[8 + h:9 + h, :]
        dt_row = packed_t[16 + h:17 + h, :]
        dt_col = sp[:, 8 + h:9 + h]
        decay = jnp.exp(jnp.where(causal, acs_col - acs_row, -jnp.inf))
        scores = cb[g] * decay * dt_row
        xs_h = xs_all[:, h * SSD_P:(h + 1) * SSD_P]
        s_h = s_out[0, 0, h]
        y = (_dot(scores.astype(BF16), xs_h.astype(BF16))
             + jnp.exp(acs_col) * _dot_nt(cm_b[g], s_h.astype(BF16)))
        a_last = cum[L - 1:L, 8 + h:9 + h]
        w = jnp.exp(a_last - acs_col) * dt_col
        s_out[0, 0, h] = jnp.exp(a_last) * s_h + _dot_tn((xs_h * w).astype(BF16), bm_b[g])
        yssd_ref[:, h * SSD_P:(h + 1) * SSD_P] = y + drow_ref[:, h * SSD_P:(h + 1) * SSD_P] * xs_h
    yz = yssd_ref[...] * _silu(pcols(C_SSD_Z, BRANCH))
    gw = BRANCH // SSD_GROUPS
    for g in range(SSD_GROUPS):
        seg = yz[:, g * gw:(g + 1) * gw]
        seg = seg * lax.rsqrt(jnp.mean(seg * seg, axis=-1, keepdims=True) + EPS)
        y_ssd = seg * ssdnw_ref[:, g * gw:(g + 1) * gw]
        mix_ref[0, :, BRANCH + g * gw:BRANCH + (g + 1) * gw] = y_ssd.astype(mix_ref.dtype)

    u = pcols(C_SC_C, BRANCH) * pcols(C_SC_H, BRANCH)
    cu, tail = _causal_conv(sc_ext, u, sccw_ref, SC_WIDTH, L)
    sccv_out[0, 0] = tail
    y_sc = pcols(C_SC_B, BRANCH) * cu * _silu(pcols(C_SC_Z, BRANCH))
    mix_ref[0, :, 2 * BRANCH:3 * BRANCH] = y_sc.astype(mix_ref.dtype)

    conv, tail = _causal_conv(rg_ext, pcols(C_RG_X, BRANCH), rgcw_ref, RG_CONV, L)
    rgcv_out[0, 0] = tail
    xr = conv + rgcb_ref[...]
    pre = _dot(xr.astype(BF16), rgw_ref[...])
    r = jax.nn.sigmoid(pre[:, 0:BRANCH] + rgba_ref[...])
    ig = jax.nn.sigmoid(pre[:, BRANCH:2 * BRANCH] + rgbx_ref[...])
    log_a = (-RG_C * r) * _softplus(-rglam_ref[...])
    a = jnp.exp(log_a)
    bterm = jnp.sqrt(-jnp.tanh(log_a) * (a * a + 1.0)) * (ig * xr)
    h_rg = _linear_scan(a, bterm, rgh_out[0, 0], L)
    rgh_out[0, 0] = h_rg[L - 1:L, :]
    y_rg = h_rg * _silu(pcols(C_RG_Z, BRANCH))
    mix_ref[0, :, 3 * BRANCH:4 * BRANCH] = y_rg.astype(mix_ref.dtype)


def _mixer(proj, states, layer, params, rows, mix_dtype):
    nb, t, _ = proj.shape
    nc = t // rows
    sidx = lambda b, c: (layer, b, 0, 0)
    sidx5 = lambda b, c: (layer, b, 0, 0, 0)
    oidx = lambda b, c: (0, b, 0, 0)
    oidx5 = lambda b, c: (0, b, 0, 0, 0)
    state_blocks = [
        (1, 1, ML_HEADS, ML_D, ML_D), (1, 1, ML_HEADS, ML_D), (1, 1, 1, LANES),
        (1, 1, SSD_HEADS, SSD_P, SSD_N), (1, 1, SSD_CONV - 1, SSD_CONV_DIM),
        (1, 1, SC_WIDTH - 1, BRANCH), (1, 1, 1, BRANCH), (1, 1, RG_CONV - 1, BRANCH),
    ]
    in_specs = [pl.BlockSpec((1, rows, PROJ_W), lambda b, c: (b, c, 0))]
    in_specs += [pl.BlockSpec(blk, sidx5 if len(blk) == 5 else sidx) for blk in state_blocks]
    in_specs += [pl.BlockSpec(p.shape, lambda b, c: (0, 0)) for p in params]
    out_specs = [pl.BlockSpec((1, rows, D_MODEL), lambda b, c: (b, c, 0))]
    out_specs += [pl.BlockSpec(blk, oidx5 if len(blk) == 5 else oidx) for blk in state_blocks]
    out_shape = [jax.ShapeDtypeStruct((nb, t, D_MODEL), mix_dtype)]
    out_shape += [jax.ShapeDtypeStruct((1, nb) + blk[2:], F32) for blk in state_blocks]
    outs = pl.pallas_call(
        functools.partial(_mixer_kernel, rows),
        grid=(nb, nc),
        in_specs=in_specs,
        out_specs=out_specs,
        out_shape=out_shape,
        scratch_shapes=[
            pltpu.VMEM((CONV_BASE + rows, SSD_CONV_DIM), F32),
            pltpu.VMEM((CONV_BASE + rows, BRANCH), F32),
            pltpu.VMEM((CONV_BASE + rows, BRANCH), F32),
            pltpu.VMEM((rows, BRANCH), F32),
        ],
        compiler_params=pltpu.CompilerParams(
            dimension_semantics=("arbitrary", "arbitrary"), vmem_limit_bytes=VMEM_LIMIT),
        name=f"mixer_rows{rows}",
    )(proj, *states, *params)
    return outs[0], [o[0] for o in outs[1:]]


def _reorder_w_in(w_in):
    o_i, o_f, o_ssd_z, o_xbc, o_dt, o_sc, o_rg, o_end = 2560, 2564, 2568, 3080, 4104, 4112, 6160, 7184
    pad = jnp.zeros(w_in.shape[:2] + (LANES - 24,), w_in.dtype)
    dt = w_in[..., o_dt:o_sc]
    return jnp.concatenate(
        [w_in[..., 0:o_i], w_in[..., o_ssd_z:o_xbc], w_in[..., o_xbc:o_dt], w_in[..., o_sc:o_rg],
         w_in[..., o_rg:o_end], w_in[..., o_i:o_f], w_in[..., o_f:o_ssd_z], dt, dt, pad],
        axis=-1).astype(BF16)


def _block_diag(w):
    nl, nblk, d, e = w.shape
    eye = jnp.eye(nblk, dtype=w.dtype)
    return jnp.einsum('lnde,nm->lndme', w, eye).reshape(nl, nblk * d, nblk * e)


def _lane_row(parts, width=LANES):
    nl = parts[0][1].shape[0]
    row = jnp.zeros((nl, width), F32)
    for off, p in parts:
        row = lax.dynamic_update_slice(row, p.astype(F32), (0, off))
    return row[:, None, :]


def kernel(x_prompt, x_sample, state_mlstm_C, state_mlstm_n, state_mlstm_m, state_ssd, state_ssd_conv, state_sconv_conv, state_rglru_h, state_rglru_conv, norm_w, w_in, ml_i_bias, ml_f_bias, ml_norm_w, ssd_conv_w, ssd_conv_b, ssd_dt_bias, ssd_A_log, ssd_D, ssd_norm_w, sc_conv_w, rg_conv_w, rg_conv_b, rg_wa, rg_ba, rg_wx, rg_bx, rg_lambda, w_out, final_norm_w):
    bp, tp, _ = x_prompt.shape
    bs, ts, _ = x_sample.shape

    w_in_r = _reorder_w_in(w_in)
    w_out_b = w_out.astype(BF16)
    rg_w = jnp.concatenate([_block_diag(rg_wa), _block_diag(rg_wx)], axis=-1).astype(BF16)
    bias_row = _lane_row([(0, ml_i_bias), (4, ml_f_bias), (8, ssd_dt_bias), (16, ssd_dt_bias)])
    alog_row = _lane_row([(8, ssd_A_log)])
    d_row = jnp.repeat(ssd_D, SSD_P, axis=-1)[:, None, :]
    row = lambda p: p[:, None, :]

    def layer_params(l):
        return [bias_row[l], alog_row[l], row(ml_norm_w)[l], ssd_conv_w[l], row(ssd_conv_b)[l], d_row[l],
                row(ssd_norm_w)[l], sc_conv_w[l], rg_conv_w[l], row(rg_conv_b)[l], rg_w[l],
                row(rg_ba)[l], row(rg_bx)[l], row(rg_lambda)[l]]

    def pad_m(m):
        return jnp.pad(m, ((0, 0), (0, 0), (4, LANES - 4 - ML_HEADS)))[:, :, None, :]

    sample_states = [state_mlstm_C, state_mlstm_n, pad_m(state_mlstm_m), state_ssd, state_ssd_conv,
                     state_sconv_conv, state_rglru_h[:, :, None, :], state_rglru_conv]
    prompt_states = [
        jnp.zeros((1, bp, ML_HEADS, ML_D, ML_D), F32),
        jnp.zeros((1, bp, ML_HEADS, ML_D), F32),
        pad_m(jnp.full((1, bp, ML_HEADS), NEG_BIG, F32)),
        jnp.zeros((1, bp, SSD_HEADS, SSD_P, SSD_N), F32),
        jnp.zeros((1, bp, SSD_CONV - 1, SSD_CONV_DIM), F32),
        jnp.zeros((1, bp, SC_WIDTH - 1, BRANCH), F32),
        jnp.zeros((1, bp, 1, BRANCH), F32),
        jnp.zeros((1, bp, RG_CONV - 1, BRANCH), F32),
    ]

    xp = x_prompt.reshape(bp * tp, D_MODEL)
    xs = x_sample.reshape(bs * ts, D_MODEL)
    fnw = final_norm_w[None, :]
    new_p, new_s = [], []
    for l in range(DEPTH):
        params = layer_params(l)
        final = l == DEPTH - 1
        proj_p = _inproj(xp, norm_w[l][None, :], w_in_r[l], 512).reshape(bp, tp, PROJ_W)
        proj_s = _inproj(xs, norm_w[l][None, :], w_in_r[l], 512).reshape(bs, ts, PROJ_W)
        mix_p, st_p = _mixer(proj_p, prompt_states, 0, params, CHUNK, BF16)
        mix_s, st_s = _mixer(proj_s, sample_states, l, params, ts, F32)
        xp = _outproj(mix_p.reshape(bp * tp, D_MODEL), xp, w_out_b[l], fnw, final, 512)
        xs = _outproj(mix_s.reshape(bs * ts, D_MODEL), xs, w_out_b[l], fnw, final, 512)
        new_p.append(st_p)
        new_s.append(st_s)

    def stack(states, i):
        a = jnp.stack([st[i] for st in states], axis=0)
        if i == 2:
            return a[:, :, 0, 4:4 + ML_HEADS]
        if i == 6:
            return a[:, :, 0, :]
        return a

    return ((xp.reshape(bp, tp, D_MODEL), xs.reshape(bs, ts, D_MODEL))
            + tuple(stack(new_p, i) for i in range(8))
            + tuple(stack(new_s, i) for i in range(8)))
```

```python
import functools

import jax
import jax.numpy as jnp
from jax import lax
from jax.experimental import pallas as pl
from jax.experimental.pallas import tpu as pltpu

F32 = jnp.float32
BF16 = jnp.bfloat16

D_MODEL = 2048
DEPTH = 2
BRANCH = 512
CHUNK = 128
EPS = 1e-6
ML_HEADS = 4
ML_D = 128
NEG_BIG = -1e30
SSD_HEADS = 8
SSD_P = 64
SSD_GROUPS = 2
SSD_N = 128
SSD_CONV = 4
SSD_CONV_DIM = 1024
SC_WIDTH = 3
RG_CONV = 4
RG_C = 8.0
RG_BLOCKS = 8

C_Q, C_K, C_V, C_O, C_Z = 0, 512, 1024, 1536, 2048
C_SSD_Z, C_XBC = 2560, 3072
C_SC_B, C_SC_C, C_SC_H, C_SC_Z = 4096, 4608, 5120, 5632
C_RG_X, C_RG_Z = 6144, 6656
C_SMALL = 7168
PROJ_W = 7296
LANES = 128
CONV_BASE = 8

VMEM_LIMIT = 56 * 1024 * 1024

NT_DIMS = (((1,), (1,)), ((), ()))
TN_DIMS = (((0,), (0,)), ((), ()))


def _softplus(x):
    return jnp.maximum(x, 0.0) + jnp.log1p(jnp.exp(-jnp.abs(x)))


def _silu(x):
    return x * jax.nn.sigmoid(x)


def _dot(a, b):
    return jnp.dot(a, b, preferred_element_type=F32)


def _dot_nt(a, b):
    return lax.dot_general(a, b, NT_DIMS, preferred_element_type=F32)


def _dot_tn(a, b):
    return lax.dot_general(a, b, TN_DIMS, preferred_element_type=F32)


def _inproj_kernel(x_ref, nw_ref, w_ref, o_ref, xn_ref):
    @pl.when(pl.program_id(1) == 0)
    def _():
        x = x_ref[...]
        y = x * lax.rsqrt(jnp.mean(x * x, axis=-1, keepdims=True) + EPS)
        xn_ref[...] = (y * nw_ref[...]).astype(BF16)

    o_ref[...] = _dot(xn_ref[...], w_ref[...])


def _inproj(x, norm_w, w, layer, tm):
    m = x.shape[0]
    tn = PROJ_W // 3
    return pl.pallas_call(
        _inproj_kernel,
        grid=(m // tm, PROJ_W // tn),
        in_specs=[
            pl.BlockSpec((tm, D_MODEL), lambda i, j: (i, 0)),
            pl.BlockSpec((None, 1, D_MODEL), lambda i, j: (layer, 0, 0)),
            pl.BlockSpec((None, D_MODEL, tn), lambda i, j: (layer, 0, j)),
        ],
        out_specs=pl.BlockSpec((tm, tn), lambda i, j: (i, j)),
        out_shape=jax.ShapeDtypeStruct((m, PROJ_W), F32),
        scratch_shapes=[pltpu.VMEM((tm, D_MODEL), BF16)],
        compiler_params=pltpu.CompilerParams(
            dimension_semantics=("arbitrary", "arbitrary"), vmem_limit_bytes=VMEM_LIMIT),
        name="inproj",
    )(x, norm_w, w)


def _outproj_kernel(final, mix_ref, x_ref, w_ref, fnw_ref, o_ref):
    y = x_ref[...] + _dot(mix_ref[...].astype(BF16), w_ref[...])
    if final:
        y = y * lax.rsqrt(jnp.mean(y * y, axis=-1, keepdims=True) + EPS) * fnw_ref[...]
    o_ref[...] = y


def _outproj(mix, x, w, fnw, layer, final, tm):
    m = x.shape[0]
    return pl.pallas_call(
        functools.partial(_outproj_kernel, final),
        grid=(m // tm,),
        in_specs=[
            pl.BlockSpec((tm, D_MODEL), lambda i: (i, 0)),
            pl.BlockSpec((tm, D_MODEL), lambda i: (i, 0)),
            pl.BlockSpec((None, D_MODEL, D_MODEL), lambda i: (layer, 0, 0)),
            pl.BlockSpec((1, D_MODEL), lambda i: (0, 0)),
        ],
        out_specs=pl.BlockSpec((tm, D_MODEL), lambda i: (i, 0)),
        out_shape=jax.ShapeDtypeStruct((m, D_MODEL), F32),
        compiler_params=pltpu.CompilerParams(
            dimension_semantics=("arbitrary",), vmem_limit_bytes=VMEM_LIMIT),
        name="outproj",
    )(mix, x, w, fnw)


def _causal_conv(ext_ref, u, w_ref, width, rows):
    lo = CONV_BASE - (width - 1)
    ext_ref[CONV_BASE:CONV_BASE + rows, :] = u
    acc = ext_ref[lo:lo + rows, :] * w_ref[0:1, :]
    for j in range(1, width):
        acc = acc + ext_ref[lo + j:lo + j + rows, :] * w_ref[j:j + 1, :]
    tail = ext_ref[lo + rows:CONV_BASE + rows, :]
    ext_ref[lo:CONV_BASE, :] = tail
    return acc, tail


def _linear_scan(a, b, h0, rows):
    if rows < 8:
        h = h0
        out = []
        for t in range(rows):
            h = a[t:t + 1, :] * h + b[t:t + 1, :]
            out.append(h)
        return jnp.concatenate(out, axis=0)
    row = lax.broadcasted_iota(jnp.int32, a.shape, 0)
    d = 1
    while d < rows:
        keep = row >= d
        a_prev = jnp.where(keep, pltpu.roll(a, d, 0), 1.0)
        b_prev = jnp.where(keep, pltpu.roll(b, d, 0), 0.0)
        b = a * b_prev + b
        a = a * a_prev
        d *= 2
    return b + a * h0


N_STATES = 8
N_PARAMS = 14


def _mixer_kernel(rows, n_carried, *refs):
    L = rows
    proj_ref = refs[0]
    c_in, n_in, m_in, s_in, ssdcv_in, sccv_in, rgh_in, rgcv_in = refs[1:1 + N_STATES]
    (bias_ref, alog_ref, mlnw_ref, ssdcw_ref, ssdcb_ref, drow_ref, ssdnw_ref, sccw_ref, rgcw_ref,
     rgcb_ref, rgw_ref, rgba_ref, rgbx_ref, rglam_ref) = refs[1 + N_STATES:1 + N_STATES + N_PARAMS]
    outs = refs[1 + N_STATES + N_PARAMS + n_carried:]
    mix_ref, c_out, n_out, m_out, s_out, ssdcv_out, sccv_out, rgh_out, rgcv_out = outs[:1 + N_STATES]
    ssd_ext, sc_ext, rg_ext, yssd_ref = outs[1 + N_STATES:]

    @pl.when(pl.program_id(1) == 0)
    def _load_state():
        c_out[0:1] = c_in[...]
        n_out[0:1] = n_in[...]
        m_out[0:1] = m_in[...]
        s_out[0:1] = s_in[...]
        rgh_out[0:1] = rgh_in[...]
        ssd_ext[CONV_BASE - (SSD_CONV - 1):CONV_BASE, :] = ssdcv_in[0, 0]
        sc_ext[CONV_BASE - (SC_WIDTH - 1):CONV_BASE, :] = sccv_in[0, 0]
        rg_ext[CONV_BASE - (RG_CONV - 1):CONV_BASE, :] = rgcv_in[0, 0]
        for out in (c_out, n_out, m_out, s_out, ssdcv_out, sccv_out, rgh_out, rgcv_out):
            if out.shape[0] > 1:
                out[1:] = jnp.zeros((out.shape[0] - 1,) + out.shape[1:], out.dtype)

    def pcols(start, width):
        return proj_ref[0, :, start:start + width]

    small = pcols(C_SMALL, LANES) + bias_ref[...]
    lane = lax.broadcasted_iota(jnp.int32, (L, LANES), 1)
    sp = _softplus(small)
    logsig = -_softplus(-small)
    a_coef = -jnp.exp(alog_ref[...])
    z = jnp.where((lane >= 4) & (lane < 8), logsig,
                  jnp.where((lane >= 8) & (lane < 16), sp * a_coef, 0.0))
    rr = lax.broadcasted_iota(jnp.int32, (L, L), 0)
    cc = lax.broadcasted_iota(jnp.int32, (L, L), 1)
    causal = cc <= rr
    tril = causal.astype(F32)
    cum = jnp.dot(tril, z, precision=lax.Precision.HIGHEST, preferred_element_type=F32)
    packed = jnp.where(lane < 4, small,
                       jnp.where(lane < 16, cum, jnp.where(lane < 24, sp, 0.0)))
    er = lax.broadcasted_iota(jnp.int32, (LANES, LANES), 0)
    ec = lax.broadcasted_iota(jnp.int32, (LANES, LANES), 1)
    eye = (er == ec).astype(F32)
    packed_t = lax.dot_general(eye, packed, NT_DIMS, precision=lax.Precision.HIGHEST,
                               preferred_element_type=F32)

    m_row = m_out[0, 0]
    lane1 = lax.broadcasted_iota(jnp.int32, (1, LANES), 1)
    m_row_new = m_row
    for h in range(ML_HEADS):
        q = pcols(C_Q + h * ML_D, ML_D)
        k = pcols(C_K + h * ML_D, ML_D) * (ML_D ** -0.5)
        v = pcols(C_V + h * ML_D, ML_D)
        qb, kb, vb = q.astype(BF16), k.astype(BF16), v.astype(BF16)
        b_col = cum[:, 4 + h:5 + h]
        i_col = small[:, h:h + 1]
        b_row = packed_t[4 + h:5 + h, :]
        i_row = packed_t[h:h + 1, :]
        m_h = m_row[:, 4 + h:5 + h]
        dmat = jnp.where(causal, b_col - b_row + i_row, -jnp.inf)
        inter = b_col + m_h
        m_t = jnp.maximum(inter, jnp.max(dmat, axis=1, keepdims=True))
        w_intra = jnp.exp(dmat - m_t)
        w_inter = jnp.exp(inter - m_t)
        s = _dot_nt(qb, kb) * w_intra
        c_h = c_out[0, 0, h]
        n_h = n_out[0, 0, h:h + 1, :]
        num = _dot(s.astype(BF16), vb) + w_inter * _dot(qb, c_h.astype(BF16))
        den = jnp.sum(s, axis=1, keepdims=True) + w_inter * jnp.sum(q * n_h, axis=1, keepdims=True)
        hh = num / jnp.maximum(jnp.abs(den), jnp.exp(-m_t))
        b_last = cum[L - 1:L, 4 + h:5 + h]
        g = b_last - b_col + i_col
        inter_end = b_last + m_h
        m_new = jnp.maximum(inter_end, jnp.max(g, axis=0, keepdims=True))
        wg = jnp.exp(g - m_new)
        we = jnp.exp(inter_end - m_new)
        kw = k * wg
        c_out[0, 0, h] = we * c_h + _dot_tn(kw.astype(BF16), vb)
        n_out[0, 0, h:h + 1, :] = we * n_h + jnp.sum(kw, axis=0, keepdims=True)
        m_row_new = jnp.where(lane1 == 4 + h, m_new, m_row_new)
        mu = jnp.mean(hh, axis=-1, keepdims=True)
        hc = hh - mu
        var = jnp.mean(hc * hc, axis=-1, keepdims=True)
        hn = hc * lax.rsqrt(var + EPS) * mlnw_ref[:, h * ML_D:(h + 1) * ML_D]
        o = pcols(C_O + h * ML_D, ML_D)
        zg = pcols(C_Z + h * ML_D, ML_D)
        y_ml = jax.nn.sigmoid(o) * hn * _silu(zg)
        mix_ref[0, :, h * ML_D:(h + 1) * ML_D] = y_ml.astype(mix_ref.dtype)
    m_out[0, 0] = m_row_new

    conv, tail = _causal_conv(ssd_ext, pcols(C_XBC, SSD_CONV_DIM), ssdcw_ref, SSD_CONV, L)
    ssdcv_out[0, 0] = tail
    xbc = _silu(conv + ssdcb_ref[...])
    xs_all = xbc[:, 0:BRANCH]
    cb = []
    bm_b, cm_b = [], []
    for g in range(SSD_GROUPS):
        bm_g = xbc[:, BRANCH + g * SSD_N:BRANCH + (g + 1) * SSD_N].astype(BF16)
        cm_g = xbc[:, BRANCH + (SSD_GROUPS + g) * SSD_N:BRANCH + (SSD_GROUPS + g + 1) * SSD_N].astype(BF16)
        bm_b.append(bm_g)
        cm_b.append(cm_g)
        cb.append(_dot_nt(cm_g, bm_g))
    rep = SSD_HEADS // SSD_GROUPS
    for h in range(SSD_HEADS):
        g = h // rep
        acs_col = cum[:, 8 + h:9 + h]
        acs_row = packed_t[8 + h:9 + h, :]
        dt_row = packed_t[16 + h:17 + h, :]
        dt_col = sp[:, 8 + h:9 + h]
        decay = jnp.exp(jnp.where(causal, acs_col - acs_row, -jnp.inf))
        scores = cb[g] * decay * dt_row
        xs_h = xs_all[:, h * SSD_P:(h + 1) * SSD_P]
        s_h = s_out[0, 0, h]
        y = (_dot(scores.astype(BF16), xs_h.astype(BF16))
             + jnp.exp(acs_col) * _dot_nt(cm_b[g], s_h.astype(BF16)))
        a_last = cum[L - 1:L, 8 + h:9 + h]
        w = jnp.exp(a_last - acs_col) * dt_col
        s_out[0, 0, h] = jnp.exp(a_last) * s_h + _dot_tn((xs_h * w).astype(BF16), bm_b[g])
        yssd_ref[:, h * SSD_P:(h + 1) * SSD_P] = y + drow_ref[:, h * SSD_P:(h + 1) * SSD_P] * xs_h
    yz = yssd_ref[...] * _silu(pcols(C_SSD_Z, BRANCH))
    gw = BRANCH // SSD_GROUPS
    for g in range(SSD_GROUPS):
        seg = yz[:, g * gw:(g + 1) * gw]
        seg = seg * lax.rsqrt(jnp.mean(seg * seg, axis=-1, keepdims=True) + EPS)
        y_ssd = seg * ssdnw_ref[:, g * gw:(g + 1) * gw]
        mix_ref[0, :, BRANCH + g * gw:BRANCH + (g + 1) * gw] = y_ssd.astype(mix_ref.dtype)

    u = pcols(C_SC_C, BRANCH) * pcols(C_SC_H, BRANCH)
    cu, tail = _causal_conv(sc_ext, u, sccw_ref, SC_WIDTH, L)
    sccv_out[0, 0] = tail
    y_sc = pcols(C_SC_B, BRANCH) * cu * _silu(pcols(C_SC_Z, BRANCH))
    mix_ref[0, :, 2 * BRANCH:3 * BRANCH] = y_sc.astype(mix_ref.dtype)

    conv, tail = _causal_conv(rg_ext, pcols(C_RG_X, BRANCH), rgcw_ref, RG_CONV, L)
    rgcv_out[0, 0] = tail
    xr = conv + rgcb_ref[...]
    pre = _dot(xr.astype(BF16), rgw_ref[...])
    r = jax.nn.sigmoid(pre[:, 0:BRANCH] + rgba_ref[...])
    ig = jax.nn.sigmoid(pre[:, BRANCH:2 * BRANCH] + rgbx_ref[...])
    log_a = (-RG_C * r) * _softplus(-rglam_ref[...])
    a = jnp.exp(log_a)
    bterm = jnp.sqrt(-jnp.tanh(log_a) * (a * a + 1.0)) * (ig * xr)
    h_rg = _linear_scan(a, bterm, rgh_out[0, 0], L)
    rgh_out[0, 0] = h_rg[L - 1:L, :]
    y_rg = h_rg * _silu(pcols(C_RG_Z, BRANCH))
    mix_ref[0, :, 3 * BRANCH:4 * BRANCH] = y_rg.astype(mix_ref.dtype)


def _mixer(proj, states, state_layer, params, layer, carried, rows, mix_dtype):
    nb, t, _ = proj.shape
    nc = t // rows
    state_blocks = [
        (1, 1, ML_HEADS, ML_D, ML_D), (1, 1, ML_HEADS, ML_D), (1, 1, 1, LANES),
        (1, 1, SSD_HEADS, SSD_P, SSD_N), (1, 1, SSD_CONV - 1, SSD_CONV_DIM),
        (1, 1, SC_WIDTH - 1, BRANCH), (1, 1, 1, BRANCH), (1, 1, RG_CONV - 1, BRANCH),
    ]

    def at_layer(lyr, blk):
        tail = (0,) * (len(blk) - 2)
        return pl.BlockSpec(blk, lambda b, c: (lyr, b) + tail)

    carried = list(carried) if carried is not None else []
    in_specs = [pl.BlockSpec((1, rows, PROJ_W), lambda b, c: (b, c, 0))]
    in_specs += [at_layer(state_layer, blk) for blk in state_blocks]
    in_specs += [pl.BlockSpec((None,) + p.shape[1:], lambda b, c: (layer, 0, 0)) for p in params]
    in_specs += [pl.BlockSpec(memory_space=pl.ANY) for _ in carried]
    out_specs = [pl.BlockSpec((1, rows, D_MODEL), lambda b, c: (b, c, 0))]
    if carried:
        out_specs += [at_layer(layer, blk) for blk in state_blocks]
    else:
        assert layer == 0
        out_specs += [at_layer(0, (DEPTH,) + blk[1:]) for blk in state_blocks]
    out_shape = [jax.ShapeDtypeStruct((nb, t, D_MODEL), mix_dtype)]
    out_shape += [jax.ShapeDtypeStruct((DEPTH, nb) + blk[2:], F32) for blk in state_blocks]
    first_carried = 1 + N_STATES + N_PARAMS
    outs = pl.pallas_call(
        functools.partial(_mixer_kernel, rows, len(carried)),
        grid=(nb, nc),
        in_specs=in_specs,
        out_specs=out_specs,
        out_shape=out_shape,
        input_output_aliases={first_carried + i: 1 + i for i in range(len(carried))},
        scratch_shapes=[
            pltpu.VMEM((CONV_BASE + rows, SSD_CONV_DIM), F32),
            pltpu.VMEM((CONV_BASE + rows, BRANCH), F32),
            pltpu.VMEM((CONV_BASE + rows, BRANCH), F32),
            pltpu.VMEM((rows, BRANCH), F32),
        ],
        compiler_params=pltpu.CompilerParams(
            dimension_semantics=("arbitrary", "arbitrary"), vmem_limit_bytes=VMEM_LIMIT),
        name=f"mixer_rows{rows}",
    )(proj, *states, *params, *carried)
    return outs[0], list(outs[1:])


def _reorder_w_in(w_in):
    o_i, o_f, o_ssd_z, o_xbc, o_dt, o_sc, o_rg, o_end = 2560, 2564, 2568, 3080, 4104, 4112, 6160, 7184
    pad = jnp.zeros(w_in.shape[:2] + (LANES - 24,), w_in.dtype)
    dt = w_in[..., o_dt:o_sc]
    return jnp.concatenate(
        [w_in[..., 0:o_i], w_in[..., o_ssd_z:o_xbc], w_in[..., o_xbc:o_dt], w_in[..., o_sc:o_rg],
         w_in[..., o_rg:o_end], w_in[..., o_i:o_f], w_in[..., o_f:o_ssd_z], dt, dt, pad],
        axis=-1).astype(BF16)


def _block_diag(w):
    nl, nblk, d, e = w.shape
    eye = jnp.eye(nblk, dtype=w.dtype)
    return jnp.einsum('lnde,nm->lndme', w, eye).reshape(nl, nblk * d, nblk * e)


def _lane_row(parts, width=LANES):
    nl = parts[0][1].shape[0]
    row = jnp.zeros((nl, width), F32)
    for off, p in parts:
        row = lax.dynamic_update_slice(row, p.astype(F32), (0, off))
    return row[:, None, :]


def kernel(x_prompt, x_sample, state_mlstm_C, state_mlstm_n, state_mlstm_m, state_ssd, state_ssd_conv, state_sconv_conv, state_rglru_h, state_rglru_conv, norm_w, w_in, ml_i_bias, ml_f_bias, ml_norm_w, ssd_conv_w, ssd_conv_b, ssd_dt_bias, ssd_A_log, ssd_D, ssd_norm_w, sc_conv_w, rg_conv_w, rg_conv_b, rg_wa, rg_ba, rg_wx, rg_bx, rg_lambda, w_out, final_norm_w):
    bp, tp, _ = x_prompt.shape
    bs, ts, _ = x_sample.shape

    w_in_r = _reorder_w_in(w_in)
    w_out_b = w_out.astype(BF16)
    rg_w = jnp.concatenate([_block_diag(rg_wa), _block_diag(rg_wx)], axis=-1).astype(BF16)
    bias_row = _lane_row([(0, ml_i_bias), (4, ml_f_bias), (8, ssd_dt_bias), (16, ssd_dt_bias)])
    alog_row = _lane_row([(8, ssd_A_log)])
    d_row = jnp.repeat(ssd_D, SSD_P, axis=-1)[:, None, :]
    row = lambda p: p[:, None, :]
    params = [bias_row, alog_row, row(ml_norm_w), ssd_conv_w, row(ssd_conv_b), d_row, row(ssd_norm_w),
              sc_conv_w, rg_conv_w, row(rg_conv_b), rg_w, row(rg_ba), row(rg_bx), row(rg_lambda)]
    norm_rows = row(norm_w)

    def pad_m(m):
        return jnp.pad(m, ((0, 0), (0, 0), (4, LANES - 4 - ML_HEADS)))[:, :, None, :]

    sample_states = [state_mlstm_C, state_mlstm_n, pad_m(state_mlstm_m), state_ssd, state_ssd_conv,
                     state_sconv_conv, state_rglru_h[:, :, None, :], state_rglru_conv]
    prompt_states = [
        jnp.zeros((1, bp, ML_HEADS, ML_D, ML_D), F32),
        jnp.zeros((1, bp, ML_HEADS, ML_D), F32),
        pad_m(jnp.full((1, bp, ML_HEADS), NEG_BIG, F32)),
        jnp.zeros((1, bp, SSD_HEADS, SSD_P, SSD_N), F32),
        jnp.zeros((1, bp, SSD_CONV - 1, SSD_CONV_DIM), F32),
        jnp.zeros((1, bp, SC_WIDTH - 1, BRANCH), F32),
        jnp.zeros((1, bp, 1, BRANCH), F32),
        jnp.zeros((1, bp, RG_CONV - 1, BRANCH), F32),
    ]

    xp = x_prompt.reshape(bp * tp, D_MODEL)
    xs = x_sample.reshape(bs * ts, D_MODEL)
    fnw = final_norm_w[None, :]
    st_p, st_s = None, None
    for l in range(DEPTH):
        final = l == DEPTH - 1
        proj_p = _inproj(xp, norm_rows, w_in_r, l, 512).reshape(bp, tp, PROJ_W)
        proj_s = _inproj(xs, norm_rows, w_in_r, l, 512).reshape(bs, ts, PROJ_W)
        mix_p, st_p = _mixer(proj_p, prompt_states, 0, params, l, st_p, CHUNK, BF16)
        mix_s, st_s = _mixer(proj_s, sample_states, l, params, l, st_s, ts, F32)
        xp = _outproj(mix_p.reshape(bp * tp, D_MODEL), xp, w_out_b, fnw, l, final, 512)
        xs = _outproj(mix_s.reshape(bs * ts, D_MODEL), xs, w_out_b, fnw, l, final, 512)

    def unpack(states):
        states = list(states)
        states[2] = states[2][:, :, 0, 4:4 + ML_HEADS]
        states[6] = states[6][:, :, 0, :]
        return tuple(states)

    return ((xp.reshape(bp, tp, D_MODEL), xs.reshape(bs, ts, D_MODEL))
            + unpack(st_p) + unpack(st_s))
```

```python
import functools

import jax
import jax.numpy as jnp
from jax import lax
from jax.experimental import pallas as pl
from jax.experimental.pallas import tpu as pltpu

F32 = jnp.float32
BF16 = jnp.bfloat16

D_MODEL = 2048
DEPTH = 2
BRANCH = 512
CHUNK = 128
EPS = 1e-6
ML_HEADS = 4
ML_D = 128
NEG_BIG = -1e30
SSD_HEADS = 8
SSD_P = 64
SSD_GROUPS = 2
SSD_N = 128
SSD_CONV = 4
SSD_CONV_DIM = 1024
SC_WIDTH = 3
RG_CONV = 4
RG_C = 8.0
RG_BLOCKS = 8

C_Q, C_K, C_V, C_O, C_Z = 0, 512, 1024, 1536, 2048
C_SSD_Z, C_XBC = 2560, 3072
C_SC_B, C_SC_C, C_SC_H, C_SC_Z = 4096, 4608, 5120, 5632
C_RG_X, C_RG_Z = 6144, 6656
C_SMALL = 7168
PROJ_W = 7296
PROJ_TILE = 256
LANES = 128
CONV_BASE = 8

V7X_VMEM_BYTES = 64 * 1024 * 1024
VMEM_LIMIT = V7X_VMEM_BYTES - 4 * 1024 * 1024

NT_DIMS = (((1,), (1,)), ((), ()))
TN_DIMS = (((0,), (0,)), ((), ()))

N_STATES = 8
N_PARAMS = 14
STATE_BLOCKS = [
    (1, 1, ML_HEADS, ML_D, ML_D), (1, 1, ML_HEADS, ML_D), (1, 1, 1, LANES),
    (1, 1, SSD_HEADS, SSD_P, SSD_N), (1, 1, SSD_CONV - 1, SSD_CONV_DIM),
    (1, 1, SC_WIDTH - 1, BRANCH), (1, 1, 1, BRANCH), (1, 1, RG_CONV - 1, BRANCH),
]


def _softplus(x):
    return jnp.maximum(x, 0.0) + jnp.log1p(jnp.exp(-jnp.abs(x)))


def _silu(x):
    return x * jax.nn.sigmoid(x)


def _dot(a, b):
    return jnp.dot(a, b, preferred_element_type=F32)


def _dot_nt(a, b):
    return lax.dot_general(a, b, NT_DIMS, preferred_element_type=F32)


def _dot_tn(a, b):
    return lax.dot_general(a, b, TN_DIMS, preferred_element_type=F32)


def _rms_norm(x, w):
    return x * lax.rsqrt(jnp.mean(x * x, axis=-1, keepdims=True) + EPS) * w


def _inproj_kernel(x_ref, nw_ref, w_ref, o_ref, xn_ref):
    @pl.when(pl.program_id(1) == 0)
    def _():
        xn_ref[...] = _rms_norm(x_ref[...], nw_ref[...]).astype(BF16)

    o_ref[...] = _dot(xn_ref[...], w_ref[...])


def _inproj(x, norm_w, w, layer, tm):
    m = x.shape[0]
    tn = PROJ_W // 3
    return pl.pallas_call(
        _inproj_kernel,
        grid=(m // tm, PROJ_W // tn),
        in_specs=[
            pl.BlockSpec((tm, D_MODEL), lambda i, j: (i, 0)),
            pl.BlockSpec((None, 1, D_MODEL), lambda i, j: (layer, 0, 0)),
            pl.BlockSpec((None, D_MODEL, tn), lambda i, j: (layer, 0, j)),
        ],
        out_specs=pl.BlockSpec((tm, tn), lambda i, j: (i, j)),
        out_shape=jax.ShapeDtypeStruct((m, PROJ_W), F32),
        scratch_shapes=[pltpu.VMEM((tm, D_MODEL), BF16)],
        compiler_params=pltpu.CompilerParams(
            dimension_semantics=("arbitrary", "arbitrary"), vmem_limit_bytes=VMEM_LIMIT),
        name="inproj",
    )(x, norm_w, w)


def _outproj_kernel(final, mix_ref, x_ref, w_ref, fnw_ref, o_ref):
    y = x_ref[...] + _dot(mix_ref[...].astype(BF16), w_ref[...])
    if final:
        y = _rms_norm(y, fnw_ref[...])
    o_ref[...] = y


def _outproj(mix, x, w, fnw, layer, final, tm):
    m = x.shape[0]
    return pl.pallas_call(
        functools.partial(_outproj_kernel, final),
        grid=(m // tm,),
        in_specs=[
            pl.BlockSpec((tm, D_MODEL), lambda i: (i, 0)),
            pl.BlockSpec((tm, D_MODEL), lambda i: (i, 0)),
            pl.BlockSpec((None, D_MODEL, D_MODEL), lambda i: (layer, 0, 0)),
            pl.BlockSpec((1, D_MODEL), lambda i: (0, 0)),
        ],
        out_specs=pl.BlockSpec((tm, D_MODEL), lambda i: (i, 0)),
        out_shape=jax.ShapeDtypeStruct((m, D_MODEL), F32),
        compiler_params=pltpu.CompilerParams(
            dimension_semantics=("arbitrary",), vmem_limit_bytes=VMEM_LIMIT),
        name="outproj",
    )(mix, x, w, fnw)


def _causal_conv(ext_ref, u, w_ref, width, rows):
    lo = CONV_BASE - (width - 1)
    ext_ref[CONV_BASE:CONV_BASE + rows, :] = u
    acc = ext_ref[lo:lo + rows, :] * w_ref[0:1, :]
    for j in range(1, width):
        acc = acc + ext_ref[lo + j:lo + j + rows, :] * w_ref[j:j + 1, :]
    tail = ext_ref[lo + rows:CONV_BASE + rows, :]
    ext_ref[lo:CONV_BASE, :] = tail
    return acc, tail


def _linear_scan(a, b, h0, rows):
    if rows < 8:
        h = h0
        out = []
        for t in range(rows):
            h = a[t:t + 1, :] * h + b[t:t + 1, :]
            out.append(h)
        return jnp.concatenate(out, axis=0)
    row = lax.broadcasted_iota(jnp.int32, a.shape, 0)
    d = 1
    while d < rows:
        keep = row >= d
        a_prev = jnp.where(keep, pltpu.roll(a, d, 0), 1.0)
        b_prev = jnp.where(keep, pltpu.roll(b, d, 0), 0.0)
        b = a * b_prev + b
        a = a * a_prev
        d *= 2
    return b + a * h0


def _load_state(state_in, state_out, ssd_ext, sc_ext, rg_ext):
    c_in, n_in, m_in, s_in, ssdcv_in, sccv_in, rgh_in, rgcv_in = state_in
    c_out, n_out, m_out, s_out, ssdcv_out, sccv_out, rgh_out, rgcv_out = state_out
    c_out[0:1] = c_in[...]
    n_out[0:1] = n_in[...]
    m_out[0:1] = m_in[...]
    s_out[0:1] = s_in[...]
    rgh_out[0:1] = rgh_in[...]
    ssd_ext[CONV_BASE - (SSD_CONV - 1):CONV_BASE, :] = ssdcv_in[0, 0]
    sc_ext[CONV_BASE - (SC_WIDTH - 1):CONV_BASE, :] = sccv_in[0, 0]
    rg_ext[CONV_BASE - (RG_CONV - 1):CONV_BASE, :] = rgcv_in[0, 0]
    for out in state_out:
        if out.shape[0] > 1:
            out[1:] = jnp.zeros((out.shape[0] - 1,) + out.shape[1:], out.dtype)


def _gate_prep(L, pcols, bias_ref, alog_ref):
    small = pcols(C_SMALL, LANES) + bias_ref[...]
    lane = lax.broadcasted_iota(jnp.int32, (L, LANES), 1)
    sp = _softplus(small)
    logsig = -_softplus(-small)
    a_coef = -jnp.exp(alog_ref[...])
    z = jnp.where((lane >= 4) & (lane < 8), logsig,
                  jnp.where((lane >= 8) & (lane < 16), sp * a_coef, 0.0))
    rr = lax.broadcasted_iota(jnp.int32, (L, L), 0)
    cc = lax.broadcasted_iota(jnp.int32, (L, L), 1)
    causal = cc <= rr
    cum = jnp.dot(causal.astype(F32), z, precision=lax.Precision.HIGHEST, preferred_element_type=F32)
    packed = jnp.where(lane < 4, small,
                       jnp.where(lane < 16, cum, jnp.where(lane < 24, sp, 0.0)))
    er = lax.broadcasted_iota(jnp.int32, (LANES, LANES), 0)
    ec = lax.broadcasted_iota(jnp.int32, (LANES, LANES), 1)
    eye = (er == ec).astype(F32)
    packed_t = lax.dot_general(eye, packed, NT_DIMS, precision=lax.Precision.HIGHEST,
                               preferred_element_type=F32)
    return small, sp, cum, packed_t, causal


def _group_mlstm(L, pcols, mix_store, gates, c_out, n_out, m_out, mlnw_ref):
    small, _, cum, packed_t, causal = gates
    m_row = m_out[0, 0]
    lane1 = lax.broadcasted_iota(jnp.int32, (1, LANES), 1)
    m_row_new = m_row
    for h in range(ML_HEADS):
        q = pcols(C_Q + h * ML_D, ML_D)
        k = pcols(C_K + h * ML_D, ML_D) * (ML_D ** -0.5)
        v = pcols(C_V + h * ML_D, ML_D)
        qb, kb, vb = q.astype(BF16), k.astype(BF16), v.astype(BF16)
        b_col = cum[:, 4 + h:5 + h]
        i_col = small[:, h:h + 1]
        b_row = packed_t[4 + h:5 + h, :]
        i_row = packed_t[h:h + 1, :]
        m_h = m_row[:, 4 + h:5 + h]
        dmat = jnp.where(causal, b_col - b_row + i_row, -jnp.inf)
        inter = b_col + m_h
        m_t = jnp.maximum(inter, jnp.max(dmat, axis=1, keepdims=True))
        w_intra = jnp.exp(dmat - m_t)
        w_inter = jnp.exp(inter - m_t)
        s = _dot_nt(qb, kb) * w_intra
        c_h = c_out[0, 0, h]
        n_h = n_out[0, 0, h:h + 1, :]
        num = _dot(s.astype(BF16), vb) + w_inter * _dot(qb, c_h.astype(BF16))
        den = jnp.sum(s, axis=1, keepdims=True) + w_inter * jnp.sum(q * n_h, axis=1, keepdims=True)
        hh = num / jnp.maximum(jnp.abs(den), jnp.exp(-m_t))
        b_last = cum[L - 1:L, 4 + h:5 + h]
        g = b_last - b_col + i_col
        inter_end = b_last + m_h
        m_new = jnp.maximum(inter_end, jnp.max(g, axis=0, keepdims=True))
        wg = jnp.exp(g - m_new)
        we = jnp.exp(inter_end - m_new)
        kw = k * wg
        c_out[0, 0, h] = we * c_h + _dot_tn(kw.astype(BF16), vb)
        n_out[0, 0, h:h + 1, :] = we * n_h + jnp.sum(kw, axis=0, keepdims=True)
        m_row_new = jnp.where(lane1 == 4 + h, m_new, m_row_new)
        mu = jnp.mean(hh, axis=-1, keepdims=True)
        hc = hh - mu
        var = jnp.mean(hc * hc, axis=-1, keepdims=True)
        hn = hc * lax.rsqrt(var + EPS) * mlnw_ref[:, h * ML_D:(h + 1) * ML_D]
        o = pcols(C_O + h * ML_D, ML_D)
        zg = pcols(C_Z + h * ML_D, ML_D)
        mix_store(h * ML_D, jax.nn.sigmoid(o) * hn * _silu(zg))
        if h == ML_HEADS - 1:
            m_out[0, 0] = m_row_new
        yield


def _group_ssd(L, pcols, mix_store, gates, s_out, ssdcv_out, ssd_ext, yssd_ref,
               ssdcw_ref, ssdcb_ref, drow_ref, ssdnw_ref):
    _, sp, cum, packed_t, causal = gates
    conv, tail = _causal_conv(ssd_ext, pcols(C_XBC, SSD_CONV_DIM), ssdcw_ref, SSD_CONV, L)
    ssdcv_out[0, 0] = tail
    xbc = _silu(conv + ssdcb_ref[...])
    xs_all = xbc[:, 0:BRANCH]
    cb, bm_b, cm_b = [], [], []
    for g in range(SSD_GROUPS):
        bm_g = xbc[:, BRANCH + g * SSD_N:BRANCH + (g + 1) * SSD_N].astype(BF16)
        cm_g = xbc[:, BRANCH + (SSD_GROUPS + g) * SSD_N:BRANCH + (SSD_GROUPS + g + 1) * SSD_N].astype(BF16)
        bm_b.append(bm_g)
        cm_b.append(cm_g)
        cb.append(_dot_nt(cm_g, bm_g))
    yield
    rep = SSD_HEADS // SSD_GROUPS
    for h in range(SSD_HEADS):
        g = h // rep
        acs_col = cum[:, 8 + h:9 + h]
        acs_row = packed_t[8 + h:9 + h, :]
        dt_row = packed_t[16 + h:17 + h, :]
        dt_col = sp[:, 8 + h:9 + h]
        decay = jnp.exp(jnp.where(causal, acs_col - acs_row, -jnp.inf))
        scores = cb[g] * decay * dt_row
        xs_h = xs_all[:, h * SSD_P:(h + 1) * SSD_P]
        s_h = s_out[0, 0, h]
        y = (_dot(scores.astype(BF16), xs_h.astype(BF16))
             + jnp.exp(acs_col) * _dot_nt(cm_b[g], s_h.astype(BF16)))
        a_last = cum[L - 1:L, 8 + h:9 + h]
        w = jnp.exp(a_last - acs_col) * dt_col
        s_out[0, 0, h] = jnp.exp(a_last) * s_h + _dot_tn((xs_h * w).astype(BF16), bm_b[g])
        yssd_ref[:, h * SSD_P:(h + 1) * SSD_P] = y + drow_ref[:, h * SSD_P:(h + 1) * SSD_P] * xs_h
        yield
    yz = yssd_ref[...] * _silu(pcols(C_SSD_Z, BRANCH))
    gw = BRANCH // SSD_GROUPS
    for g in range(SSD_GROUPS):
        seg = yz[:, g * gw:(g + 1) * gw]
        seg = seg * lax.rsqrt(jnp.mean(seg * seg, axis=-1, keepdims=True) + EPS)
        mix_store(BRANCH + g * gw, seg * ssdnw_ref[:, g * gw:(g + 1) * gw])
    yield


def _group_sconv(L, pcols, mix_store, sccv_out, sc_ext, sccw_ref):
    u = pcols(C_SC_C, BRANCH) * pcols(C_SC_H, BRANCH)
    cu, tail = _causal_conv(sc_ext, u, sccw_ref, SC_WIDTH, L)
    sccv_out[0, 0] = tail
    mix_store(2 * BRANCH, pcols(C_SC_B, BRANCH) * cu * _silu(pcols(C_SC_Z, BRANCH)))
    yield


def _group_rglru(L, pcols, mix_store, rgh_out, rgcv_out, rg_ext,
                 rgcw_ref, rgcb_ref, rgw_ref, rgba_ref, rgbx_ref, rglam_ref):
    conv, tail = _causal_conv(rg_ext, pcols(C_RG_X, BRANCH), rgcw_ref, RG_CONV, L)
    rgcv_out[0, 0] = tail
    xr = conv + rgcb_ref[...]
    pre = _dot(xr.astype(BF16), rgw_ref[...])
    r = jax.nn.sigmoid(pre[:, 0:BRANCH] + rgba_ref[...])
    ig = jax.nn.sigmoid(pre[:, BRANCH:2 * BRANCH] + rgbx_ref[...])
    log_a = (-RG_C * r) * _softplus(-rglam_ref[...])
    a = jnp.exp(log_a)
    bterm = jnp.sqrt(-jnp.tanh(log_a) * (a * a + 1.0)) * (ig * xr)
    yield
    h_rg = _linear_scan(a, bterm, rgh_out[0, 0], L)
    rgh_out[0, 0] = h_rg[L - 1:L, :]
    mix_store(3 * BRANCH, h_rg * _silu(pcols(C_RG_Z, BRANCH)))
    yield


def _split_refs(refs, n_lead, n_carried, n_out_lead):
    lead = refs[:n_lead]
    state_in = refs[n_lead:n_lead + N_STATES]
    params = refs[n_lead + N_STATES:n_lead + N_STATES + N_PARAMS]
    rest = refs[n_lead + N_STATES + N_PARAMS + n_carried:]
    out_lead = rest[:n_out_lead]
    state_out = rest[n_out_lead:n_out_lead + N_STATES]
    scratch = rest[n_out_lead + N_STATES:]
    return lead, state_in, params, out_lead, state_out, scratch


def _mixer_kernel(rows, n_carried, *refs):
    (proj_ref,), state_in, params, (mix_ref,), state_out, scratch = _split_refs(refs, 1, n_carried, 1)
    (bias_ref, alog_ref, mlnw_ref, ssdcw_ref, ssdcb_ref, drow_ref, ssdnw_ref, sccw_ref, rgcw_ref,
     rgcb_ref, rgw_ref, rgba_ref, rgbx_ref, rglam_ref) = params
    c_out, n_out, m_out, s_out, ssdcv_out, sccv_out, rgh_out, rgcv_out = state_out
    ssd_ext, sc_ext, rg_ext, yssd_ref = scratch
    L = rows

    @pl.when(pl.program_id(1) == 0)
    def _():
        _load_state(state_in, state_out, ssd_ext, sc_ext, rg_ext)

    def pcols(start, width):
        return proj_ref[0, :, start:start + width]

    def mix_store(start, val):
        mix_ref[0, :, start:start + val.shape[1]] = val.astype(mix_ref.dtype)

    gates = _gate_prep(L, pcols, bias_ref, alog_ref)
    groups = [
        _group_mlstm(L, pcols, mix_store, gates, c_out, n_out, m_out, mlnw_ref),
        _group_ssd(L, pcols, mix_store, gates, s_out, ssdcv_out, ssd_ext, yssd_ref,
                   ssdcw_ref, ssdcb_ref, drow_ref, ssdnw_ref),
        _group_sconv(L, pcols, mix_store, sccv_out, sc_ext, sccw_ref),
        _group_rglru(L, pcols, mix_store, rgh_out, rgcv_out, rg_ext,
                     rgcw_ref, rgcb_ref, rgw_ref, rgba_ref, rgbx_ref, rglam_ref),
    ]
    for group in groups:
        for _ in group:
            pass


def _state_specs(state_layer, layer, carried):
    def at_layer(lyr, blk):
        tail = (0,) * (len(blk) - 2)
        return pl.BlockSpec(blk, lambda b, c: (lyr, b) + tail)

    in_specs = [at_layer(state_layer, blk) for blk in STATE_BLOCKS]
    if carried:
        out_specs = [at_layer(layer, blk) for blk in STATE_BLOCKS]
    else:
        assert layer == 0
        out_specs = [at_layer(0, (DEPTH,) + blk[1:]) for blk in STATE_BLOCKS]
    return in_specs, out_specs


def _state_shapes(nb):
    return [jax.ShapeDtypeStruct((DEPTH, nb) + blk[2:], F32) for blk in STATE_BLOCKS]


def _mix_scratch(rows):
    return [
        pltpu.VMEM((CONV_BASE + rows, SSD_CONV_DIM), F32),
        pltpu.VMEM((CONV_BASE + rows, BRANCH), F32),
        pltpu.VMEM((CONV_BASE + rows, BRANCH), F32),
        pltpu.VMEM((rows, BRANCH), F32),
    ]


def _mixer(proj, states, state_layer, params, layer, carried, rows, mix_dtype):
    nb, t, _ = proj.shape
    carried = list(carried) if carried is not None else []
    st_in, st_out = _state_specs(state_layer, layer, carried)
    in_specs = [pl.BlockSpec((1, rows, PROJ_W), lambda b, c: (b, c, 0))] + st_in
    in_specs += [pl.BlockSpec((None,) + p.shape[1:], lambda b, c: (layer, 0, 0)) for p in params]
    in_specs += [pl.BlockSpec(memory_space=pl.ANY) for _ in carried]
    first_carried = 1 + N_STATES + N_PARAMS
    outs = pl.pallas_call(
        functools.partial(_mixer_kernel, rows, len(carried)),
        grid=(nb, t // rows),
        in_specs=in_specs,
        out_specs=[pl.BlockSpec((1, rows, D_MODEL), lambda b, c: (b, c, 0))] + st_out,
        out_shape=[jax.ShapeDtypeStruct((nb, t, D_MODEL), mix_dtype)] + _state_shapes(nb),
        input_output_aliases={first_carried + i: 1 + i for i in range(len(carried))},
        scratch_shapes=_mix_scratch(rows),
        compiler_params=pltpu.CompilerParams(
            dimension_semantics=("arbitrary", "arbitrary"), vmem_limit_bytes=VMEM_LIMIT),
        name=f"mixer_rows{rows}",
    )(proj, *states, *params, *carried)
    return outs[0], list(outs[1:])


def _layer_kernel(final, n_carried, *refs):
    lead, state_in, params, (y_ref,), state_out, scratch = _split_refs(refs, 6, n_carried, 1)
    x_ref, xnext_ref, nw_ref, win_ref, wout_ref, fnw_ref = lead
    (bias_ref, alog_ref, mlnw_ref, ssdcw_ref, ssdcb_ref, drow_ref, ssdnw_ref, sccw_ref, rgcw_ref,
     rgcb_ref, rgw_ref, rgba_ref, rgbx_ref, rglam_ref) = params
    c_out, n_out, m_out, s_out, ssdcv_out, sccv_out, rgh_out, rgcv_out = state_out
    proj_ref, mix_ref, ssd_ext, sc_ext, rg_ext, yssd_ref = scratch
    L = CHUNK
    step = pl.program_id(0) * pl.num_programs(1) + pl.program_id(1)
    cur = step % 2
    nxt = 1 - cur

    @pl.when(pl.program_id(1) == 0)
    def _():
        _load_state(state_in, state_out, ssd_ext, sc_ext, rg_ext)

    @pl.when(step == 0)
    def _():
        xn0 = _rms_norm(x_ref[0], nw_ref[...]).astype(BF16)
        for start in range(0, PROJ_W, PROJ_TILE):
            stop = min(start + PROJ_TILE, PROJ_W)
            proj_ref[0, :, start:stop] = _dot(xn0, win_ref[:, start:stop])

    x = x_ref[0]
    xn_next = _rms_norm(xnext_ref[0], nw_ref[...]).astype(BF16)
    tiles = [(s, min(s + PROJ_TILE, PROJ_W)) for s in range(0, PROJ_W, PROJ_TILE)]
    n_pieces = 18

    def project_some(piece):
        lo = (piece * len(tiles)) // n_pieces
        hi = ((piece + 1) * len(tiles)) // n_pieces
        for start, stop in tiles[lo:hi]:
            proj_ref[nxt, :, start:stop] = _dot(xn_next, win_ref[:, start:stop])

    def pcols(start, width):
        return proj_ref[cur, :, start:start + width]

    def mix_store(start, val):
        mix_ref[:, start:start + val.shape[1]] = val.astype(BF16)

    def out_project(group):
        lo, hi = group * BRANCH, (group + 1) * BRANCH
        return _dot(mix_ref[:, lo:hi], wout_ref[lo:hi, :])

    project_some(0)
    gates = _gate_prep(L, pcols, bias_ref, alog_ref)
    groups = [
        _group_mlstm(L, pcols, mix_store, gates, c_out, n_out, m_out, mlnw_ref),
        _group_ssd(L, pcols, mix_store, gates, s_out, ssdcv_out, ssd_ext, yssd_ref,
                   ssdcw_ref, ssdcb_ref, drow_ref, ssdnw_ref),
        _group_sconv(L, pcols, mix_store, sccv_out, sc_ext, sccw_ref),
        _group_rglru(L, pcols, mix_store, rgh_out, rgcv_out, rg_ext,
                     rgcw_ref, rgcb_ref, rgw_ref, rgba_ref, rgbx_ref, rglam_ref),
    ]
    piece = 1
    y = x
    for gi, group in enumerate(groups):
        for _ in group:
            project_some(piece)
            piece += 1
        y = y + out_project(gi)
    assert piece == n_pieces
    if final:
        y = _rms_norm(y, fnw_ref[...])
    y_ref[0] = y


def _layer_prompt(x, norm_w, w_in, w_out, fnw, states, params, layer, carried, final):
    nb, t, _ = x.shape
    nc = t // CHUNK
    carried = list(carried) if carried is not None else []
    st_in, st_out = _state_specs(0, layer, carried)
    resident = dict(pipeline_mode=pl.Buffered(1))

    def next_chunk(b, c):
        step = jnp.minimum(b * nc + c + 1, nb * nc - 1)
        return (step // nc, step % nc, 0)

    in_specs = [
        pl.BlockSpec((1, CHUNK, D_MODEL), lambda b, c: (b, c, 0)),
        pl.BlockSpec((1, CHUNK, D_MODEL), next_chunk),
        pl.BlockSpec((None, 1, D_MODEL), lambda b, c: (layer, 0, 0)),
        pl.BlockSpec((None, D_MODEL, PROJ_W), lambda b, c: (layer, 0, 0), **resident),
        pl.BlockSpec((None, D_MODEL, D_MODEL), lambda b, c: (layer, 0, 0), **resident),
        pl.BlockSpec((1, D_MODEL), lambda b, c: (0, 0)),
    ] + st_in
    in_specs += [pl.BlockSpec((None,) + p.shape[1:], lambda b, c: (layer, 0, 0)) for p in params]
    in_specs += [pl.BlockSpec(memory_space=pl.ANY) for _ in carried]
    first_carried = 6 + N_STATES + N_PARAMS
    outs = pl.pallas_call(
        functools.partial(_layer_kernel, final, len(carried)),
        grid=(nb, nc),
        in_specs=in_specs,
        out_specs=[pl.BlockSpec((1, CHUNK, D_MODEL), lambda b, c: (b, c, 0))] + st_out,
        out_shape=[jax.ShapeDtypeStruct((nb, t, D_MODEL), F32)] + _state_shapes(nb),
        input_output_aliases={first_carried + i: 1 + i for i in range(len(carried))},
        scratch_shapes=[pltpu.VMEM((2, CHUNK, PROJ_W), F32), pltpu.VMEM((CHUNK, D_MODEL), BF16)]
        + _mix_scratch(CHUNK),
        compiler_params=pltpu.CompilerParams(
            dimension_semantics=("arbitrary", "arbitrary"), vmem_limit_bytes=VMEM_LIMIT),
        name="layer_prompt",
    )(x, x, norm_w, w_in, w_out, fnw, *states, *params, *carried)
    return outs[0], list(outs[1:])


def _reorder_w_in(w_in):
    o_i, o_f, o_ssd_z, o_xbc, o_dt, o_sc, o_rg, o_end = 2560, 2564, 2568, 3080, 4104, 4112, 6160, 7184
    pad = jnp.zeros(w_in.shape[:2] + (LANES - 24,), w_in.dtype)
    dt = w_in[..., o_dt:o_sc]
    return jnp.concatenate(
        [w_in[..., 0:o_i], w_in[..., o_ssd_z:o_xbc], w_in[..., o_xbc:o_dt], w_in[..., o_sc:o_rg],
         w_in[..., o_rg:o_end], w_in[..., o_i:o_f], w_in[..., o_f:o_ssd_z], dt, dt, pad],
        axis=-1).astype(BF16)


def _block_diag(w):
    nl, nblk, d, e = w.shape
    eye = jnp.eye(nblk, dtype=w.dtype)
    return jnp.einsum('lnde,nm->lndme', w, eye).reshape(nl, nblk * d, nblk * e)


def _lane_row(parts, width=LANES):
    nl = parts[0][1].shape[0]
    row = jnp.zeros((nl, width), F32)
    for off, p in parts:
        row = lax.dynamic_update_slice(row, p.astype(F32), (0, off))
    return row[:, None, :]


def kernel(x_prompt, x_sample, state_mlstm_C, state_mlstm_n, state_mlstm_m, state_ssd, state_ssd_conv, state_sconv_conv, state_rglru_h, state_rglru_conv, norm_w, w_in, ml_i_bias, ml_f_bias, ml_norm_w, ssd_conv_w, ssd_conv_b, ssd_dt_bias, ssd_A_log, ssd_D, ssd_norm_w, sc_conv_w, rg_conv_w, rg_conv_b, rg_wa, rg_ba, rg_wx, rg_bx, rg_lambda, w_out, final_norm_w):
    bp, tp, _ = x_prompt.shape
    bs, ts, _ = x_sample.shape

    w_in_r = _reorder_w_in(w_in)
    w_out_b = w_out.astype(BF16)
    rg_w = jnp.concatenate([_block_diag(rg_wa), _block_diag(rg_wx)], axis=-1).astype(BF16)
    bias_row = _lane_row([(0, ml_i_bias), (4, ml_f_bias), (8, ssd_dt_bias), (16, ssd_dt_bias)])
    alog_row = _lane_row([(8, ssd_A_log)])
    d_row = jnp.repeat(ssd_D, SSD_P, axis=-1)[:, None, :]
    row = lambda p: p[:, None, :]
    params = [bias_row, alog_row, row(ml_norm_w), ssd_conv_w, row(ssd_conv_b), d_row, row(ssd_norm_w),
              sc_conv_w, rg_conv_w, row(rg_conv_b), rg_w, row(rg_ba), row(rg_bx), row(rg_lambda)]
    norm_rows = row(norm_w)

    def pad_m(m):
        return jnp.pad(m, ((0, 0), (0, 0), (4, LANES - 4 - ML_HEADS)))[:, :, None, :]

    sample_states = [state_mlstm_C, state_mlstm_n, pad_m(state_mlstm_m), state_ssd, state_ssd_conv,
                     state_sconv_conv, state_rglru_h[:, :, None, :], state_rglru_conv]
    prompt_states = [
        jnp.zeros((1, bp, ML_HEADS, ML_D, ML_D), F32),
        jnp.zeros((1, bp, ML_HEADS, ML_D), F32),
        pad_m(jnp.full((1, bp, ML_HEADS), NEG_BIG, F32)),
        jnp.zeros((1, bp, SSD_HEADS, SSD_P, SSD_N), F32),
        jnp.zeros((1, bp, SSD_CONV - 1, SSD_CONV_DIM), F32),
        jnp.zeros((1, bp, SC_WIDTH - 1, BRANCH), F32),
        jnp.zeros((1, bp, 1, BRANCH), F32),
        jnp.zeros((1, bp, RG_CONV - 1, BRANCH), F32),
    ]

    xp = x_prompt
    xs = x_sample.reshape(bs * ts, D_MODEL)
    fnw = final_norm_w[None, :]
    st_p, st_s = None, None
    for l in range(DEPTH):
        final = l == DEPTH - 1
        xp, st_p = _layer_prompt(xp, norm_rows, w_in_r, w_out_b, fnw, prompt_states, params, l, st_p, final)
        proj_s = _inproj(xs, norm_rows, w_in_r, l, 512).reshape(bs, ts, PROJ_W)
        mix_s, st_s = _mixer(proj_s, sample_states, l, params, l, st_s, ts, F32)
        xs = _outproj(mix_s.reshape(bs * ts, D_MODEL), xs, w_out_b, fnw, l, final, 512)

    def unpack(states):
        states = list(states)
        states[2] = states[2][:, :, 0, 4:4 + ML_HEADS]
        states[6] = states[6][:, :, 0, :]
        return tuple(states)

    return (xp, xs.reshape(bs, ts, D_MODEL)) + unpack(st_p) + unpack(st_s)
```

```python
import functools

import jax
import jax.numpy as jnp
from jax import lax
from jax.experimental import pallas as pl
from jax.experimental.pallas import tpu as pltpu

F32 = jnp.float32
BF16 = jnp.bfloat16

D_MODEL = 2048
DEPTH = 2
BRANCH = 512
CHUNK = 128
EPS = 1e-6
ML_HEADS = 4
ML_D = 128
NEG_BIG = -1e30
SSD_HEADS = 8
SSD_P = 64
SSD_GROUPS = 2
SSD_N = 128
SSD_CONV = 4
SSD_CONV_DIM = 1024
SC_WIDTH = 3
RG_CONV = 4
RG_C = 8.0
RG_BLOCKS = 8

C_Q, C_K, C_V, C_O, C_Z = 0, 512, 1024, 1536, 2048
C_SSD_Z, C_XBC = 2560, 3072
C_SC_B, C_SC_C, C_SC_H, C_SC_Z = 4096, 4608, 5120, 5632
C_RG_X, C_RG_Z = 6144, 6656
C_SMALL = 7168
PROJ_W = 7296
PROJ_TILE = 256
LANES = 128
CONV_BASE = 8
SAMPLE_SEQS = 8

V7X_VMEM_BYTES = 64 * 1024 * 1024
VMEM_LIMIT = V7X_VMEM_BYTES - 4 * 1024 * 1024

NT_DIMS = (((1,), (1,)), ((), ()))
TN_DIMS = (((0,), (0,)), ((), ()))

N_STATES = 8
N_PARAMS = 14
STATE_BLOCKS = [
    (1, 1, ML_HEADS, ML_D, ML_D), (1, 1, ML_HEADS, ML_D), (1, 1, 1, LANES),
    (1, 1, SSD_HEADS, SSD_P, SSD_N), (1, 1, SSD_CONV - 1, SSD_CONV_DIM),
    (1, 1, SC_WIDTH - 1, BRANCH), (1, 1, 1, BRANCH), (1, 1, RG_CONV - 1, BRANCH),
]


def _softplus(x):
    return jnp.maximum(x, 0.0) + jnp.log1p(jnp.exp(-jnp.abs(x)))


def _silu(x):
    return x * jax.nn.sigmoid(x)


def _dot(a, b):
    return jnp.dot(a, b, preferred_element_type=F32)


def _dot_nt(a, b):
    return lax.dot_general(a, b, NT_DIMS, preferred_element_type=F32)


def _dot_tn(a, b):
    return lax.dot_general(a, b, TN_DIMS, preferred_element_type=F32)


def _rms_norm(x, w):
    return x * lax.rsqrt(jnp.mean(x * x, axis=-1, keepdims=True) + EPS) * w


def _inproj_kernel(x_ref, nw_ref, w_ref, o_ref, xn_ref):
    @pl.when(pl.program_id(1) == 0)
    def _():
        xn_ref[...] = _rms_norm(x_ref[...], nw_ref[...]).astype(BF16)

    o_ref[...] = _dot(xn_ref[...], w_ref[...])


def _inproj(x, norm_w, w, layer, tm):
    m = x.shape[0]
    tn = PROJ_W // 3
    return pl.pallas_call(
        _inproj_kernel,
        grid=(m // tm, PROJ_W // tn),
        in_specs=[
            pl.BlockSpec((tm, D_MODEL), lambda i, j: (i, 0)),
            pl.BlockSpec((None, 1, D_MODEL), lambda i, j: (layer, 0, 0)),
            pl.BlockSpec((None, D_MODEL, tn), lambda i, j: (layer, 0, j)),
        ],
        out_specs=pl.BlockSpec((tm, tn), lambda i, j: (i, j)),
        out_shape=jax.ShapeDtypeStruct((m, PROJ_W), F32),
        scratch_shapes=[pltpu.VMEM((tm, D_MODEL), BF16)],
        compiler_params=pltpu.CompilerParams(
            dimension_semantics=("arbitrary", "arbitrary"), vmem_limit_bytes=VMEM_LIMIT),
        name="inproj",
    )(x, norm_w, w)


def _outproj_kernel(final, mix_ref, x_ref, w_ref, fnw_ref, o_ref):
    y = x_ref[...] + _dot(mix_ref[...].astype(BF16), w_ref[...])
    if final:
        y = _rms_norm(y, fnw_ref[...])
    o_ref[...] = y


def _outproj(mix, x, w, fnw, layer, final, tm):
    m = x.shape[0]
    return pl.pallas_call(
        functools.partial(_outproj_kernel, final),
        grid=(m // tm,),
        in_specs=[
            pl.BlockSpec((tm, D_MODEL), lambda i: (i, 0)),
            pl.BlockSpec((tm, D_MODEL), lambda i: (i, 0)),
            pl.BlockSpec((None, D_MODEL, D_MODEL), lambda i: (layer, 0, 0)),
            pl.BlockSpec((1, D_MODEL), lambda i: (0, 0)),
        ],
        out_specs=pl.BlockSpec((tm, D_MODEL), lambda i: (i, 0)),
        out_shape=jax.ShapeDtypeStruct((m, D_MODEL), F32),
        compiler_params=pltpu.CompilerParams(
            dimension_semantics=("arbitrary",), vmem_limit_bytes=VMEM_LIMIT),
        name="outproj",
    )(mix, x, w, fnw)


def _causal_conv(ext_ref, u, w_ref, width, rows):
    lo = CONV_BASE - (width - 1)
    ext_ref[CONV_BASE:CONV_BASE + rows, :] = u
    acc = ext_ref[lo:lo + rows, :] * w_ref[0:1, :]
    for j in range(1, width):
        acc = acc + ext_ref[lo + j:lo + j + rows, :] * w_ref[j:j + 1, :]
    tail = ext_ref[lo + rows:CONV_BASE + rows, :]
    ext_ref[lo:CONV_BASE, :] = tail
    return acc, tail


def _linear_scan(a, b, h0, rows):
    if rows < 8:
        h = h0
        out = []
        for t in range(rows):
            h = a[t:t + 1, :] * h + b[t:t + 1, :]
            out.append(h)
        return jnp.concatenate(out, axis=0)
    row = lax.broadcasted_iota(jnp.int32, a.shape, 0)
    d = 1
    while d < rows:
        keep = row >= d
        a_prev = jnp.where(keep, pltpu.roll(a, d, 0), 1.0)
        b_prev = jnp.where(keep, pltpu.roll(b, d, 0), 0.0)
        b = a * b_prev + b
        a = a * a_prev
        d *= 2
    return b + a * h0


def _cumsum_rows(z, rows):
    if rows < 8:
        out = [z[0:1, :]]
        for t in range(1, rows):
            out.append(out[-1] + z[t:t + 1, :])
        return jnp.concatenate(out, axis=0)
    row = lax.broadcasted_iota(jnp.int32, z.shape, 0)
    d = 1
    while d < rows:
        z = z + jnp.where(row >= d, pltpu.roll(z, d, 0), 0.0)
        d *= 2
    return z


def _load_state(state_in, state_out, ssd_ext, sc_ext, rg_ext):
    c_in, n_in, m_in, s_in, ssdcv_in, sccv_in, rgh_in, rgcv_in = state_in
    c_out, n_out, m_out, s_out, ssdcv_out, sccv_out, rgh_out, rgcv_out = state_out
    c_out[0:1] = c_in[...]
    n_out[0:1] = n_in[...]
    m_out[0:1] = m_in[...]
    s_out[0:1] = s_in[...]
    rgh_out[0:1] = rgh_in[...]
    for g in range(ssd_ext.shape[0]):
        ssd_ext[g, CONV_BASE - (SSD_CONV - 1):CONV_BASE, :] = ssdcv_in[0, g]
        sc_ext[g, CONV_BASE - (SC_WIDTH - 1):CONV_BASE, :] = sccv_in[0, g]
        rg_ext[g, CONV_BASE - (RG_CONV - 1):CONV_BASE, :] = rgcv_in[0, g]
    for out in state_out:
        if out.shape[0] > 1:
            out[1:] = jnp.zeros((out.shape[0] - 1,) + out.shape[1:], out.dtype)


def _gate_prep(L, pcols, bias_ref, alog_ref):
    small = pcols(C_SMALL, LANES) + bias_ref[...]
    lane = lax.broadcasted_iota(jnp.int32, (L, LANES), 1)
    sp = _softplus(small)
    logsig = -_softplus(-small)
    a_coef = -jnp.exp(alog_ref[...])
    z = jnp.where((lane >= 4) & (lane < 8), logsig,
                  jnp.where((lane >= 8) & (lane < 16), sp * a_coef, 0.0))
    rr = lax.broadcasted_iota(jnp.int32, (L, L), 0)
    cc = lax.broadcasted_iota(jnp.int32, (L, L), 1)
    causal = cc <= rr
    cum = _cumsum_rows(z, L)
    packed = jnp.where(lane < 4, small,
                       jnp.where(lane < 16, cum, jnp.where(lane < 24, sp, 0.0)))
    packed_t = packed.T
    return small, sp, cum, packed_t, causal


def _group_mlstm(L, pcols, mix_store, gates, c_out, n_out, m_out, mlnw_ref):
    small, _, cum, packed_t, causal = gates
    m_row = m_out[...]
    lane1 = lax.broadcasted_iota(jnp.int32, (1, LANES), 1)
    m_row_new = m_row
    for h in range(ML_HEADS):
        q = pcols(C_Q + h * ML_D, ML_D)
        k = pcols(C_K + h * ML_D, ML_D) * (ML_D ** -0.5)
        v = pcols(C_V + h * ML_D, ML_D)
        qb, kb, vb = q.astype(BF16), k.astype(BF16), v.astype(BF16)
        b_col = cum[:, 4 + h:5 + h]
        i_col = small[:, h:h + 1]
        b_row = packed_t[4 + h:5 + h, :]
        i_row = packed_t[h:h + 1, :]
        m_h = m_row[:, 4 + h:5 + h]
        dmat = jnp.where(causal, b_col - b_row + i_row, -jnp.inf)
        inter = b_col + m_h
        m_t = jnp.maximum(inter, jnp.max(dmat, axis=1, keepdims=True))
        w_intra = jnp.exp(dmat - m_t)
        w_inter = jnp.exp(inter - m_t)
        s = _dot_nt(qb, kb) * w_intra
        c_h = c_out[h]
        n_h = n_out[h:h + 1, :]
        num = _dot(s.astype(BF16), vb) + w_inter * _dot(qb, c_h.astype(BF16))
        den = jnp.sum(s, axis=1, keepdims=True) + w_inter * jnp.sum(q * n_h, axis=1, keepdims=True)
        hh = num / jnp.maximum(jnp.abs(den), jnp.exp(-m_t))
        b_last = cum[L - 1:L, 4 + h:5 + h]
        g = b_last - b_col + i_col
        inter_end = b_last + m_h
        m_new = jnp.maximum(inter_end, jnp.max(g, axis=0, keepdims=True))
        wg = jnp.exp(g - m_new)
        we = jnp.exp(inter_end - m_new)
        kw = k * wg
        c_out[h] = we * c_h + _dot_tn(kw.astype(BF16), vb)
        n_out[h:h + 1, :] = we * n_h + jnp.sum(kw, axis=0, keepdims=True)
        m_row_new = jnp.where(lane1 == 4 + h, m_new, m_row_new)
        mu = jnp.mean(hh, axis=-1, keepdims=True)
        hc = hh - mu
        var = jnp.mean(hc * hc, axis=-1, keepdims=True)
        hn = hc * lax.rsqrt(var + EPS) * mlnw_ref[:, h * ML_D:(h + 1) * ML_D]
        o = pcols(C_O + h * ML_D, ML_D)
        zg = pcols(C_Z + h * ML_D, ML_D)
        mix_store(h * ML_D, jax.nn.sigmoid(o) * hn * _silu(zg))
        if h == ML_HEADS - 1:
            m_out[...] = m_row_new
        yield


def _group_ssd(L, pcols, mix_store, gates, s_out, ssdcv_out, ssd_ext, yssd_ref,
               ssdcw_ref, ssdcb_ref, drow_ref, ssdnw_ref):
    _, sp, cum, packed_t, causal = gates
    conv, tail = _causal_conv(ssd_ext, pcols(C_XBC, SSD_CONV_DIM), ssdcw_ref, SSD_CONV, L)
    ssdcv_out[...] = tail
    xbc = _silu(conv + ssdcb_ref[...])
    xs_all = xbc[:, 0:BRANCH]
    cb, bm_b, cm_b = [], [], []
    for g in range(SSD_GROUPS):
        bm_g = xbc[:, BRANCH + g * SSD_N:BRANCH + (g + 1) * SSD_N].astype(BF16)
        cm_g = xbc[:, BRANCH + (SSD_GROUPS + g) * SSD_N:BRANCH + (SSD_GROUPS + g + 1) * SSD_N].astype(BF16)
        bm_b.append(bm_g)
        cm_b.append(cm_g)
        cb.append(_dot_nt(cm_g, bm_g))
    yield
    rep = SSD_HEADS // SSD_GROUPS
    for h in range(SSD_HEADS):
        g = h // rep
        acs_col = cum[:, 8 + h:9 + h]
        acs_row = packed_t[8 + h:9 + h, :]
        dt_row = packed_t[16 + h:17 + h, :]
        dt_col = sp[:, 8 + h:9 + h]
        decay = jnp.exp(jnp.where(causal, acs_col - acs_row, -jnp.inf))
        scores = cb[g] * decay * dt_row
        xs_h = xs_all[:, h * SSD_P:(h + 1) * SSD_P]
        s_h = s_out[h]
        y = (_dot(scores.astype(BF16), xs_h.astype(BF16))
             + jnp.exp(acs_col) * _dot_nt(cm_b[g], s_h.astype(BF16)))
        a_last = cum[L - 1:L, 8 + h:9 + h]
        w = jnp.exp(a_last - acs_col) * dt_col
        s_out[h] = jnp.exp(a_last) * s_h + _dot_tn((xs_h * w).astype(BF16), bm_b[g])
        yssd_ref[:, h * SSD_P:(h + 1) * SSD_P] = y + drow_ref[:, h * SSD_P:(h + 1) * SSD_P] * xs_h
        yield
    yz = yssd_ref[...] * _silu(pcols(C_SSD_Z, BRANCH))
    gw = BRANCH // SSD_GROUPS
    for g in range(SSD_GROUPS):
        seg = yz[:, g * gw:(g + 1) * gw]
        seg = seg * lax.rsqrt(jnp.mean(seg * seg, axis=-1, keepdims=True) + EPS)
        mix_store(BRANCH + g * gw, seg * ssdnw_ref[:, g * gw:(g + 1) * gw])
    yield


def _group_sconv(L, pcols, mix_store, sccv_out, sc_ext, sccw_ref):
    u = pcols(C_SC_C, BRANCH) * pcols(C_SC_H, BRANCH)
    cu, tail = _causal_conv(sc_ext, u, sccw_ref, SC_WIDTH, L)
    sccv_out[...] = tail
    mix_store(2 * BRANCH, pcols(C_SC_B, BRANCH) * cu * _silu(pcols(C_SC_Z, BRANCH)))
    yield


def _group_rglru(L, pcols, mix_store, rgh_out, rgcv_out, rg_ext,
                 rgcw_ref, rgcb_ref, rgw_ref, rgba_ref, rgbx_ref, rglam_ref):
    conv, tail = _causal_conv(rg_ext, pcols(C_RG_X, BRANCH), rgcw_ref, RG_CONV, L)
    rgcv_out[...] = tail
    xr = conv + rgcb_ref[...]
    pre = _dot(xr.astype(BF16), rgw_ref[...])
    r = jax.nn.sigmoid(pre[:, 0:BRANCH] + rgba_ref[...])
    ig = jax.nn.sigmoid(pre[:, BRANCH:2 * BRANCH] + rgbx_ref[...])
    log_a = (-RG_C * r) * _softplus(-rglam_ref[...])
    a = jnp.exp(log_a)
    bterm = jnp.sqrt(-jnp.tanh(log_a) * (a * a + 1.0)) * (ig * xr)
    yield
    h_rg = _linear_scan(a, bterm, rgh_out[...], L)
    rgh_out[...] = h_rg[L - 1:L, :]
    mix_store(3 * BRANCH, h_rg * _silu(pcols(C_RG_Z, BRANCH)))
    yield


def _split_refs(refs, n_lead, n_carried, n_out_lead):
    lead = refs[:n_lead]
    state_in = refs[n_lead:n_lead + N_STATES]
    params = refs[n_lead + N_STATES:n_lead + N_STATES + N_PARAMS]
    rest = refs[n_lead + N_STATES + N_PARAMS + n_carried:]
    out_lead = rest[:n_out_lead]
    state_out = rest[n_out_lead:n_out_lead + N_STATES]
    scratch = rest[n_out_lead + N_STATES:]
    return lead, state_in, params, out_lead, state_out, scratch


def _seq_groups(L, seq, pcols, mix_store, state_out, scratch, params):
    (bias_ref, alog_ref, mlnw_ref, ssdcw_ref, ssdcb_ref, drow_ref, ssdnw_ref, sccw_ref, rgcw_ref,
     rgcb_ref, rgw_ref, rgba_ref, rgbx_ref, rglam_ref) = params
    c_out, n_out, m_out, s_out, ssdcv_out, sccv_out, rgh_out, rgcv_out = [r.at[0, seq] for r in state_out]
    ssd_ext, sc_ext, rg_ext, yssd_ref = [r.at[seq] for r in scratch]
    gates = _gate_prep(L, pcols, bias_ref, alog_ref)
    return [
        _group_mlstm(L, pcols, mix_store, gates, c_out, n_out, m_out, mlnw_ref),
        _group_ssd(L, pcols, mix_store, gates, s_out, ssdcv_out, ssd_ext, yssd_ref,
                   ssdcw_ref, ssdcb_ref, drow_ref, ssdnw_ref),
        _group_sconv(L, pcols, mix_store, sccv_out, sc_ext, sccw_ref),
        _group_rglru(L, pcols, mix_store, rgh_out, rgcv_out, rg_ext,
                     rgcw_ref, rgcb_ref, rgw_ref, rgba_ref, rgbx_ref, rglam_ref),
    ]


def _mixer_kernel(rows, nseq, n_carried, *refs):
    (proj_ref,), state_in, params, (mix_ref,), state_out, scratch = _split_refs(refs, 1, n_carried, 1)

    @pl.when(pl.program_id(1) == 0)
    def _():
        _load_state(state_in, state_out, *scratch[:3])

    def chain(seq):
        def pcols(start, width):
            return proj_ref[seq, :, start:start + width]

        def mix_store(start, val):
            mix_ref[seq, :, start:start + val.shape[1]] = val.astype(mix_ref.dtype)

        for group in _seq_groups(rows, seq, pcols, mix_store, state_out, scratch, params):
            yield from group

    chains = [chain(seq) for seq in range(nseq)]
    while chains:
        chains = [c for c in chains if next(c, StopIteration) is not StopIteration]


def _state_specs(state_layer, layer, carried, nseq):
    def at_layer(lyr, blk):
        blk = (blk[0], nseq) + blk[2:]
        tail = (0,) * (len(blk) - 2)
        return pl.BlockSpec(blk, lambda b, c: (lyr, b) + tail)

    in_specs = [at_layer(state_layer, blk) for blk in STATE_BLOCKS]
    if carried:
        out_specs = [at_layer(layer, blk) for blk in STATE_BLOCKS]
    else:
        assert layer == 0
        out_specs = [at_layer(0, (DEPTH,) + blk[1:]) for blk in STATE_BLOCKS]
    return in_specs, out_specs


def _state_shapes(nb):
    return [jax.ShapeDtypeStruct((DEPTH, nb) + blk[2:], F32) for blk in STATE_BLOCKS]


def _mix_scratch(rows, nseq):
    return [
        pltpu.VMEM((nseq, CONV_BASE + rows, SSD_CONV_DIM), F32),
        pltpu.VMEM((nseq, CONV_BASE + rows, BRANCH), F32),
        pltpu.VMEM((nseq, CONV_BASE + rows, BRANCH), F32),
        pltpu.VMEM((nseq, rows, BRANCH), F32),
    ]


def _mixer(proj, states, state_layer, params, layer, carried, rows, nseq, mix_dtype):
    nb, t, _ = proj.shape
    carried = list(carried) if carried is not None else []
    st_in, st_out = _state_specs(state_layer, layer, carried, nseq)
    in_specs = [pl.BlockSpec((nseq, rows, PROJ_W), lambda b, c: (b, c, 0))] + st_in
    in_specs += [pl.BlockSpec((None,) + p.shape[1:], lambda b, c: (layer, 0, 0)) for p in params]
    in_specs += [pl.BlockSpec(memory_space=pl.ANY) for _ in carried]
    first_carried = 1 + N_STATES + N_PARAMS
    outs = pl.pallas_call(
        functools.partial(_mixer_kernel, rows, nseq, len(carried)),
        grid=(nb // nseq, t // rows),
        in_specs=in_specs,
        out_specs=[pl.BlockSpec((nseq, rows, D_MODEL), lambda b, c: (b, c, 0))] + st_out,
        out_shape=[jax.ShapeDtypeStruct((nb, t, D_MODEL), mix_dtype)] + _state_shapes(nb),
        input_output_aliases={first_carried + i: 1 + i for i in range(len(carried))},
        scratch_shapes=_mix_scratch(rows, nseq),
        compiler_params=pltpu.CompilerParams(
            dimension_semantics=("arbitrary", "arbitrary"), vmem_limit_bytes=VMEM_LIMIT),
        name=f"mixer_rows{rows}",
    )(proj, *states, *params, *carried)
    return outs[0], list(outs[1:])


def _layer_kernel(final, n_carried, *refs):
    lead, state_in, params, (y_ref,), state_out, scratch = _split_refs(refs, 6, n_carried, 1)
    x_ref, xnext_ref, nw_ref, win_ref, wout_ref, fnw_ref = lead
    proj_ref, mix_ref = scratch[:2]
    mix_scratch = scratch[2:]
    step = pl.program_id(0) * pl.num_programs(1) + pl.program_id(1)
    cur = step % 2
    nxt = 1 - cur

    @pl.when(pl.program_id(1) == 0)
    def _():
        _load_state(state_in, state_out, *mix_scratch[:3])

    @pl.when(step == 0)
    def _():
        xn0 = _rms_norm(x_ref[0], nw_ref[...]).astype(BF16)
        for start in range(0, PROJ_W, PROJ_TILE):
            stop = min(start + PROJ_TILE, PROJ_W)
            proj_ref[0, :, start:stop] = _dot(xn0, win_ref[:, start:stop])

    x = x_ref[0]
    xn_next = _rms_norm(xnext_ref[0], nw_ref[...]).astype(BF16)
    tiles = [(s, min(s + PROJ_TILE, PROJ_W)) for s in range(0, PROJ_W, PROJ_TILE)]
    n_pieces = 18

    def project_some(piece):
        lo = (piece * len(tiles)) // n_pieces
        hi = ((piece + 1) * len(tiles)) // n_pieces
        for start, stop in tiles[lo:hi]:
            proj_ref[nxt, :, start:stop] = _dot(xn_next, win_ref[:, start:stop])

    def pcols(start, width):
        return proj_ref[cur, :, start:start + width]

    def mix_store(start, val):
        mix_ref[:, start:start + val.shape[1]] = val.astype(BF16)

    def out_project(group):
        lo, hi = group * BRANCH, (group + 1) * BRANCH
        return _dot(mix_ref[:, lo:hi], wout_ref[lo:hi, :])

    project_some(0)
    groups = _seq_groups(CHUNK, 0, pcols, mix_store, state_out, mix_scratch, params)
    piece = 1
    y = x
    for gi, group in enumerate(groups):
        for _ in group:
            project_some(piece)
            piece += 1
        y = y + out_project(gi)
    assert piece == n_pieces
    if final:
        y = _rms_norm(y, fnw_ref[...])
    y_ref[0] = y


def _layer_prompt(x, norm_w, w_in, w_out, fnw, states, params, layer, carried, final):
    nb, t, _ = x.shape
    nc = t // CHUNK
    carried = list(carried) if carried is not None else []
    st_in, st_out = _state_specs(0, layer, carried, 1)
    resident = dict(pipeline_mode=pl.Buffered(1))

    def next_chunk(b, c):
        step = jnp.minimum(b * nc + c + 1, nb * nc - 1)
        return (step // nc, step % nc, 0)

    in_specs = [
        pl.BlockSpec((1, CHUNK, D_MODEL), lambda b, c: (b, c, 0)),
        pl.BlockSpec((1, CHUNK, D_MODEL), next_chunk),
        pl.BlockSpec((None, 1, D_MODEL), lambda b, c: (layer, 0, 0)),
        pl.BlockSpec((None, D_MODEL, PROJ_W), lambda b, c: (layer, 0, 0), **resident),
        pl.BlockSpec((None, D_MODEL, D_MODEL), lambda b, c: (layer, 0, 0), **resident),
        pl.BlockSpec((1, D_MODEL), lambda b, c: (0, 0)),
    ] + st_in
    in_specs += [pl.BlockSpec((None,) + p.shape[1:], lambda b, c: (layer, 0, 0)) for p in params]
    in_specs += [pl.BlockSpec(memory_space=pl.ANY) for _ in carried]
    first_carried = 6 + N_STATES + N_PARAMS
    outs = pl.pallas_call(
        functools.partial(_layer_kernel, final, len(carried)),
        grid=(nb, nc),
        in_specs=in_specs,
        out_specs=[pl.BlockSpec((1, CHUNK, D_MODEL), lambda b, c: (b, c, 0))] + st_out,
        out_shape=[jax.ShapeDtypeStruct((nb, t, D_MODEL), F32)] + _state_shapes(nb),
        input_output_aliases={first_carried + i: 1 + i for i in range(len(carried))},
        scratch_shapes=[pltpu.VMEM((2, CHUNK, PROJ_W), F32), pltpu.VMEM((CHUNK, D_MODEL), BF16)]
        + _mix_scratch(CHUNK, 1),
        compiler_params=pltpu.CompilerParams(
            dimension_semantics=("arbitrary", "arbitrary"), vmem_limit_bytes=VMEM_LIMIT),
        name="layer_prompt",
    )(x, x, norm_w, w_in, w_out, fnw, *states, *params, *carried)
    return outs[0], list(outs[1:])


O_I, O_SSD_Z, O_XBC, O_DT, O_SC, O_RG, O_END = 2560, 2568, 3080, 4104, 4112, 6160, 7184
W_IN_ROWS = 256


def _reorder_kernel(w_ref, o_ref):
    o_ref[:, C_Q:C_SSD_Z] = w_ref[:, 0:O_I].astype(BF16)
    o_ref[:, C_SSD_Z:C_SC_B] = w_ref[:, O_SSD_Z:O_DT].astype(BF16)
    o_ref[:, C_SC_B:C_SMALL] = w_ref[:, O_SC:O_END].astype(BF16)
    gates_if = w_ref[:, O_I:O_I + LANES]
    gates_dt = w_ref[:, O_DT - 8:O_DT - 8 + LANES]
    lane = lax.broadcasted_iota(jnp.int32, gates_if.shape, 1)
    small = jnp.where(lane < 8, gates_if,
                      jnp.where(lane < 16, gates_dt,
                                jnp.where(lane < 24, pltpu.roll(gates_dt, 8, 1), 0.0)))
    o_ref[:, C_SMALL:PROJ_W] = small.astype(BF16)


def _reorder_w_in(w_in):
    nl, d, n = w_in.shape
    assert n == O_END and O_DT - 8 == 32 * LANES and O_I == 20 * LANES
    return pl.pallas_call(
        _reorder_kernel,
        grid=(nl, d // W_IN_ROWS),
        in_specs=[pl.BlockSpec((None, W_IN_ROWS, n), lambda l, i: (l, i, 0))],
        out_specs=pl.BlockSpec((None, W_IN_ROWS, PROJ_W), lambda l, i: (l, i, 0)),
        out_shape=jax.ShapeDtypeStruct((nl, d, PROJ_W), BF16),
        compiler_params=pltpu.CompilerParams(
            dimension_semantics=("arbitrary", "arbitrary"), vmem_limit_bytes=VMEM_LIMIT),
        name="reorder_w_in",
    )(w_in)


def _block_diag(w):
    nl, nblk, d, e = w.shape
    eye = jnp.eye(nblk, dtype=w.dtype)
    return jnp.einsum('lnde,nm->lndme', w, eye).reshape(nl, nblk * d, nblk * e)


def _lane_row(parts, width=LANES):
    nl = parts[0][1].shape[0]
    row = jnp.zeros((nl, width), F32)
    for off, p in parts:
        row = lax.dynamic_update_slice(row, p.astype(F32), (0, off))
    return row[:, None, :]


def kernel(x_prompt, x_sample, state_mlstm_C, state_mlstm_n, state_mlstm_m, state_ssd, state_ssd_conv, state_sconv_conv, state_rglru_h, state_rglru_conv, norm_w, w_in, ml_i_bias, ml_f_bias, ml_norm_w, ssd_conv_w, ssd_conv_b, ssd_dt_bias, ssd_A_log, ssd_D, ssd_norm_w, sc_conv_w, rg_conv_w, rg_conv_b, rg_wa, rg_ba, rg_wx, rg_bx, rg_lambda, w_out, final_norm_w):
    bp, tp, _ = x_prompt.shape
    bs, ts, _ = x_sample.shape

    w_in_r = _reorder_w_in(w_in)
    w_out_b = w_out.astype(BF16)
    rg_w = jnp.concatenate([_block_diag(rg_wa), _block_diag(rg_wx)], axis=-1).astype(BF16)
    bias_row = _lane_row([(0, ml_i_bias), (4, ml_f_bias), (8, ssd_dt_bias), (16, ssd_dt_bias)])
    alog_row = _lane_row([(8, ssd_A_log)])
    d_row = jnp.repeat(ssd_D, SSD_P, axis=-1)[:, None, :]
    row = lambda p: p[:, None, :]
    params = [bias_row, alog_row, row(ml_norm_w), ssd_conv_w, row(ssd_conv_b), d_row, row(ssd_norm_w),
              sc_conv_w, rg_conv_w, row(rg_conv_b), rg_w, row(rg_ba), row(rg_bx), row(rg_lambda)]
    norm_rows = row(norm_w)

    def pad_m(m):
        return jnp.pad(m, ((0, 0), (0, 0), (4, LANES - 4 - ML_HEADS)))[:, :, None, :]

    sample_states = [state_mlstm_C, state_mlstm_n, pad_m(state_mlstm_m), state_ssd, state_ssd_conv,
                     state_sconv_conv, state_rglru_h[:, :, None, :], state_rglru_conv]
    prompt_states = [
        jnp.zeros((1, bp, ML_HEADS, ML_D, ML_D), F32),
        jnp.zeros((1, bp, ML_HEADS, ML_D), F32),
        pad_m(jnp.full((1, bp, ML_HEADS), NEG_BIG, F32)),
        jnp.zeros((1, bp, SSD_HEADS, SSD_P, SSD_N), F32),
        jnp.zeros((1, bp, SSD_CONV - 1, SSD_CONV_DIM), F32),
        jnp.zeros((1, bp, SC_WIDTH - 1, BRANCH), F32),
        jnp.zeros((1, bp, 1, BRANCH), F32),
        jnp.zeros((1, bp, RG_CONV - 1, BRANCH), F32),
    ]

    xp = x_prompt
    xs = x_sample.reshape(bs * ts, D_MODEL)
    fnw = final_norm_w[None, :]
    st_p, st_s = None, None
    for l in range(DEPTH):
        final = l == DEPTH - 1
        xp, st_p = _layer_prompt(xp, norm_rows, w_in_r, w_out_b, fnw, prompt_states, params, l, st_p, final)
        proj_s = _inproj(xs, norm_rows, w_in_r, l, 512).reshape(bs, ts, PROJ_W)
        mix_s, st_s = _mixer(proj_s, sample_states, l, params, l, st_s, ts, SAMPLE_SEQS, F32)
        xs = _outproj(mix_s.reshape(bs * ts, D_MODEL), xs, w_out_b, fnw, l, final, 512)

    def unpack(states):
        states = list(states)
        states[2] = states[2][:, :, 0, 4:4 + ML_HEADS]
        states[6] = states[6][:, :, 0, :]
        return tuple(states)

    return (xp, xs.reshape(bs, ts, D_MODEL)) + unpack(st_p) + unpack(st_s)
```

```python
import functools

import jax
import jax.numpy as jnp
from jax import lax
from jax.experimental import pallas as pl
from jax.experimental.pallas import tpu as pltpu

F32 = jnp.float32
BF16 = jnp.bfloat16

D_MODEL = 2048
DEPTH = 2
BRANCH = 512
CHUNK = 128
EPS = 1e-6
ML_HEADS = 4
ML_D = 128
NEG_BIG = -1e30
SSD_HEADS = 8
SSD_P = 64
SSD_GROUPS = 2
SSD_N = 128
SSD_CONV = 4
SSD_CONV_DIM = 1024
SC_WIDTH = 3
RG_CONV = 4
RG_C = 8.0
RG_BLOCKS = 8

C_Q, C_K, C_V, C_O, C_Z = 0, 512, 1024, 1536, 2048
C_SSD_Z, C_XBC = 2560, 3072
C_SC_B, C_SC_C, C_SC_H, C_SC_Z = 4096, 4608, 5120, 5632
C_RG_X, C_RG_Z = 6144, 6656
C_SMALL = 7168
PROJ_W = 7296
PROJ_TILE = 256
LANES = 128
CONV_BASE = 8
SAMPLE_SEQS = 8

V7X_VMEM_BYTES = 64 * 1024 * 1024
VMEM_LIMIT = V7X_VMEM_BYTES - 4 * 1024 * 1024

NT_DIMS = (((1,), (1,)), ((), ()))
TN_DIMS = (((0,), (0,)), ((), ()))

N_STATES = 8
N_PARAMS = 14
STATE_BLOCKS = [
    (1, 1, ML_HEADS, ML_D, ML_D), (1, 1, ML_HEADS, ML_D), (1, 1, 1, LANES),
    (1, 1, SSD_HEADS, SSD_P, SSD_N), (1, 1, SSD_CONV - 1, SSD_CONV_DIM),
    (1, 1, SC_WIDTH - 1, BRANCH), (1, 1, 1, BRANCH), (1, 1, RG_CONV - 1, BRANCH),
]


def _softplus(x):
    return jnp.maximum(x, 0.0) + jnp.log1p(jnp.exp(-jnp.abs(x)))


def _silu(x):
    return x * jax.nn.sigmoid(x)


def _dot(a, b):
    return jnp.dot(a, b, preferred_element_type=F32)


def _dot_nt(a, b):
    return lax.dot_general(a, b, NT_DIMS, preferred_element_type=F32)


def _dot_tn(a, b):
    return lax.dot_general(a, b, TN_DIMS, preferred_element_type=F32)


def _rms_norm(x, w):
    return x * lax.rsqrt(jnp.mean(x * x, axis=-1, keepdims=True) + EPS) * w


def _inproj_kernel(x_ref, nw_ref, w_ref, o_ref, xn_ref):
    @pl.when(pl.program_id(1) == 0)
    def _():
        xn_ref[...] = _rms_norm(x_ref[...], nw_ref[...]).astype(BF16)

    o_ref[...] = _dot(xn_ref[...], w_ref[...])


def _inproj(x, norm_w, w, layer, tm):
    m = x.shape[0]
    tn = PROJ_W // 3
    return pl.pallas_call(
        _inproj_kernel,
        grid=(m // tm, PROJ_W // tn),
        in_specs=[
            pl.BlockSpec((tm, D_MODEL), lambda i, j: (i, 0)),
            pl.BlockSpec((None, 1, D_MODEL), lambda i, j: (layer, 0, 0)),
            pl.BlockSpec((None, D_MODEL, tn), lambda i, j: (layer, 0, j)),
        ],
        out_specs=pl.BlockSpec((tm, tn), lambda i, j: (i, j)),
        out_shape=jax.ShapeDtypeStruct((m, PROJ_W), F32),
        scratch_shapes=[pltpu.VMEM((tm, D_MODEL), BF16)],
        compiler_params=pltpu.CompilerParams(
            dimension_semantics=("arbitrary", "arbitrary"), vmem_limit_bytes=VMEM_LIMIT),
        name="inproj",
    )(x, norm_w, w)


def _outproj_kernel(final, mix_ref, x_ref, w_ref, fnw_ref, o_ref):
    y = x_ref[...] + _dot(mix_ref[...].astype(BF16), w_ref[...])
    if final:
        y = _rms_norm(y, fnw_ref[...])
    o_ref[...] = y


def _outproj(mix, x, w, fnw, layer, final, tm):
    m = x.shape[0]
    return pl.pallas_call(
        functools.partial(_outproj_kernel, final),
        grid=(m // tm,),
        in_specs=[
            pl.BlockSpec((tm, D_MODEL), lambda i: (i, 0)),
            pl.BlockSpec((tm, D_MODEL), lambda i: (i, 0)),
            pl.BlockSpec((None, D_MODEL, D_MODEL), lambda i: (layer, 0, 0)),
            pl.BlockSpec((1, D_MODEL), lambda i: (0, 0)),
        ],
        out_specs=pl.BlockSpec((tm, D_MODEL), lambda i: (i, 0)),
        out_shape=jax.ShapeDtypeStruct((m, D_MODEL), F32),
        compiler_params=pltpu.CompilerParams(
            dimension_semantics=("arbitrary",), vmem_limit_bytes=VMEM_LIMIT),
        name="outproj",
    )(mix, x, w, fnw)


def _causal_conv(ext_ref, u, w_ref, width, rows):
    lo = CONV_BASE - (width - 1)
    ext_ref[CONV_BASE:CONV_BASE + rows, :] = u
    acc = ext_ref[lo:lo + rows, :] * w_ref[0:1, :]
    for j in range(1, width):
        acc = acc + ext_ref[lo + j:lo + j + rows, :] * w_ref[j:j + 1, :]
    tail = ext_ref[lo + rows:CONV_BASE + rows, :]
    ext_ref[lo:CONV_BASE, :] = tail
    return acc, tail


def _linear_scan(a, b, h0, rows):
    if rows < 8:
        h = h0
        out = []
        for t in range(rows):
            h = a[t:t + 1, :] * h + b[t:t + 1, :]
            out.append(h)
        return jnp.concatenate(out, axis=0)
    row = lax.broadcasted_iota(jnp.int32, a.shape, 0)
    d = 1
    while d < rows:
        keep = row >= d
        a_prev = jnp.where(keep, pltpu.roll(a, d, 0), 1.0)
        b_prev = jnp.where(keep, pltpu.roll(b, d, 0), 0.0)
        b = a * b_prev + b
        a = a * a_prev
        d *= 2
    return b + a * h0


def _cumsum_rows(z, rows):
    if rows < 8:
        out = [z[0:1, :]]
        for t in range(1, rows):
            out.append(out[-1] + z[t:t + 1, :])
        return jnp.concatenate(out, axis=0)
    row = lax.broadcasted_iota(jnp.int32, z.shape, 0)
    d = 1
    while d < rows:
        z = z + jnp.where(row >= d, pltpu.roll(z, d, 0), 0.0)
        d *= 2
    return z


def _load_state(state_in, state_out, ssd_ext, sc_ext, rg_ext):
    c_in, n_in, m_in, s_in, ssdcv_in, sccv_in, rgh_in, rgcv_in = state_in
    c_out, n_out, m_out, s_out, ssdcv_out, sccv_out, rgh_out, rgcv_out = state_out
    c_out[0:1] = c_in[...]
    n_out[0:1] = n_in[...]
    m_out[0:1] = m_in[...]
    s_out[0:1] = s_in[...]
    rgh_out[0:1] = rgh_in[...]
    for g in range(ssd_ext.shape[0]):
        ssd_ext[g, CONV_BASE - (SSD_CONV - 1):CONV_BASE, :] = ssdcv_in[0, g]
        sc_ext[g, CONV_BASE - (SC_WIDTH - 1):CONV_BASE, :] = sccv_in[0, g]
        rg_ext[g, CONV_BASE - (RG_CONV - 1):CONV_BASE, :] = rgcv_in[0, g]
    for out in state_out:
        if out.shape[0] > 1:
            out[1:] = jnp.zeros((out.shape[0] - 1,) + out.shape[1:], out.dtype)


def _gate_prep(L, pcols, bias_ref, alog_ref):
    small = pcols(C_SMALL, LANES) + bias_ref[...]
    lane = lax.broadcasted_iota(jnp.int32, (L, LANES), 1)
    sp = _softplus(small)
    logsig = -_softplus(-small)
    a_coef = -jnp.exp(alog_ref[...])
    z = jnp.where((lane >= 4) & (lane < 8), logsig,
                  jnp.where((lane >= 8) & (lane < 16), sp * a_coef, 0.0))
    rr = lax.broadcasted_iota(jnp.int32, (L, L), 0)
    cc = lax.broadcasted_iota(jnp.int32, (L, L), 1)
    causal = cc <= rr
    cum = _cumsum_rows(z, L)
    packed = jnp.where(lane < 4, small,
                       jnp.where(lane < 16, cum, jnp.where(lane < 24, sp, 0.0)))
    packed_t = packed.T
    return small, sp, cum, packed_t, causal


def _group_mlstm(L, pcols, mix_store, gates, c_out, n_out, m_out, mlnw_ref):
    small, _, cum, packed_t, causal = gates
    m_row = m_out[...]
    lane1 = lax.broadcasted_iota(jnp.int32, (1, LANES), 1)
    m_row_new = m_row
    for h in range(ML_HEADS):
        q = pcols(C_Q + h * ML_D, ML_D)
        k = pcols(C_K + h * ML_D, ML_D) * (ML_D ** -0.5)
        v = pcols(C_V + h * ML_D, ML_D)
        qb, kb, vb = q.astype(BF16), k.astype(BF16), v.astype(BF16)
        b_col = cum[:, 4 + h:5 + h]
        i_col = small[:, h:h + 1]
        b_row = packed_t[4 + h:5 + h, :]
        i_row = packed_t[h:h + 1, :]
        m_h = m_row[:, 4 + h:5 + h]
        dmat = jnp.where(causal, b_col - b_row + i_row, -jnp.inf)
        inter = b_col + m_h
        m_t = jnp.maximum(inter, jnp.max(dmat, axis=1, keepdims=True))
        w_intra = jnp.exp(dmat - m_t)
        w_inter = jnp.exp(inter - m_t)
        s = _dot_nt(qb, kb) * w_intra
        c_h = c_out[h]
        n_h = n_out[h:h + 1, :]
        num = _dot(s.astype(BF16), vb) + w_inter * _dot(qb, c_h.astype(BF16))
        den = jnp.sum(s, axis=1, keepdims=True) + w_inter * jnp.sum(q * n_h, axis=1, keepdims=True)
        hh = num / jnp.maximum(jnp.abs(den), jnp.exp(-m_t))
        b_last = cum[L - 1:L, 4 + h:5 + h]
        g = b_last - b_col + i_col
        inter_end = b_last + m_h
        m_new = jnp.maximum(inter_end, jnp.max(g, axis=0, keepdims=True))
        wg = jnp.exp(g - m_new)
        we = jnp.exp(inter_end - m_new)
        kw = k * wg
        c_out[h] = we * c_h + _dot_tn(kw.astype(BF16), vb)
        n_out[h:h + 1, :] = we * n_h + jnp.sum(kw, axis=0, keepdims=True)
        m_row_new = jnp.where(lane1 == 4 + h, m_new, m_row_new)
        mu = jnp.mean(hh, axis=-1, keepdims=True)
        hc = hh - mu
        var = jnp.mean(hc * hc, axis=-1, keepdims=True)
        hn = hc * lax.rsqrt(var + EPS) * mlnw_ref[:, h * ML_D:(h + 1) * ML_D]
        o = pcols(C_O + h * ML_D, ML_D)
        zg = pcols(C_Z + h * ML_D, ML_D)
        mix_store(h * ML_D, jax.nn.sigmoid(o) * hn * _silu(zg))
        if h == ML_HEADS - 1:
            m_out[...] = m_row_new
        yield


def _group_ssd(L, pcols, mix_store, gates, s_out, ssdcv_out, ssd_ext, yssd_ref,
               ssdcw_ref, ssdcb_ref, drow_ref, ssdnw_ref):
    _, sp, cum, packed_t, causal = gates
    conv, tail = _causal_conv(ssd_ext, pcols(C_XBC, SSD_CONV_DIM), ssdcw_ref, SSD_CONV, L)
    ssdcv_out[...] = tail
    xbc = _silu(conv + ssdcb_ref[...])
    xs_all = xbc[:, 0:BRANCH]
    cb, bm_b, cm_b = [], [], []
    for g in range(SSD_GROUPS):
        bm_g = xbc[:, BRANCH + g * SSD_N:BRANCH + (g + 1) * SSD_N].astype(BF16)
        cm_g = xbc[:, BRANCH + (SSD_GROUPS + g) * SSD_N:BRANCH + (SSD_GROUPS + g + 1) * SSD_N].astype(BF16)
        bm_b.append(bm_g)
        cm_b.append(cm_g)
        cb.append(_dot_nt(cm_g, bm_g))
    yield
    rep = SSD_HEADS // SSD_GROUPS
    for h in range(SSD_HEADS):
        g = h // rep
        acs_col = cum[:, 8 + h:9 + h]
        acs_row = packed_t[8 + h:9 + h, :]
        dt_row = packed_t[16 + h:17 + h, :]
        dt_col = sp[:, 8 + h:9 + h]
        decay = jnp.exp(jnp.where(causal, acs_col - acs_row, -jnp.inf))
        scores = cb[g] * decay * dt_row
        xs_h = xs_all[:, h * SSD_P:(h + 1) * SSD_P]
        s_h = s_out[h]
        y = (_dot(scores.astype(BF16), xs_h.astype(BF16))
             + jnp.exp(acs_col) * _dot_nt(cm_b[g], s_h.astype(BF16)))
        a_last = cum[L - 1:L, 8 + h:9 + h]
        w = jnp.exp(a_last - acs_col) * dt_col
        s_out[h] = jnp.exp(a_last) * s_h + _dot_tn((xs_h * w).astype(BF16), bm_b[g])
        yssd_ref[:, h * SSD_P:(h + 1) * SSD_P] = y + drow_ref[:, h * SSD_P:(h + 1) * SSD_P] * xs_h
        yield
    yz = yssd_ref[...] * _silu(pcols(C_SSD_Z, BRANCH))
    gw = BRANCH // SSD_GROUPS
    for g in range(SSD_GROUPS):
        seg = yz[:, g * gw:(g + 1) * gw]
        seg = seg * lax.rsqrt(jnp.mean(seg * seg, axis=-1, keepdims=True) + EPS)
        mix_store(BRANCH + g * gw, seg * ssdnw_ref[:, g * gw:(g + 1) * gw])
    yield


def _group_sconv(L, pcols, mix_store, sccv_out, sc_ext, sccw_ref):
    u = pcols(C_SC_C, BRANCH) * pcols(C_SC_H, BRANCH)
    cu, tail = _causal_conv(sc_ext, u, sccw_ref, SC_WIDTH, L)
    sccv_out[...] = tail
    mix_store(2 * BRANCH, pcols(C_SC_B, BRANCH) * cu * _silu(pcols(C_SC_Z, BRANCH)))
    yield


def _group_rglru(L, pcols, mix_store, rgh_out, rgcv_out, rg_ext,
                 rgcw_ref, rgcb_ref, rgw_ref, rgba_ref, rgbx_ref, rglam_ref):
    conv, tail = _causal_conv(rg_ext, pcols(C_RG_X, BRANCH), rgcw_ref, RG_CONV, L)
    rgcv_out[...] = tail
    xr = conv + rgcb_ref[...]
    pre = _dot(xr.astype(BF16), rgw_ref[...])
    r = jax.nn.sigmoid(pre[:, 0:BRANCH] + rgba_ref[...])
    ig = jax.nn.sigmoid(pre[:, BRANCH:2 * BRANCH] + rgbx_ref[...])
    log_a = (-RG_C * r) * _softplus(-rglam_ref[...])
    a = jnp.exp(log_a)
    bterm = jnp.sqrt(-jnp.tanh(log_a) * (a * a + 1.0)) * (ig * xr)
    yield
    h_rg = _linear_scan(a, bterm, rgh_out[...], L)
    rgh_out[...] = h_rg[L - 1:L, :]
    mix_store(3 * BRANCH, h_rg * _silu(pcols(C_RG_Z, BRANCH)))
    yield


def _split_refs(refs, n_lead, n_carried, n_out_lead):
    lead = refs[:n_lead]
    state_in = refs[n_lead:n_lead + N_STATES]
    params = refs[n_lead + N_STATES:n_lead + N_STATES + N_PARAMS]
    rest = refs[n_lead + N_STATES + N_PARAMS + n_carried:]
    out_lead = rest[:n_out_lead]
    state_out = rest[n_out_lead:n_out_lead + N_STATES]
    scratch = rest[n_out_lead + N_STATES:]
    return lead, state_in, params, out_lead, state_out, scratch


def _seq_groups(L, seq, pcols, mix_store, state_out, scratch, params):
    (bias_ref, alog_ref, mlnw_ref, ssdcw_ref, ssdcb_ref, drow_ref, ssdnw_ref, sccw_ref, rgcw_ref,
     rgcb_ref, rgw_ref, rgba_ref, rgbx_ref, rglam_ref) = params
    c_out, n_out, m_out, s_out, ssdcv_out, sccv_out, rgh_out, rgcv_out = [r.at[0, seq] for r in state_out]
    ssd_ext, sc_ext, rg_ext, yssd_ref = [r.at[seq] for r in scratch]
    gates = _gate_prep(L, pcols, bias_ref, alog_ref)
    return [
        _group_mlstm(L, pcols, mix_store, gates, c_out, n_out, m_out, mlnw_ref),
        _group_ssd(L, pcols, mix_store, gates, s_out, ssdcv_out, ssd_ext, yssd_ref,
                   ssdcw_ref, ssdcb_ref, drow_ref, ssdnw_ref),
        _group_sconv(L, pcols, mix_store, sccv_out, sc_ext, sccw_ref),
        _group_rglru(L, pcols, mix_store, rgh_out, rgcv_out, rg_ext,
                     rgcw_ref, rgcb_ref, rgw_ref, rgba_ref, rgbx_ref, rglam_ref),
    ]


def _mixer_kernel(rows, nseq, n_carried, *refs):
    (proj_ref,), state_in, params, (mix_ref,), state_out, scratch = _split_refs(refs, 1, n_carried, 1)

    @pl.when(pl.program_id(1) == 0)
    def _():
        _load_state(state_in, state_out, *scratch[:3])

    def chain(seq):
        def pcols(start, width):
            return proj_ref[seq, :, start:start + width]

        def mix_store(start, val):
            mix_ref[seq, :, start:start + val.shape[1]] = val.astype(mix_ref.dtype)

        for group in _seq_groups(rows, seq, pcols, mix_store, state_out, scratch, params):
            yield from group

    chains = [chain(seq) for seq in range(nseq)]
    while chains:
        chains = [c for c in chains if next(c, StopIteration) is not StopIteration]


def _state_specs(state_layer, layer, carried, nseq):
    def at_layer(lyr, blk):
        blk = (blk[0], nseq) + blk[2:]
        tail = (0,) * (len(blk) - 2)
        return pl.BlockSpec(blk, lambda b, c: (lyr, b) + tail)

    in_specs = [at_layer(state_layer, blk) for blk in STATE_BLOCKS]
    if carried:
        out_specs = [at_layer(layer, blk) for blk in STATE_BLOCKS]
    else:
        assert layer == 0
        out_specs = [at_layer(0, (DEPTH,) + blk[1:]) for blk in STATE_BLOCKS]
    return in_specs, out_specs


def _state_shapes(nb):
    return [jax.ShapeDtypeStruct((DEPTH, nb) + blk[2:], F32) for blk in STATE_BLOCKS]


def _mix_scratch(rows, nseq):
    return [
        pltpu.VMEM((nseq, CONV_BASE + rows, SSD_CONV_DIM), F32),
        pltpu.VMEM((nseq, CONV_BASE + rows, BRANCH), F32),
        pltpu.VMEM((nseq, CONV_BASE + rows, BRANCH), F32),
        pltpu.VMEM((nseq, rows, BRANCH), F32),
    ]


def _mixer(proj, states, state_layer, params, layer, carried, rows, nseq, mix_dtype):
    nb, t, _ = proj.shape
    carried = list(carried) if carried is not None else []
    st_in, st_out = _state_specs(state_layer, layer, carried, nseq)
    in_specs = [pl.BlockSpec((nseq, rows, PROJ_W), lambda b, c: (b, c, 0))] + st_in
    in_specs += [pl.BlockSpec((None,) + p.shape[1:], lambda b, c: (layer, 0, 0)) for p in params]
    in_specs += [pl.BlockSpec(memory_space=pl.ANY) for _ in carried]
    first_carried = 1 + N_STATES + N_PARAMS
    outs = pl.pallas_call(
        functools.partial(_mixer_kernel, rows, nseq, len(carried)),
        grid=(nb // nseq, t // rows),
        in_specs=in_specs,
        out_specs=[pl.BlockSpec((nseq, rows, D_MODEL), lambda b, c: (b, c, 0))] + st_out,
        out_shape=[jax.ShapeDtypeStruct((nb, t, D_MODEL), mix_dtype)] + _state_shapes(nb),
        input_output_aliases={first_carried + i: 1 + i for i in range(len(carried))},
        scratch_shapes=_mix_scratch(rows, nseq),
        compiler_params=pltpu.CompilerParams(
            dimension_semantics=("arbitrary", "arbitrary"), vmem_limit_bytes=VMEM_LIMIT),
        name=f"mixer_rows{rows}",
    )(proj, *states, *params, *carried)
    return outs[0], list(outs[1:])


def _layer_kernel(final, n_carried, *refs):
    lead, state_in, params, (y_ref,), state_out, scratch = _split_refs(refs, 6, n_carried, 1)
    x_ref, xnext_ref, nw_ref, win_ref, wout_ref, fnw_ref = lead
    proj_ref, mix_ref = scratch[:2]
    mix_scratch = scratch[2:]
    step = pl.program_id(0) * pl.num_programs(1) + pl.program_id(1)

    @pl.when(pl.program_id(1) == 0)
    def _():
        _load_state(state_in, state_out, *mix_scratch[:3])

    @pl.when(step == 0)
    def _():
        xn0 = _rms_norm(x_ref[0], nw_ref[...]).astype(BF16)
        for start in range(0, PROJ_W, PROJ_TILE):
            stop = min(start + PROJ_TILE, PROJ_W)
            proj_ref[0, :, start:stop] = _dot(xn0, win_ref[:, start:stop])

    tiles = [(s, min(s + PROJ_TILE, PROJ_W)) for s in range(0, PROJ_W, PROJ_TILE)]
    n_pieces = 18

    def chunk_step(cur, nxt):
        x = x_ref[0]
        xn_next = _rms_norm(xnext_ref[0], nw_ref[...]).astype(BF16)

        def project_some(piece):
            lo = (piece * len(tiles)) // n_pieces
            hi = ((piece + 1) * len(tiles)) // n_pieces
            for start, stop in tiles[lo:hi]:
                proj_ref[nxt, :, start:stop] = _dot(xn_next, win_ref[:, start:stop])

        def pcols(start, width):
            return proj_ref[cur, :, start:start + width]

        def mix_store(start, val):
            mix_ref[:, start:start + val.shape[1]] = val.astype(BF16)

        def out_project(group):
            lo, hi = group * BRANCH, (group + 1) * BRANCH
            return _dot(mix_ref[:, lo:hi], wout_ref[lo:hi, :])

        project_some(0)
        groups = _seq_groups(CHUNK, 0, pcols, mix_store, state_out, mix_scratch, params)
        piece = 1
        y = x
        for gi, group in enumerate(groups):
            for _ in group:
                project_some(piece)
                piece += 1
            y = y + out_project(gi)
        assert piece == n_pieces
        if final:
            y = _rms_norm(y, fnw_ref[...])
        y_ref[0] = y

    for parity in range(2):
        pl.when(step % 2 == parity)(functools.partial(chunk_step, parity, 1 - parity))


def _layer_prompt(x, norm_w, w_in, w_out, fnw, states, params, layer, carried, final):
    nb, t, _ = x.shape
    nc = t // CHUNK
    carried = list(carried) if carried is not None else []
    st_in, st_out = _state_specs(0, layer, carried, 1)
    resident = dict(pipeline_mode=pl.Buffered(1))

    def next_chunk(b, c):
        step = jnp.minimum(b * nc + c + 1, nb * nc - 1)
        return (step // nc, step % nc, 0)

    in_specs = [
        pl.BlockSpec((1, CHUNK, D_MODEL), lambda b, c: (b, c, 0)),
        pl.BlockSpec((1, CHUNK, D_MODEL), next_chunk),
        pl.BlockSpec((None, 1, D_MODEL), lambda b, c: (layer, 0, 0)),
        pl.BlockSpec((None, D_MODEL, PROJ_W), lambda b, c: (layer, 0, 0), **resident),
        pl.BlockSpec((None, D_MODEL, D_MODEL), lambda b, c: (layer, 0, 0), **resident),
        pl.BlockSpec((1, D_MODEL), lambda b, c: (0, 0)),
    ] + st_in
    in_specs += [pl.BlockSpec((None,) + p.shape[1:], lambda b, c: (layer, 0, 0)) for p in params]
    in_specs += [pl.BlockSpec(memory_space=pl.ANY) for _ in carried]
    first_carried = 6 + N_STATES + N_PARAMS
    outs = pl.pallas_call(
        functools.partial(_layer_kernel, final, len(carried)),
        grid=(nb, nc),
        in_specs=in_specs,
        out_specs=[pl.BlockSpec((1, CHUNK, D_MODEL), lambda b, c: (b, c, 0))] + st_out,
        out_shape=[jax.ShapeDtypeStruct((nb, t, D_MODEL), F32)] + _state_shapes(nb),
        input_output_aliases={first_carried + i: 1 + i for i in range(len(carried))},
        scratch_shapes=[pltpu.VMEM((2, CHUNK, PROJ_W), F32), pltpu.VMEM((CHUNK, D_MODEL), BF16)]
        + _mix_scratch(CHUNK, 1),
        compiler_params=pltpu.CompilerParams(
            dimension_semantics=("arbitrary", "arbitrary"), vmem_limit_bytes=VMEM_LIMIT),
        name="layer_prompt",
    )(x, x, norm_w, w_in, w_out, fnw, *states, *params, *carried)
    return outs[0], list(outs[1:])


O_I, O_SSD_Z, O_XBC, O_DT, O_SC, O_RG, O_END = 2560, 2568, 3080, 4104, 4112, 6160, 7184
REORDER_TILE = 512
GATE_TILE = C_SMALL // REORDER_TILE


def _reorder_kernel(src_ref, if_ref, dt_ref, o_ref):
    j = pl.program_id(1)

    @pl.when(j < GATE_TILE)
    def _():
        o_ref[...] = src_ref[0].T.astype(BF16)

    @pl.when(j == GATE_TILE)
    def _():
        zeros = jnp.zeros((REORDER_TILE - 24, D_MODEL), F32)
        small = jnp.concatenate([if_ref[0], dt_ref[0], dt_ref[0], zeros], axis=0)
        o_ref[...] = small.T.astype(BF16)


def _reorder_w_in(w_in):
    nl, d, n = w_in.shape
    assert n == O_END and d == D_MODEL
    assert C_SSD_Z % REORDER_TILE == 0 and C_SC_B % REORDER_TILE == 0 and C_SMALL % REORDER_TILE == 0
    w_t = jnp.swapaxes(w_in, 1, 2)

    def src_row(l, j):
        t_ssd, t_sc = C_SSD_Z // REORDER_TILE, C_SC_B // REORDER_TILE
        row = jnp.where(j < t_ssd, j * REORDER_TILE,
                        jnp.where(j < t_sc, O_SSD_Z + (j - t_ssd) * REORDER_TILE,
                                  O_SC + (j - t_sc) * REORDER_TILE))
        row = jnp.where(j < GATE_TILE, row, 0)
        return (l, pl.multiple_of(row, 8), 0)

    return pl.pallas_call(
        _reorder_kernel,
        grid=(nl, GATE_TILE + 1),
        in_specs=[
            pl.BlockSpec((pl.Element(1), pl.Element(REORDER_TILE), pl.Element(d)), src_row),
            pl.BlockSpec((pl.Element(1), pl.Element(8), pl.Element(d)), lambda l, j: (l, O_I, 0)),
            pl.BlockSpec((pl.Element(1), pl.Element(8), pl.Element(d)), lambda l, j: (l, O_DT, 0)),
        ],
        out_specs=pl.BlockSpec((None, d, REORDER_TILE), lambda l, j: (l, 0, j)),
        out_shape=jax.ShapeDtypeStruct((nl, d, PROJ_W), BF16),
        compiler_params=pltpu.CompilerParams(
            dimension_semantics=("arbitrary", "arbitrary"), vmem_limit_bytes=VMEM_LIMIT),
        name="reorder_w_in",
    )(w_t, w_t, w_t)


def _block_diag(w):
    nl, nblk, d, e = w.shape
    eye = jnp.eye(nblk, dtype=w.dtype)
    return jnp.einsum('lnde,nm->lndme', w, eye).reshape(nl, nblk * d, nblk * e)


def _lane_row(parts, width=LANES):
    nl = parts[0][1].shape[0]
    row = jnp.zeros((nl, width), F32)
    for off, p in parts:
        row = lax.dynamic_update_slice(row, p.astype(F32), (0, off))
    return row[:, None, :]


def kernel(x_prompt, x_sample, state_mlstm_C, state_mlstm_n, state_mlstm_m, state_ssd, state_ssd_conv, state_sconv_conv, state_rglru_h, state_rglru_conv, norm_w, w_in, ml_i_bias, ml_f_bias, ml_norm_w, ssd_conv_w, ssd_conv_b, ssd_dt_bias, ssd_A_log, ssd_D, ssd_norm_w, sc_conv_w, rg_conv_w, rg_conv_b, rg_wa, rg_ba, rg_wx, rg_bx, rg_lambda, w_out, final_norm_w):
    bp, tp, _ = x_prompt.shape
    bs, ts, _ = x_sample.shape

    w_in_r = _reorder_w_in(w_in)
    w_out_b = w_out.astype(BF16)
    rg_w = jnp.concatenate([_block_diag(rg_wa), _block_diag(rg_wx)], axis=-1).astype(BF16)
    bias_row = _lane_row([(0, ml_i_bias), (4, ml_f_bias), (8, ssd_dt_bias), (16, ssd_dt_bias)])
    alog_row = _lane_row([(8, ssd_A_log)])
    d_row = jnp.repeat(ssd_D, SSD_P, axis=-1)[:, None, :]
    row = lambda p: p[:, None, :]
    params = [bias_row, alog_row, row(ml_norm_w), ssd_conv_w, row(ssd_conv_b), d_row, row(ssd_norm_w),
              sc_conv_w, rg_conv_w, row(rg_conv_b), rg_w, row(rg_ba), row(rg_bx), row(rg_lambda)]
    norm_rows = row(norm_w)

    def pad_m(m):
        return jnp.pad(m, ((0, 0), (0, 0), (4, LANES - 4 - ML_HEADS)))[:, :, None, :]

    sample_states = [state_mlstm_C, state_mlstm_n, pad_m(state_mlstm_m), state_ssd, state_ssd_conv,
                     state_sconv_conv, state_rglru_h[:, :, None, :], state_rglru_conv]
    prompt_states = [
        jnp.zeros((1, bp, ML_HEADS, ML_D, ML_D), F32),
        jnp.zeros((1, bp, ML_HEADS, ML_D), F32),
        pad_m(jnp.full((1, bp, ML_HEADS), NEG_BIG, F32)),
        jnp.zeros((1, bp, SSD_HEADS, SSD_P, SSD_N), F32),
        jnp.zeros((1, bp, SSD_CONV - 1, SSD_CONV_DIM), F32),
        jnp.zeros((1, bp, SC_WIDTH - 1, BRANCH), F32),
        jnp.zeros((1, bp, 1, BRANCH), F32),
        jnp.zeros((1, bp, RG_CONV - 1, BRANCH), F32),
    ]

    xp = x_prompt
    xs = x_sample.reshape(bs * ts, D_MODEL)
    fnw = final_norm_w[None, :]
    st_p, st_s = None, None
    for l in range(DEPTH):
        final = l == DEPTH - 1
        xp, st_p = _layer_prompt(xp, norm_rows, w_in_r, w_out_b, fnw, prompt_states, params, l, st_p, final)
        proj_s = _inproj(xs, norm_rows, w_in_r, l, 512).reshape(bs, ts, PROJ_W)
        mix_s, st_s = _mixer(proj_s, sample_states, l, params, l, st_s, ts, SAMPLE_SEQS, F32)
        xs = _outproj(mix_s.reshape(bs * ts, D_MODEL), xs, w_out_b, fnw, l, final, 512)

    def unpack(states):
        states = list(states)
        states[2] = states[2][:, :, 0, 4:4 + ML_HEADS]
        states[6] = states[6][:, :, 0, :]
        return tuple(states)

    return (xp, xs.reshape(bs, ts, D_MODEL)) + unpack(st_p) + unpack(st_s)
```

```python
import functools

import jax
import jax.numpy as jnp
from jax import lax
from jax.experimental import pallas as pl
from jax.experimental.pallas import tpu as pltpu

F32 = jnp.float32
BF16 = jnp.bfloat16

D_MODEL = 2048
DEPTH = 2
BRANCH = 512
CHUNK = 128
EPS = 1e-6
ML_HEADS = 4
ML_D = 128
NEG_BIG = -1e30
SSD_HEADS = 8
SSD_P = 64
SSD_GROUPS = 2
SSD_N = 128
SSD_CONV = 4
SSD_CONV_DIM = 1024
SC_WIDTH = 3
RG_CONV = 4
RG_C = 8.0
RG_BLOCKS = 8

C_Q, C_K, C_V, C_O, C_Z = 0, 512, 1024, 1536, 2048
C_SSD_Z, C_XBC = 2560, 3072
C_SC_B, C_SC_C, C_SC_H, C_SC_Z = 4096, 4608, 5120, 5632
C_RG_X, C_RG_Z = 6144, 6656
C_SMALL = 7168
PROJ_W = 7296
PROJ_TILE = 256
MAJOR = "major"
LANES = 128
CONV_BASE = 8
SAMPLE_SEQS = 8

V7X_VMEM_BYTES = 64 * 1024 * 1024
VMEM_LIMIT = V7X_VMEM_BYTES - 4 * 1024 * 1024

NT_DIMS = (((1,), (1,)), ((), ()))
TN_DIMS = (((0,), (0,)), ((), ()))

N_STATES = 8
N_PARAMS = 14
STATE_BLOCKS = [
    (1, 1, ML_HEADS, ML_D, ML_D), (1, 1, ML_HEADS, ML_D), (1, 1, 1, LANES),
    (1, 1, SSD_HEADS, SSD_P, SSD_N), (1, 1, SSD_CONV - 1, SSD_CONV_DIM),
    (1, 1, SC_WIDTH - 1, BRANCH), (1, 1, 1, BRANCH), (1, 1, RG_CONV - 1, BRANCH),
]


def _softplus(x):
    return jnp.maximum(x, 0.0) + jnp.log1p(jnp.exp(-jnp.abs(x)))


def _silu(x):
    return x * jax.nn.sigmoid(x)


def _dot(a, b):
    return jnp.dot(a, b, preferred_element_type=F32)


def _dot_nt(a, b):
    return lax.dot_general(a, b, NT_DIMS, preferred_element_type=F32)


def _dot_tn(a, b):
    return lax.dot_general(a, b, TN_DIMS, preferred_element_type=F32)


def _rms_norm(x, w):
    return x * lax.rsqrt(jnp.mean(x * x, axis=-1, keepdims=True) + EPS) * w


def _inproj_kernel(x_ref, nw_ref, w_ref, o_ref, xn_ref):
    @pl.when(pl.program_id(1) == 0)
    def _():
        xn_ref[...] = _rms_norm(x_ref[...], nw_ref[...]).astype(BF16)

    o_ref[...] = _dot(xn_ref[...], w_ref[...])


def _inproj(x, norm_w, w, layer, tm):
    m = x.shape[0]
    tn = PROJ_W // 3
    return pl.pallas_call(
        _inproj_kernel,
        grid=(m // tm, PROJ_W // tn),
        in_specs=[
            pl.BlockSpec((tm, D_MODEL), lambda i, j: (i, 0)),
            pl.BlockSpec((None, 1, D_MODEL), lambda i, j: (layer, 0, 0)),
            pl.BlockSpec((None, D_MODEL, tn), lambda i, j: (layer, 0, j)),
        ],
        out_specs=pl.BlockSpec((tm, tn), lambda i, j: (i, j)),
        out_shape=jax.ShapeDtypeStruct((m, PROJ_W), F32),
        scratch_shapes=[pltpu.VMEM((tm, D_MODEL), BF16)],
        compiler_params=pltpu.CompilerParams(
            dimension_semantics=("arbitrary", "arbitrary"), vmem_limit_bytes=VMEM_LIMIT),
        name="inproj",
    )(x, norm_w, w)


def _outproj_kernel(final, mix_ref, x_ref, w_ref, fnw_ref, o_ref):
    y = x_ref[...] + _dot(mix_ref[...].astype(BF16), w_ref[...])
    if final:
        y = _rms_norm(y, fnw_ref[...])
    o_ref[...] = y


def _outproj(mix, x, w, fnw, layer, final, tm):
    m = x.shape[0]
    return pl.pallas_call(
        functools.partial(_outproj_kernel, final),
        grid=(m // tm,),
        in_specs=[
            pl.BlockSpec((tm, D_MODEL), lambda i: (i, 0)),
            pl.BlockSpec((tm, D_MODEL), lambda i: (i, 0)),
            pl.BlockSpec((None, D_MODEL, D_MODEL), lambda i: (layer, 0, 0)),
            pl.BlockSpec((1, D_MODEL), lambda i: (0, 0)),
        ],
        out_specs=pl.BlockSpec((tm, D_MODEL), lambda i: (i, 0)),
        out_shape=jax.ShapeDtypeStruct((m, D_MODEL), F32),
        compiler_params=pltpu.CompilerParams(
            dimension_semantics=("arbitrary",), vmem_limit_bytes=VMEM_LIMIT),
        name="outproj",
    )(mix, x, w, fnw)


def _causal_conv(ext_ref, u, w_ref, width, rows):
    lo = CONV_BASE - (width - 1)
    ext_ref[CONV_BASE:CONV_BASE + rows, :] = u
    acc = ext_ref[lo:lo + rows, :] * w_ref[0:1, :]
    for j in range(1, width):
        acc = acc + ext_ref[lo + j:lo + j + rows, :] * w_ref[j:j + 1, :]
    tail = ext_ref[lo + rows:CONV_BASE + rows, :]
    ext_ref[lo:CONV_BASE, :] = tail
    return acc, tail


def _linear_scan(a, b, h0, rows):
    if rows < 8:
        h = h0
        out = []
        for t in range(rows):
            h = a[t:t + 1, :] * h + b[t:t + 1, :]
            out.append(h)
        return jnp.concatenate(out, axis=0)
    row = lax.broadcasted_iota(jnp.int32, a.shape, 0)
    d = 1
    while d < rows:
        keep = row >= d
        a_prev = jnp.where(keep, pltpu.roll(a, d, 0), 1.0)
        b_prev = jnp.where(keep, pltpu.roll(b, d, 0), 0.0)
        b = a * b_prev + b
        a = a * a_prev
        d *= 2
    return b + a * h0


def _cumsum_rows(z, rows):
    if rows < 8:
        out = [z[0:1, :]]
        for t in range(1, rows):
            out.append(out[-1] + z[t:t + 1, :])
        return jnp.concatenate(out, axis=0)
    row = lax.broadcasted_iota(jnp.int32, z.shape, 0)
    d = 1
    while d < rows:
        z = z + jnp.where(row >= d, pltpu.roll(z, d, 0), 0.0)
        d *= 2
    return z


def _load_state(state_in, state_out, ssd_ext, sc_ext, rg_ext):
    c_in, n_in, m_in, s_in, ssdcv_in, sccv_in, rgh_in, rgcv_in = state_in
    c_out, n_out, m_out, s_out, ssdcv_out, sccv_out, rgh_out, rgcv_out = state_out
    c_out[0:1] = c_in[...]
    n_out[0:1] = n_in[...]
    m_out[0:1] = m_in[...]
    s_out[0:1] = s_in[...]
    rgh_out[0:1] = rgh_in[...]
    for g in range(ssd_ext.shape[0]):
        ssd_ext[g, CONV_BASE - (SSD_CONV - 1):CONV_BASE, :] = ssdcv_in[0, g]
        sc_ext[g, CONV_BASE - (SC_WIDTH - 1):CONV_BASE, :] = sccv_in[0, g]
        rg_ext[g, CONV_BASE - (RG_CONV - 1):CONV_BASE, :] = rgcv_in[0, g]
    for out in state_out:
        if out.shape[0] > 1:
            out[1:] = jnp.zeros((out.shape[0] - 1,) + out.shape[1:], out.dtype)


def _gate_prep(L, pcols, bias_ref, alog_ref):
    small = pcols(C_SMALL, LANES) + bias_ref[...]
    lane = lax.broadcasted_iota(jnp.int32, (L, LANES), 1)
    sp = _softplus(small)
    logsig = -_softplus(-small)
    a_coef = -jnp.exp(alog_ref[...])
    z = jnp.where((lane >= 4) & (lane < 8), logsig,
                  jnp.where((lane >= 8) & (lane < 16), sp * a_coef, 0.0))
    rr = lax.broadcasted_iota(jnp.int32, (L, L), 0)
    cc = lax.broadcasted_iota(jnp.int32, (L, L), 1)
    causal = cc <= rr
    cum = _cumsum_rows(z, L)
    packed = jnp.where(lane < 4, small,
                       jnp.where(lane < 16, cum, jnp.where(lane < 24, sp, 0.0)))
    packed_t = packed.T
    return small, sp, cum, packed_t, causal


def _group_mlstm(L, pcols, mix_store, gates, c_out, n_out, m_out, mlnw_ref):
    small, _, cum, packed_t, causal = gates
    m_row = m_out[...]
    lane1 = lax.broadcasted_iota(jnp.int32, (1, LANES), 1)
    m_row_new = m_row
    for h in range(ML_HEADS):
        q = pcols(C_Q + h * ML_D, ML_D)
        k = pcols(C_K + h * ML_D, ML_D) * (ML_D ** -0.5)
        v = pcols(C_V + h * ML_D, ML_D)
        qb, kb, vb = q.astype(BF16), k.astype(BF16), v.astype(BF16)
        yield
        c_h = c_out[h]
        if L < 8:
            qk = _dot_nt(qb, kb)
            qc = _dot(qb, c_h.astype(BF16))
        yield
        b_col = cum[:, 4 + h:5 + h]
        i_col = small[:, h:h + 1]
        b_row = packed_t[4 + h:5 + h, :]
        i_row = packed_t[h:h + 1, :]
        m_h = m_row[:, 4 + h:5 + h]
        dmat = jnp.where(causal, b_col - b_row + i_row, -jnp.inf)
        inter = b_col + m_h
        yield
        m_t = jnp.maximum(inter, jnp.max(dmat, axis=1, keepdims=True))
        yield
        w_intra = jnp.exp(dmat - m_t)
        w_inter = jnp.exp(inter - m_t)
        yield
        if L >= 8:
            qk = _dot_nt(qb, kb)
        s = qk * w_intra
        n_h = n_out[h:h + 1, :]
        yield
        if L >= 8:
            qc = _dot(qb, c_h.astype(BF16))
        num = _dot(s.astype(BF16), vb) + w_inter * qc
        yield
        den = jnp.sum(s, axis=1, keepdims=True) + w_inter * jnp.sum(q * n_h, axis=1, keepdims=True)
        yield
        hh = num / jnp.maximum(jnp.abs(den), jnp.exp(-m_t))
        yield
        b_last = cum[L - 1:L, 4 + h:5 + h]
        g = b_last - b_col + i_col
        inter_end = b_last + m_h
        m_new = jnp.maximum(inter_end, jnp.max(g, axis=0, keepdims=True))
        yield
        wg = jnp.exp(g - m_new)
        we = jnp.exp(inter_end - m_new)
        kw = k * wg
        yield
        c_out[h] = we * c_h + _dot_tn(kw.astype(BF16), vb)
        yield
        n_out[h:h + 1, :] = we * n_h + jnp.sum(kw, axis=0, keepdims=True)
        m_row_new = jnp.where(lane1 == 4 + h, m_new, m_row_new)
        yield
        mu = jnp.mean(hh, axis=-1, keepdims=True)
        yield
        hc = hh - mu
        var = jnp.mean(hc * hc, axis=-1, keepdims=True)
        yield
        hn = hc * lax.rsqrt(var + EPS) * mlnw_ref[:, h * ML_D:(h + 1) * ML_D]
        o = pcols(C_O + h * ML_D, ML_D)
        zg = pcols(C_Z + h * ML_D, ML_D)
        mix_store(h * ML_D, jax.nn.sigmoid(o) * hn * _silu(zg))
        if h == ML_HEADS - 1:
            m_out[...] = m_row_new
        yield MAJOR


def _group_ssd(L, pcols, mix_store, gates, s_out, ssdcv_out, ssd_ext, yssd_ref,
               ssdcw_ref, ssdcb_ref, drow_ref, ssdnw_ref):
    _, sp, cum, packed_t, causal = gates
    conv, tail = _causal_conv(ssd_ext, pcols(C_XBC, SSD_CONV_DIM), ssdcw_ref, SSD_CONV, L)
    ssdcv_out[...] = tail
    yield
    xbc = _silu(conv + ssdcb_ref[...])
    xs_all = xbc[:, 0:BRANCH]
    yield
    cb, bm_b, cm_b = [], [], []
    for g in range(SSD_GROUPS):
        bm_g = xbc[:, BRANCH + g * SSD_N:BRANCH + (g + 1) * SSD_N].astype(BF16)
        cm_g = xbc[:, BRANCH + (SSD_GROUPS + g) * SSD_N:BRANCH + (SSD_GROUPS + g + 1) * SSD_N].astype(BF16)
        bm_b.append(bm_g)
        cm_b.append(cm_g)
        cb.append(_dot_nt(cm_g, bm_g))
        yield MAJOR if g == SSD_GROUPS - 1 else None
    rep = SSD_HEADS // SSD_GROUPS
    for h in range(SSD_HEADS):
        g = h // rep
        xs_h = xs_all[:, h * SSD_P:(h + 1) * SSD_P]
        s_h = s_out[h]
        if L < 8:
            cs = _dot_nt(cm_b[g], s_h.astype(BF16))
        yield
        acs_col = cum[:, 8 + h:9 + h]
        acs_row = packed_t[8 + h:9 + h, :]
        dt_row = packed_t[16 + h:17 + h, :]
        dt_col = sp[:, 8 + h:9 + h]
        decay = jnp.exp(jnp.where(causal, acs_col - acs_row, -jnp.inf))
        yield
        scores = cb[g] * decay * dt_row
        yield
        if L >= 8:
            cs = _dot_nt(cm_b[g], s_h.astype(BF16))
        y = _dot(scores.astype(BF16), xs_h.astype(BF16)) + jnp.exp(acs_col) * cs
        yield
        a_last = cum[L - 1:L, 8 + h:9 + h]
        w = jnp.exp(a_last - acs_col) * dt_col
        yield
        s_out[h] = jnp.exp(a_last) * s_h + _dot_tn((xs_h * w).astype(BF16), bm_b[g])
        yield
        yssd_ref[:, h * SSD_P:(h + 1) * SSD_P] = y + drow_ref[:, h * SSD_P:(h + 1) * SSD_P] * xs_h
        yield MAJOR
    yz = yssd_ref[...] * _silu(pcols(C_SSD_Z, BRANCH))
    yield
    gw = BRANCH // SSD_GROUPS
    for g in range(SSD_GROUPS):
        seg = yz[:, g * gw:(g + 1) * gw]
        seg = seg * lax.rsqrt(jnp.mean(seg * seg, axis=-1, keepdims=True) + EPS)
        mix_store(BRANCH + g * gw, seg * ssdnw_ref[:, g * gw:(g + 1) * gw])
        yield MAJOR if g == SSD_GROUPS - 1 else None


def _group_sconv(L, pcols, mix_store, sccv_out, sc_ext, sccw_ref):
    u = pcols(C_SC_C, BRANCH) * pcols(C_SC_H, BRANCH)
    yield
    cu, tail = _causal_conv(sc_ext, u, sccw_ref, SC_WIDTH, L)
    sccv_out[...] = tail
    yield
    mix_store(2 * BRANCH, pcols(C_SC_B, BRANCH) * cu * _silu(pcols(C_SC_Z, BRANCH)))
    yield MAJOR


def _group_rglru(L, pcols, mix_store, rgh_out, rgcv_out, rg_ext,
                 rgcw_ref, rgcb_ref, rgw_ref, rgba_ref, rgbx_ref, rglam_ref):
    conv, tail = _causal_conv(rg_ext, pcols(C_RG_X, BRANCH), rgcw_ref, RG_CONV, L)
    rgcv_out[...] = tail
    xr = conv + rgcb_ref[...]
    yield
    pre = _dot(xr.astype(BF16), rgw_ref[...])
    yield
    r = jax.nn.sigmoid(pre[:, 0:BRANCH] + rgba_ref[...])
    ig = jax.nn.sigmoid(pre[:, BRANCH:2 * BRANCH] + rgbx_ref[...])
    yield
    log_a = (-RG_C * r) * _softplus(-rglam_ref[...])
    a = jnp.exp(log_a)
    bterm = jnp.sqrt(-jnp.tanh(log_a) * (a * a + 1.0)) * (ig * xr)
    yield MAJOR
    h_rg = _linear_scan(a, bterm, rgh_out[...], L)
    rgh_out[...] = h_rg[L - 1:L, :]
    yield
    mix_store(3 * BRANCH, h_rg * _silu(pcols(C_RG_Z, BRANCH)))
    yield MAJOR


def _split_refs(refs, n_lead, n_carried, n_out_lead):
    lead = refs[:n_lead]
    state_in = refs[n_lead:n_lead + N_STATES]
    params = refs[n_lead + N_STATES:n_lead + N_STATES + N_PARAMS]
    rest = refs[n_lead + N_STATES + N_PARAMS + n_carried:]
    out_lead = rest[:n_out_lead]
    state_out = rest[n_out_lead:n_out_lead + N_STATES]
    scratch = rest[n_out_lead + N_STATES:]
    return lead, state_in, params, out_lead, state_out, scratch


def _seq_groups(L, seq, pcols, mix_store, state_out, scratch, params):
    (bias_ref, alog_ref, mlnw_ref, ssdcw_ref, ssdcb_ref, drow_ref, ssdnw_ref, sccw_ref, rgcw_ref,
     rgcb_ref, rgw_ref, rgba_ref, rgbx_ref, rglam_ref) = params
    c_out, n_out, m_out, s_out, ssdcv_out, sccv_out, rgh_out, rgcv_out = [r.at[0, seq] for r in state_out]
    ssd_ext, sc_ext, rg_ext, yssd_ref = [r.at[seq] for r in scratch]
    gates = _gate_prep(L, pcols, bias_ref, alog_ref)
    return [
        _group_mlstm(L, pcols, mix_store, gates, c_out, n_out, m_out, mlnw_ref),
        _group_ssd(L, pcols, mix_store, gates, s_out, ssdcv_out, ssd_ext, yssd_ref,
                   ssdcw_ref, ssdcb_ref, drow_ref, ssdnw_ref),
        _group_sconv(L, pcols, mix_store, sccv_out, sc_ext, sccw_ref),
        _group_rglru(L, pcols, mix_store, rgh_out, rgcv_out, rg_ext,
                     rgcw_ref, rgcb_ref, rgw_ref, rgba_ref, rgbx_ref, rglam_ref),
    ]


def _mixer_kernel(rows, nseq, n_carried, *refs):
    (proj_ref,), state_in, params, (mix_ref,), state_out, scratch = _split_refs(refs, 1, n_carried, 1)

    @pl.when(pl.program_id(1) == 0)
    def _():
        _load_state(state_in, state_out, *scratch[:3])

    def chain(seq):
        def pcols(start, width):
            return proj_ref[seq, :, start:start + width]

        def mix_store(start, val):
            mix_ref[seq, :, start:start + val.shape[1]] = val.astype(mix_ref.dtype)

        for group in _seq_groups(rows, seq, pcols, mix_store, state_out, scratch, params):
            yield from group

    chains = [chain(seq) for seq in range(nseq)]
    done = object()
    while chains:
        chains = [c for c in chains if next(c, done) is not done]


def _state_specs(state_layer, layer, carried, nseq):
    def at_layer(lyr, blk):
        blk = (blk[0], nseq) + blk[2:]
        tail = (0,) * (len(blk) - 2)
        return pl.BlockSpec(blk, lambda b, c: (lyr, b) + tail)

    in_specs = [at_layer(state_layer, blk) for blk in STATE_BLOCKS]
    if carried:
        out_specs = [at_layer(layer, blk) for blk in STATE_BLOCKS]
    else:
        assert layer == 0
        out_specs = [at_layer(0, (DEPTH,) + blk[1:]) for blk in STATE_BLOCKS]
    return in_specs, out_specs


def _state_shapes(nb):
    return [jax.ShapeDtypeStruct((DEPTH, nb) + blk[2:], F32) for blk in STATE_BLOCKS]


def _mix_scratch(rows, nseq):
    return [
        pltpu.VMEM((nseq, CONV_BASE + rows, SSD_CONV_DIM), F32),
        pltpu.VMEM((nseq, CONV_BASE + rows, BRANCH), F32),
        pltpu.VMEM((nseq, CONV_BASE + rows, BRANCH), F32),
        pltpu.VMEM((nseq, rows, BRANCH), F32),
    ]


def _mixer(proj, states, state_layer, params, layer, carried, rows, nseq, mix_dtype):
    nb, t, _ = proj.shape
    carried = list(carried) if carried is not None else []
    st_in, st_out = _state_specs(state_layer, layer, carried, nseq)
    in_specs = [pl.BlockSpec((nseq, rows, PROJ_W), lambda b, c: (b, c, 0))] + st_in
    in_specs += [pl.BlockSpec((None,) + p.shape[1:], lambda b, c: (layer, 0, 0)) for p in params]
    in_specs += [pl.BlockSpec(memory_space=pl.ANY) for _ in carried]
    first_carried = 1 + N_STATES + N_PARAMS
    outs = pl.pallas_call(
        functools.partial(_mixer_kernel, rows, nseq, len(carried)),
        grid=(nb // nseq, t // rows),
        in_specs=in_specs,
        out_specs=[pl.BlockSpec((nseq, rows, D_MODEL), lambda b, c: (b, c, 0))] + st_out,
        out_shape=[jax.ShapeDtypeStruct((nb, t, D_MODEL), mix_dtype)] + _state_shapes(nb),
        input_output_aliases={first_carried + i: 1 + i for i in range(len(carried))},
        scratch_shapes=_mix_scratch(rows, nseq),
        compiler_params=pltpu.CompilerParams(
            dimension_semantics=("arbitrary", "arbitrary"), vmem_limit_bytes=VMEM_LIMIT),
        name=f"mixer_rows{rows}",
    )(proj, *states, *params, *carried)
    return outs[0], list(outs[1:])


def _layer_kernel(final, n_carried, *refs):
    lead, state_in, params, (y_ref,), state_out, scratch = _split_refs(refs, 6, n_carried, 1)
    x_ref, xnext_ref, nw_ref, win_ref, wout_ref, fnw_ref = lead
    proj_ref, mix_ref = scratch[:2]
    mix_scratch = scratch[2:]
    step = pl.program_id(0) * pl.num_programs(1) + pl.program_id(1)

    @pl.when(pl.program_id(1) == 0)
    def _():
        _load_state(state_in, state_out, *mix_scratch[:3])

    @pl.when(step == 0)
    def _():
        xn0 = _rms_norm(x_ref[0], nw_ref[...]).astype(BF16)
        for start in range(0, PROJ_W, PROJ_TILE):
            stop = min(start + PROJ_TILE, PROJ_W)
            proj_ref[0, :, start:stop] = _dot(xn0, win_ref[:, start:stop])

    tiles = [(s, min(s + PROJ_TILE, PROJ_W)) for s in range(0, PROJ_W, PROJ_TILE)]
    n_pieces = 18

    def chunk_step(cur, nxt):
        x = x_ref[0]
        xn_next = _rms_norm(xnext_ref[0], nw_ref[...]).astype(BF16)

        def project_some(piece):
            lo = (piece * len(tiles)) // n_pieces
            hi = ((piece + 1) * len(tiles)) // n_pieces
            for start, stop in tiles[lo:hi]:
                proj_ref[nxt, :, start:stop] = _dot(xn_next, win_ref[:, start:stop])

        def pcols(start, width):
            return proj_ref[cur, :, start:start + width]

        def mix_store(start, val):
            mix_ref[:, start:start + val.shape[1]] = val.astype(BF16)

        def out_project(group):
            lo, hi = group * BRANCH, (group + 1) * BRANCH
            return _dot(mix_ref[:, lo:hi], wout_ref[lo:hi, :])

        project_some(0)
        groups = _seq_groups(CHUNK, 0, pcols, mix_store, state_out, mix_scratch, params)
        piece = 1
        y = x
        for gi, group in enumerate(groups):
            for mark in group:
                if mark is MAJOR:
                    project_some(piece)
                    piece += 1
            y = y + out_project(gi)
        assert piece == n_pieces, piece
        if final:
            y = _rms_norm(y, fnw_ref[...])
        y_ref[0] = y

    for parity in range(2):
        pl.when(step % 2 == parity)(functools.partial(chunk_step, parity, 1 - parity))


def _layer_prompt(x, norm_w, w_in, w_out, fnw, states, params, layer, carried, final):
    nb, t, _ = x.shape
    nc = t // CHUNK
    carried = list(carried) if carried is not None else []
    st_in, st_out = _state_specs(0, layer, carried, 1)
    resident = dict(pipeline_mode=pl.Buffered(1))

    def next_chunk(b, c):
        step = jnp.minimum(b * nc + c + 1, nb * nc - 1)
        return (step // nc, step % nc, 0)

    in_specs = [
        pl.BlockSpec((1, CHUNK, D_MODEL), lambda b, c: (b, c, 0)),
        pl.BlockSpec((1, CHUNK, D_MODEL), next_chunk),
        pl.BlockSpec((None, 1, D_MODEL), lambda b, c: (layer, 0, 0)),
        pl.BlockSpec((None, D_MODEL, PROJ_W), lambda b, c: (layer, 0, 0), **resident),
        pl.BlockSpec((None, D_MODEL, D_MODEL), lambda b, c: (layer, 0, 0), **resident),
        pl.BlockSpec((1, D_MODEL), lambda b, c: (0, 0)),
    ] + st_in
    in_specs += [pl.BlockSpec((None,) + p.shape[1:], lambda b, c: (layer, 0, 0)) for p in params]
    in_specs += [pl.BlockSpec(memory_space=pl.ANY) for _ in carried]
    first_carried = 6 + N_STATES + N_PARAMS
    outs = pl.pallas_call(
        functools.partial(_layer_kernel, final, len(carried)),
        grid=(nb, nc),
        in_specs=in_specs,
        out_specs=[pl.BlockSpec((1, CHUNK, D_MODEL), lambda b, c: (b, c, 0))] + st_out,
        out_shape=[jax.ShapeDtypeStruct((nb, t, D_MODEL), F32)] + _state_shapes(nb),
        input_output_aliases={first_carried + i: 1 + i for i in range(len(carried))},
        scratch_shapes=[pltpu.VMEM((2, CHUNK, PROJ_W), F32), pltpu.VMEM((CHUNK, D_MODEL), BF16)]
        + _mix_scratch(CHUNK, 1),
        compiler_params=pltpu.CompilerParams(
            dimension_semantics=("arbitrary", "arbitrary"), vmem_limit_bytes=VMEM_LIMIT),
        name="layer_prompt",
    )(x, x, norm_w, w_in, w_out, fnw, *states, *params, *carried)
    return outs[0], list(outs[1:])


O_I, O_SSD_Z, O_XBC, O_DT, O_SC, O_RG, O_END = 2560, 2568, 3080, 4104, 4112, 6160, 7184
REORDER_TILE = 512
GATE_TILE = C_SMALL // REORDER_TILE


def _reorder_kernel(src_ref, if_ref, dt_ref, o_ref):
    j = pl.program_id(1)

    @pl.when(j < GATE_TILE)
    def _():
        o_ref[...] = src_ref[0].T.astype(BF16)

    @pl.when(j == GATE_TILE)
    def _():
        zeros = jnp.zeros((REORDER_TILE - 24, D_MODEL), F32)
        small = jnp.concatenate([if_ref[0], dt_ref[0], dt_ref[0], zeros], axis=0)
        o_ref[...] = small.T.astype(BF16)


def _reorder_w_in(w_in):
    nl, d, n = w_in.shape
    assert n == O_END and d == D_MODEL
    assert C_SSD_Z % REORDER_TILE == 0 and C_SC_B % REORDER_TILE == 0 and C_SMALL % REORDER_TILE == 0
    w_t = jnp.swapaxes(w_in, 1, 2)

    def src_row(l, j):
        t_ssd, t_sc = C_SSD_Z // REORDER_TILE, C_SC_B // REORDER_TILE
        row = jnp.where(j < t_ssd, j * REORDER_TILE,
                        jnp.where(j < t_sc, O_SSD_Z + (j - t_ssd) * REORDER_TILE,
                                  O_SC + (j - t_sc) * REORDER_TILE))
        row = jnp.where(j < GATE_TILE, row, 0)
        return (l, pl.multiple_of(row, 8), 0)

    return pl.pallas_call(
        _reorder_kernel,
        grid=(nl, GATE_TILE + 1),
        in_specs=[
            pl.BlockSpec((pl.Element(1), pl.Element(REORDER_TILE), pl.Element(d)), src_row),
            pl.BlockSpec((pl.Element(1), pl.Element(8), pl.Element(d)), lambda l, j: (l, O_I, 0)),
            pl.BlockSpec((pl.Element(1), pl.Element(8), pl.Element(d)), lambda l, j: (l, O_DT, 0)),
        ],
        out_specs=pl.BlockSpec((None, d, REORDER_TILE), lambda l, j: (l, 0, j)),
        out_shape=jax.ShapeDtypeStruct((nl, d, PROJ_W), BF16),
        compiler_params=pltpu.CompilerParams(
            dimension_semantics=("arbitrary", "arbitrary"), vmem_limit_bytes=VMEM_LIMIT),
        name="reorder_w_in",
    )(w_t, w_t, w_t)


def _block_diag(w):
    nl, nblk, d, e = w.shape
    eye = jnp.eye(nblk, dtype=w.dtype)
    return jnp.einsum('lnde,nm->lndme', w, eye).reshape(nl, nblk * d, nblk * e)


def _lane_row(parts, width=LANES):
    nl = parts[0][1].shape[0]
    row = jnp.zeros((nl, width), F32)
    for off, p in parts:
        row = lax.dynamic_update_slice(row, p.astype(F32), (0, off))
    return row[:, None, :]


def kernel(x_prompt, x_sample, state_mlstm_C, state_mlstm_n, state_mlstm_m, state_ssd, state_ssd_conv, state_sconv_conv, state_rglru_h, state_rglru_conv, norm_w, w_in, ml_i_bias, ml_f_bias, ml_norm_w, ssd_conv_w, ssd_conv_b, ssd_dt_bias, ssd_A_log, ssd_D, ssd_norm_w, sc_conv_w, rg_conv_w, rg_conv_b, rg_wa, rg_ba, rg_wx, rg_bx, rg_lambda, w_out, final_norm_w):
    bp, tp, _ = x_prompt.shape
    bs, ts, _ = x_sample.shape

    w_in_r = _reorder_w_in(w_in)
    w_out_b = w_out.astype(BF16)
    rg_w = jnp.concatenate([_block_diag(rg_wa), _block_diag(rg_wx)], axis=-1).astype(BF16)
    bias_row = _lane_row([(0, ml_i_bias), (4, ml_f_bias), (8, ssd_dt_bias), (16, ssd_dt_bias)])
    alog_row = _lane_row([(8, ssd_A_log)])
    d_row = jnp.repeat(ssd_D, SSD_P, axis=-1)[:, None, :]
    row = lambda p: p[:, None, :]
    params = [bias_row, alog_row, row(ml_norm_w), ssd_conv_w, row(ssd_conv_b), d_row, row(ssd_norm_w),
              sc_conv_w, rg_conv_w, row(rg_conv_b), rg_w, row(rg_ba), row(rg_bx), row(rg_lambda)]
    norm_rows = row(norm_w)

    def pad_m(m):
        return jnp.pad(m, ((0, 0), (0, 0), (4, LANES - 4 - ML_HEADS)))[:, :, None, :]

    sample_states = [state_mlstm_C, state_mlstm_n, pad_m(state_mlstm_m), state_ssd, state_ssd_conv,
                     state_sconv_conv, state_rglru_h[:, :, None, :], state_rglru_conv]
    prompt_states = [
        jnp.zeros((1, bp, ML_HEADS, ML_D, ML_D), F32),
        jnp.zeros((1, bp, ML_HEADS, ML_D), F32),
        pad_m(jnp.full((1, bp, ML_HEADS), NEG_BIG, F32)),
        jnp.zeros((1, bp, SSD_HEADS, SSD_P, SSD_N), F32),
        jnp.zeros((1, bp, SSD_CONV - 1, SSD_CONV_DIM), F32),
        jnp.zeros((1, bp, SC_WIDTH - 1, BRANCH), F32),
        jnp.zeros((1, bp, 1, BRANCH), F32),
        jnp.zeros((1, bp, RG_CONV - 1, BRANCH), F32),
    ]

    xp = x_prompt
    xs = x_sample.reshape(bs * ts, D_MODEL)
    fnw = final_norm_w[None, :]
    st_p, st_s = None, None
    for l in range(DEPTH):
        final = l == DEPTH - 1
        xp, st_p = _layer_prompt(xp, norm_rows, w_in_r, w_out_b, fnw, prompt_states, params, l, st_p, final)
        proj_s = _inproj(xs, norm_rows, w_in_r, l, 512).reshape(bs, ts, PROJ_W)
        mix_s, st_s = _mixer(proj_s, sample_states, l, params, l, st_s, ts, SAMPLE_SEQS, F32)
        xs = _outproj(mix_s.reshape(bs * ts, D_MODEL), xs, w_out_b, fnw, l, final, 512)

    def unpack(states):
        states = list(states)
        states[2] = states[2][:, :, 0, 4:4 + ML_HEADS]
        states[6] = states[6][:, :, 0, :]
        return tuple(states)

    return (xp, xs.reshape(bs, ts, D_MODEL)) + unpack(st_p) + unpack(st_s)
```

```python
import functools

import jax
import jax.numpy as jnp
from jax import lax
from jax.experimental import pallas as pl
from jax.experimental.pallas import tpu as pltpu

F32 = jnp.float32
BF16 = jnp.bfloat16

D_MODEL = 2048
DEPTH = 2
BRANCH = 512
CHUNK = 128
EPS = 1e-6
ML_HEADS = 4
ML_D = 128
NEG_BIG = -1e30
SSD_HEADS = 8
SSD_P = 64
SSD_GROUPS = 2
SSD_N = 128
SSD_CONV = 4
SSD_CONV_DIM = 1024
SC_WIDTH = 3
RG_CONV = 4
RG_C = 8.0
RG_BLOCKS = 8

C_Q, C_K, C_V, C_O, C_Z = 0, 512, 1024, 1536, 2048
C_SSD_Z, C_XBC = 2560, 3072
C_SC_B, C_SC_C, C_SC_H, C_SC_Z = 4096, 4608, 5120, 5632
C_RG_X, C_RG_Z = 6144, 6656
C_SMALL = 7168
PROJ_W = 7296
PROJ_TILE = 256
MAJOR = "major"
PROMPT_HEADS_ABREAST = 1
LANES = 128
CONV_BASE = 8
SAMPLE_SEQS = 8

V7X_VMEM_BYTES = 64 * 1024 * 1024
VMEM_LIMIT = V7X_VMEM_BYTES - 4 * 1024 * 1024

NT_DIMS = (((1,), (1,)), ((), ()))
TN_DIMS = (((0,), (0,)), ((), ()))

N_STATES = 8
N_PARAMS = 14
STATE_BLOCKS = [
    (1, 1, ML_HEADS, ML_D, ML_D), (1, 1, ML_HEADS, ML_D), (1, 1, 1, LANES),
    (1, 1, SSD_HEADS, SSD_P, SSD_N), (1, 1, SSD_CONV - 1, SSD_CONV_DIM),
    (1, 1, SC_WIDTH - 1, BRANCH), (1, 1, 1, BRANCH), (1, 1, RG_CONV - 1, BRANCH),
]


def _softplus(x):
    return jnp.maximum(x, 0.0) + jnp.log1p(jnp.exp(-jnp.abs(x)))


def _silu(x):
    return x * jax.nn.sigmoid(x)


def _dot(a, b):
    return jnp.dot(a, b, preferred_element_type=F32)


def _dot_nt(a, b):
    return lax.dot_general(a, b, NT_DIMS, preferred_element_type=F32)


def _dot_tn(a, b):
    return lax.dot_general(a, b, TN_DIMS, preferred_element_type=F32)


def _rms_norm(x, w):
    return x * lax.rsqrt(jnp.mean(x * x, axis=-1, keepdims=True) + EPS) * w


def _inproj_kernel(x_ref, nw_ref, w_ref, o_ref, xn_ref):
    @pl.when(pl.program_id(1) == 0)
    def _():
        xn_ref[...] = _rms_norm(x_ref[...], nw_ref[...]).astype(BF16)

    o_ref[...] = _dot(xn_ref[...], w_ref[...])


def _inproj(x, norm_w, w, layer, tm):
    m = x.shape[0]
    tn = PROJ_W // 3
    return pl.pallas_call(
        _inproj_kernel,
        grid=(m // tm, PROJ_W // tn),
        in_specs=[
            pl.BlockSpec((tm, D_MODEL), lambda i, j: (i, 0)),
            pl.BlockSpec((None, 1, D_MODEL), lambda i, j: (layer, 0, 0)),
            pl.BlockSpec((None, D_MODEL, tn), lambda i, j: (layer, 0, j)),
        ],
        out_specs=pl.BlockSpec((tm, tn), lambda i, j: (i, j)),
        out_shape=jax.ShapeDtypeStruct((m, PROJ_W), F32),
        scratch_shapes=[pltpu.VMEM((tm, D_MODEL), BF16)],
        compiler_params=pltpu.CompilerParams(
            dimension_semantics=("arbitrary", "arbitrary"), vmem_limit_bytes=VMEM_LIMIT),
        name="inproj",
    )(x, norm_w, w)


def _outproj_kernel(final, mix_ref, x_ref, w_ref, fnw_ref, o_ref):
    y = x_ref[...] + _dot(mix_ref[...].astype(BF16), w_ref[...])
    if final:
        y = _rms_norm(y, fnw_ref[...])
    o_ref[...] = y


def _outproj(mix, x, w, fnw, layer, final, tm):
    m = x.shape[0]
    return pl.pallas_call(
        functools.partial(_outproj_kernel, final),
        grid=(m // tm,),
        in_specs=[
            pl.BlockSpec((tm, D_MODEL), lambda i: (i, 0)),
            pl.BlockSpec((tm, D_MODEL), lambda i: (i, 0)),
            pl.BlockSpec((None, D_MODEL, D_MODEL), lambda i: (layer, 0, 0)),
            pl.BlockSpec((1, D_MODEL), lambda i: (0, 0)),
        ],
        out_specs=pl.BlockSpec((tm, D_MODEL), lambda i: (i, 0)),
        out_shape=jax.ShapeDtypeStruct((m, D_MODEL), F32),
        compiler_params=pltpu.CompilerParams(
            dimension_semantics=("arbitrary",), vmem_limit_bytes=VMEM_LIMIT),
        name="outproj",
    )(mix, x, w, fnw)


def _causal_conv(ext_ref, u, w_ref, width, rows):
    lo = CONV_BASE - (width - 1)
    ext_ref[CONV_BASE:CONV_BASE + rows, :] = u
    acc = ext_ref[lo:lo + rows, :] * w_ref[0:1, :]
    for j in range(1, width):
        acc = acc + ext_ref[lo + j:lo + j + rows, :] * w_ref[j:j + 1, :]
    tail = ext_ref[lo + rows:CONV_BASE + rows, :]
    ext_ref[lo:CONV_BASE, :] = tail
    return acc, tail


def _linear_scan(a, b, h0, rows):
    if rows < 8:
        h = h0
        out = []
        for t in range(rows):
            h = a[t:t + 1, :] * h + b[t:t + 1, :]
            out.append(h)
        return jnp.concatenate(out, axis=0)
    row = lax.broadcasted_iota(jnp.int32, a.shape, 0)
    d = 1
    while d < rows:
        keep = row >= d
        a_prev = jnp.where(keep, pltpu.roll(a, d, 0), 1.0)
        b_prev = jnp.where(keep, pltpu.roll(b, d, 0), 0.0)
        b = a * b_prev + b
        a = a * a_prev
        d *= 2
    return b + a * h0


def _cumsum_rows(z, rows):
    if rows < 8:
        out = [z[0:1, :]]
        for t in range(1, rows):
            out.append(out[-1] + z[t:t + 1, :])
        return jnp.concatenate(out, axis=0)
    row = lax.broadcasted_iota(jnp.int32, z.shape, 0)
    d = 1
    while d < rows:
        z = z + jnp.where(row >= d, pltpu.roll(z, d, 0), 0.0)
        d *= 2
    return z


def _load_state(state_in, state_out, ssd_ext, sc_ext, rg_ext):
    c_in, n_in, m_in, s_in, ssdcv_in, sccv_in, rgh_in, rgcv_in = state_in
    c_out, n_out, m_out, s_out, ssdcv_out, sccv_out, rgh_out, rgcv_out = state_out
    c_out[0:1] = c_in[...]
    n_out[0:1] = n_in[...]
    m_out[0:1] = m_in[...]
    s_out[0:1] = s_in[...]
    rgh_out[0:1] = rgh_in[...]
    for g in range(ssd_ext.shape[0]):
        ssd_ext[g, CONV_BASE - (SSD_CONV - 1):CONV_BASE, :] = ssdcv_in[0, g]
        sc_ext[g, CONV_BASE - (SC_WIDTH - 1):CONV_BASE, :] = sccv_in[0, g]
        rg_ext[g, CONV_BASE - (RG_CONV - 1):CONV_BASE, :] = rgcv_in[0, g]
    for out in state_out:
        if out.shape[0] > 1:
            out[1:] = jnp.zeros((out.shape[0] - 1,) + out.shape[1:], out.dtype)


def _gate_prep(L, pcols, bias_ref, alog_ref):
    small = pcols(C_SMALL, LANES) + bias_ref[...]
    lane = lax.broadcasted_iota(jnp.int32, (L, LANES), 1)
    sp = _softplus(small)
    logsig = -_softplus(-small)
    a_coef = -jnp.exp(alog_ref[...])
    z = jnp.where((lane >= 4) & (lane < 8), logsig,
                  jnp.where((lane >= 8) & (lane < 16), sp * a_coef, 0.0))
    rr = lax.broadcasted_iota(jnp.int32, (L, L), 0)
    cc = lax.broadcasted_iota(jnp.int32, (L, L), 1)
    causal = cc <= rr
    cum = _cumsum_rows(z, L)
    packed = jnp.where(lane < 4, small,
                       jnp.where(lane < 16, cum, jnp.where(lane < 24, sp, 0.0)))
    packed_t = packed.T
    return small, sp, cum, packed_t, causal


def _interleave(gens, width):
    gens = list(gens)
    done = object()
    for first in range(0, len(gens), width):
        live = gens[first:first + width]
        while live:
            still = []
            for gen in live:
                mark = next(gen, done)
                if mark is not done:
                    still.append(gen)
                    yield mark
            live = still


def _group_mlstm(L, pcols, mix_store, gates, c_out, n_out, m_out, mlnw_ref, width):
    small, _, cum, packed_t, causal = gates
    m_row = m_out[...]
    m_news = [None] * ML_HEADS

    def head(h):
        q = pcols(C_Q + h * ML_D, ML_D)
        k = pcols(C_K + h * ML_D, ML_D) * (ML_D ** -0.5)
        v = pcols(C_V + h * ML_D, ML_D)
        qb, kb, vb = q.astype(BF16), k.astype(BF16), v.astype(BF16)
        yield
        c_h = c_out[h]
        if L < 8:
            qk = _dot_nt(qb, kb)
            qc = _dot(qb, c_h.astype(BF16))
        yield
        b_col = cum[:, 4 + h:5 + h]
        i_col = small[:, h:h + 1]
        b_row = packed_t[4 + h:5 + h, :]
        i_row = packed_t[h:h + 1, :]
        m_h = m_row[:, 4 + h:5 + h]
        dmat = jnp.where(causal, b_col - b_row + i_row, -jnp.inf)
        inter = b_col + m_h
        yield
        m_t = jnp.maximum(inter, jnp.max(dmat, axis=1, keepdims=True))
        yield
        w_intra = jnp.exp(dmat - m_t)
        w_inter = jnp.exp(inter - m_t)
        yield
        if L >= 8:
            qk = _dot_nt(qb, kb)
        s = qk * w_intra
        n_h = n_out[h:h + 1, :]
        yield
        if L >= 8:
            qc = _dot(qb, c_h.astype(BF16))
        num = _dot(s.astype(BF16), vb) + w_inter * qc
        yield
        den = jnp.sum(s, axis=1, keepdims=True) + w_inter * jnp.sum(q * n_h, axis=1, keepdims=True)
        yield
        hh = num / jnp.maximum(jnp.abs(den), jnp.exp(-m_t))
        yield
        b_last = cum[L - 1:L, 4 + h:5 + h]
        g = b_last - b_col + i_col
        inter_end = b_last + m_h
        m_new = jnp.maximum(inter_end, jnp.max(g, axis=0, keepdims=True))
        yield
        wg = jnp.exp(g - m_new)
        we = jnp.exp(inter_end - m_new)
        kw = k * wg
        yield
        c_out[h] = we * c_h + _dot_tn(kw.astype(BF16), vb)
        yield
        n_out[h:h + 1, :] = we * n_h + jnp.sum(kw, axis=0, keepdims=True)
        m_news[h] = m_new
        yield
        mu = jnp.mean(hh, axis=-1, keepdims=True)
        yield
        hc = hh - mu
        var = jnp.mean(hc * hc, axis=-1, keepdims=True)
        yield
        hn = hc * lax.rsqrt(var + EPS) * mlnw_ref[:, h * ML_D:(h + 1) * ML_D]
        o = pcols(C_O + h * ML_D, ML_D)
        zg = pcols(C_Z + h * ML_D, ML_D)
        mix_store(h * ML_D, jax.nn.sigmoid(o) * hn * _silu(zg))
        yield MAJOR

    yield from _interleave([head(h) for h in range(ML_HEADS)], width)
    lane1 = lax.broadcasted_iota(jnp.int32, (1, LANES), 1)
    for h in range(ML_HEADS):
        m_row = jnp.where(lane1 == 4 + h, m_news[h], m_row)
    m_out[...] = m_row


def _group_ssd(L, pcols, mix_store, gates, s_out, ssdcv_out, ssd_ext, yssd_ref,
               ssdcw_ref, ssdcb_ref, drow_ref, ssdnw_ref, width):
    _, sp, cum, packed_t, causal = gates
    conv, tail = _causal_conv(ssd_ext, pcols(C_XBC, SSD_CONV_DIM), ssdcw_ref, SSD_CONV, L)
    ssdcv_out[...] = tail
    yield
    xbc = _silu(conv + ssdcb_ref[...])
    xs_all = xbc[:, 0:BRANCH]
    yield
    cb, bm_b, cm_b = [], [], []
    for g in range(SSD_GROUPS):
        bm_g = xbc[:, BRANCH + g * SSD_N:BRANCH + (g + 1) * SSD_N].astype(BF16)
        cm_g = xbc[:, BRANCH + (SSD_GROUPS + g) * SSD_N:BRANCH + (SSD_GROUPS + g + 1) * SSD_N].astype(BF16)
        bm_b.append(bm_g)
        cm_b.append(cm_g)
        cb.append(_dot_nt(cm_g, bm_g))
        yield MAJOR if g == SSD_GROUPS - 1 else None
    rep = SSD_HEADS // SSD_GROUPS

    def head(h):
        g = h // rep
        xs_h = xs_all[:, h * SSD_P:(h + 1) * SSD_P]
        s_h = s_out[h]
        if L < 8:
            cs = _dot_nt(cm_b[g], s_h.astype(BF16))
        yield
        acs_col = cum[:, 8 + h:9 + h]
        acs_row = packed_t[8 + h:9 + h, :]
        dt_row = packed_t[16 + h:17 + h, :]
        dt_col = sp[:, 8 + h:9 + h]
        decay = jnp.exp(jnp.where(causal, acs_col - acs_row, -jnp.inf))
        yield
        scores = cb[g] * decay * dt_row
        yield
        if L >= 8:
            cs = _dot_nt(cm_b[g], s_h.astype(BF16))
        y = _dot(scores.astype(BF16), xs_h.astype(BF16)) + jnp.exp(acs_col) * cs
        yield
        a_last = cum[L - 1:L, 8 + h:9 + h]
        w = jnp.exp(a_last - acs_col) * dt_col
        yield
        s_out[h] = jnp.exp(a_last) * s_h + _dot_tn((xs_h * w).astype(BF16), bm_b[g])
        yield
        yssd_ref[:, h * SSD_P:(h + 1) * SSD_P] = y + drow_ref[:, h * SSD_P:(h + 1) * SSD_P] * xs_h
        yield MAJOR

    yield from _interleave([head(h) for h in range(SSD_HEADS)], width)
    yz = yssd_ref[...] * _silu(pcols(C_SSD_Z, BRANCH))
    yield
    gw = BRANCH // SSD_GROUPS
    for g in range(SSD_GROUPS):
        seg = yz[:, g * gw:(g + 1) * gw]
        seg = seg * lax.rsqrt(jnp.mean(seg * seg, axis=-1, keepdims=True) + EPS)
        mix_store(BRANCH + g * gw, seg * ssdnw_ref[:, g * gw:(g + 1) * gw])
        yield MAJOR if g == SSD_GROUPS - 1 else None


def _group_sconv(L, pcols, mix_store, sccv_out, sc_ext, sccw_ref):
    u = pcols(C_SC_C, BRANCH) * pcols(C_SC_H, BRANCH)
    yield
    cu, tail = _causal_conv(sc_ext, u, sccw_ref, SC_WIDTH, L)
    sccv_out[...] = tail
    yield
    mix_store(2 * BRANCH, pcols(C_SC_B, BRANCH) * cu * _silu(pcols(C_SC_Z, BRANCH)))
    yield MAJOR


def _group_rglru(L, pcols, mix_store, rgh_out, rgcv_out, rg_ext,
                 rgcw_ref, rgcb_ref, rgw_ref, rgba_ref, rgbx_ref, rglam_ref):
    conv, tail = _causal_conv(rg_ext, pcols(C_RG_X, BRANCH), rgcw_ref, RG_CONV, L)
    rgcv_out[...] = tail
    xr = conv + rgcb_ref[...]
    yield
    pre = _dot(xr.astype(BF16), rgw_ref[...])
    yield
    r = jax.nn.sigmoid(pre[:, 0:BRANCH] + rgba_ref[...])
    ig = jax.nn.sigmoid(pre[:, BRANCH:2 * BRANCH] + rgbx_ref[...])
    yield
    log_a = (-RG_C * r) * _softplus(-rglam_ref[...])
    a = jnp.exp(log_a)
    bterm = jnp.sqrt(-jnp.tanh(log_a) * (a * a + 1.0)) * (ig * xr)
    yield MAJOR
    h_rg = _linear_scan(a, bterm, rgh_out[...], L)
    rgh_out[...] = h_rg[L - 1:L, :]
    yield
    mix_store(3 * BRANCH, h_rg * _silu(pcols(C_RG_Z, BRANCH)))
    yield MAJOR


def _split_refs(refs, n_lead, n_carried, n_out_lead):
    lead = refs[:n_lead]
    state_in = refs[n_lead:n_lead + N_STATES]
    params = refs[n_lead + N_STATES:n_lead + N_STATES + N_PARAMS]
    rest = refs[n_lead + N_STATES + N_PARAMS + n_carried:]
    out_lead = rest[:n_out_lead]
    state_out = rest[n_out_lead:n_out_lead + N_STATES]
    scratch = rest[n_out_lead + N_STATES:]
    return lead, state_in, params, out_lead, state_out, scratch


def _seq_groups(L, seq, pcols, mix_store, state_out, scratch, params, width=1):
    (bias_ref, alog_ref, mlnw_ref, ssdcw_ref, ssdcb_ref, drow_ref, ssdnw_ref, sccw_ref, rgcw_ref,
     rgcb_ref, rgw_ref, rgba_ref, rgbx_ref, rglam_ref) = params
    c_out, n_out, m_out, s_out, ssdcv_out, sccv_out, rgh_out, rgcv_out = [r.at[0, seq] for r in state_out]
    ssd_ext, sc_ext, rg_ext, yssd_ref = [r.at[seq] for r in scratch]
    gates = _gate_prep(L, pcols, bias_ref, alog_ref)
    return [
        _group_mlstm(L, pcols, mix_store, gates, c_out, n_out, m_out, mlnw_ref, width),
        _group_ssd(L, pcols, mix_store, gates, s_out, ssdcv_out, ssd_ext, yssd_ref,
                   ssdcw_ref, ssdcb_ref, drow_ref, ssdnw_ref, width),
        _group_sconv(L, pcols, mix_store, sccv_out, sc_ext, sccw_ref),
        _group_rglru(L, pcols, mix_store, rgh_out, rgcv_out, rg_ext,
                     rgcw_ref, rgcb_ref, rgw_ref, rgba_ref, rgbx_ref, rglam_ref),
    ]


def _mixer_kernel(rows, nseq, n_carried, *refs):
    (proj_ref,), state_in, params, (mix_ref,), state_out, scratch = _split_refs(refs, 1, n_carried, 1)

    @pl.when(pl.program_id(1) == 0)
    def _():
        _load_state(state_in, state_out, *scratch[:3])

    def chain(seq):
        lo, hi = seq * rows, (seq + 1) * rows

        def pcols(start, width):
            return proj_ref[lo:hi, start:start + width]

        def mix_store(start, val):
            mix_ref[lo:hi, start:start + val.shape[1]] = val.astype(mix_ref.dtype)

        for group in _seq_groups(rows, seq, pcols, mix_store, state_out, scratch, params):
            yield from group

    chains = [chain(seq) for seq in range(nseq)]
    done = object()
    while chains:
        chains = [c for c in chains if next(c, done) is not done]


def _state_specs(state_layer, layer, carried, nseq):
    def at_layer(lyr, blk):
        blk = (blk[0], nseq) + blk[2:]
        tail = (0,) * (len(blk) - 2)
        return pl.BlockSpec(blk, lambda b, c: (lyr, b) + tail)

    in_specs = [at_layer(state_layer, blk) for blk in STATE_BLOCKS]
    if carried:
        out_specs = [at_layer(layer, blk) for blk in STATE_BLOCKS]
    else:
        assert layer == 0
        out_specs = [at_layer(0, (DEPTH,) + blk[1:]) for blk in STATE_BLOCKS]
    return in_specs, out_specs


def _state_shapes(nb):
    return [jax.ShapeDtypeStruct((DEPTH, nb) + blk[2:], F32) for blk in STATE_BLOCKS]


def _mix_scratch(rows, nseq):
    return [
        pltpu.VMEM((nseq, CONV_BASE + rows, SSD_CONV_DIM), F32),
        pltpu.VMEM((nseq, CONV_BASE + rows, BRANCH), F32),
        pltpu.VMEM((nseq, CONV_BASE + rows, BRANCH), F32),
        pltpu.VMEM((nseq, rows, BRANCH), F32),
    ]


def _mixer(proj, states, state_layer, params, layer, carried, rows, nseq, mix_dtype):
    nb = proj.shape[0] // rows
    carried = list(carried) if carried is not None else []
    st_in, st_out = _state_specs(state_layer, layer, carried, nseq)
    in_specs = [pl.BlockSpec((nseq * rows, PROJ_W), lambda b, c: (b, 0))] + st_in
    in_specs += [pl.BlockSpec((None,) + p.shape[1:], lambda b, c: (layer, 0, 0)) for p in params]
    in_specs += [pl.BlockSpec(memory_space=pl.ANY) for _ in carried]
    first_carried = 1 + N_STATES + N_PARAMS
    outs = pl.pallas_call(
        functools.partial(_mixer_kernel, rows, nseq, len(carried)),
        grid=(nb // nseq, 1),
        in_specs=in_specs,
        out_specs=[pl.BlockSpec((nseq * rows, D_MODEL), lambda b, c: (b, 0))] + st_out,
        out_shape=[jax.ShapeDtypeStruct((nb * rows, D_MODEL), mix_dtype)] + _state_shapes(nb),
        input_output_aliases={first_carried + i: 1 + i for i in range(len(carried))},
        scratch_shapes=_mix_scratch(rows, nseq),
        compiler_params=pltpu.CompilerParams(
            dimension_semantics=("arbitrary", "arbitrary"), vmem_limit_bytes=VMEM_LIMIT),
        name=f"mixer_rows{rows}",
    )(proj, *states, *params, *carried)
    return outs[0], list(outs[1:])


def _layer_kernel(final, n_carried, *refs):
    lead, state_in, params, (y_ref,), state_out, scratch = _split_refs(refs, 6, n_carried, 1)
    x_ref, xnext_ref, nw_ref, win_ref, wout_ref, fnw_ref = lead
    proj_ref, mix_ref = scratch[:2]
    mix_scratch = scratch[2:]
    step = pl.program_id(0) * pl.num_programs(1) + pl.program_id(1)

    @pl.when(pl.program_id(1) == 0)
    def _():
        _load_state(state_in, state_out, *mix_scratch[:3])

    @pl.when(step == 0)
    def _():
        xn0 = _rms_norm(x_ref[0], nw_ref[...]).astype(BF16)
        for start in range(0, PROJ_W, PROJ_TILE):
            stop = min(start + PROJ_TILE, PROJ_W)
            proj_ref[0, :, start:stop] = _dot(xn0, win_ref[:, start:stop])

    tiles = [(s, min(s + PROJ_TILE, PROJ_W)) for s in range(0, PROJ_W, PROJ_TILE)]
    n_pieces = 18

    def chunk_step(cur, nxt):
        x = x_ref[0]
        xn_next = _rms_norm(xnext_ref[0], nw_ref[...]).astype(BF16)

        def project_some(piece):
            lo = (piece * len(tiles)) // n_pieces
            hi = ((piece + 1) * len(tiles)) // n_pieces
            for start, stop in tiles[lo:hi]:
                proj_ref[nxt, :, start:stop] = _dot(xn_next, win_ref[:, start:stop])

        def pcols(start, width):
            return proj_ref[cur, :, start:start + width]

        def mix_store(start, val):
            mix_ref[:, start:start + val.shape[1]] = val.astype(BF16)

        def out_project(group):
            lo, hi = group * BRANCH, (group + 1) * BRANCH
            return _dot(mix_ref[:, lo:hi], wout_ref[lo:hi, :])

        project_some(0)
        groups = _seq_groups(CHUNK, 0, pcols, mix_store, state_out, mix_scratch, params, PROMPT_HEADS_ABREAST)
        piece = 1
        y = x
        for gi, group in enumerate(groups):
            for mark in group:
                if mark is MAJOR:
                    project_some(piece)
                    piece += 1
            y = y + out_project(gi)
        assert piece == n_pieces, piece
        if final:
            y = _rms_norm(y, fnw_ref[...])
        y_ref[0] = y

    for parity in range(2):
        pl.when(step % 2 == parity)(functools.partial(chunk_step, parity, 1 - parity))


def _layer_prompt(x, norm_w, w_in, w_out, fnw, states, params, layer, carried, final):
    nb, t, _ = x.shape
    nc = t // CHUNK
    carried = list(carried) if carried is not None else []
    st_in, st_out = _state_specs(0, layer, carried, 1)
    resident = dict(pipeline_mode=pl.Buffered(1))

    def next_chunk(b, c):
        step = jnp.minimum(b * nc + c + 1, nb * nc - 1)
        return (step // nc, step % nc, 0)

    in_specs = [
        pl.BlockSpec((1, CHUNK, D_MODEL), lambda b, c: (b, c, 0)),
        pl.BlockSpec((1, CHUNK, D_MODEL), next_chunk),
        pl.BlockSpec((None, 1, D_MODEL), lambda b, c: (layer, 0, 0)),
        pl.BlockSpec((None, D_MODEL, PROJ_W), lambda b, c: (layer, 0, 0), **resident),
        pl.BlockSpec((None, D_MODEL, D_MODEL), lambda b, c: (layer, 0, 0), **resident),
        pl.BlockSpec((1, D_MODEL), lambda b, c: (0, 0)),
    ] + st_in
    in_specs += [pl.BlockSpec((None,) + p.shape[1:], lambda b, c: (layer, 0, 0)) for p in params]
    in_specs += [pl.BlockSpec(memory_space=pl.ANY) for _ in carried]
    first_carried = 6 + N_STATES + N_PARAMS
    outs = pl.pallas_call(
        functools.partial(_layer_kernel, final, len(carried)),
        grid=(nb, nc),
        in_specs=in_specs,
        out_specs=[pl.BlockSpec((1, CHUNK, D_MODEL), lambda b, c: (b, c, 0))] + st_out,
        out_shape=[jax.ShapeDtypeStruct((nb, t, D_MODEL), F32)] + _state_shapes(nb),
        input_output_aliases={first_carried + i: 1 + i for i in range(len(carried))},
        scratch_shapes=[pltpu.VMEM((2, CHUNK, PROJ_W), F32), pltpu.VMEM((CHUNK, D_MODEL), BF16)]
        + _mix_scratch(CHUNK, 1),
        compiler_params=pltpu.CompilerParams(
            dimension_semantics=("arbitrary", "arbitrary"), vmem_limit_bytes=VMEM_LIMIT),
        name="layer_prompt",
    )(x, x, norm_w, w_in, w_out, fnw, *states, *params, *carried)
    return outs[0], list(outs[1:])


O_I, O_SSD_Z, O_XBC, O_DT, O_SC, O_RG, O_END = 2560, 2568, 3080, 4104, 4112, 6160, 7184
REORDER_TILE = 512
GATE_TILE = C_SMALL // REORDER_TILE


def _reorder_kernel(src_ref, if_ref, dt_ref, o_ref):
    j = pl.program_id(1)

    @pl.when(j < GATE_TILE)
    def _():
        o_ref[...] = src_ref[0].T.astype(BF16)

    @pl.when(j == GATE_TILE)
    def _():
        zeros = jnp.zeros((REORDER_TILE - 24, D_MODEL), F32)
        small = jnp.concatenate([if_ref[0], dt_ref[0], dt_ref[0], zeros], axis=0)
        o_ref[...] = small.T.astype(BF16)


def _reorder_w_in(w_in):
    nl, d, n = w_in.shape
    assert n == O_END and d == D_MODEL
    assert C_SSD_Z % REORDER_TILE == 0 and C_SC_B % REORDER_TILE == 0 and C_SMALL % REORDER_TILE == 0
    w_t = jnp.swapaxes(w_in, 1, 2)

    def src_row(l, j):
        t_ssd, t_sc = C_SSD_Z // REORDER_TILE, C_SC_B // REORDER_TILE
        row = jnp.where(j < t_ssd, j * REORDER_TILE,
                        jnp.where(j < t_sc, O_SSD_Z + (j - t_ssd) * REORDER_TILE,
                                  O_SC + (j - t_sc) * REORDER_TILE))
        row = jnp.where(j < GATE_TILE, row, 0)
        return (l, pl.multiple_of(row, 8), 0)

    return pl.pallas_call(
        _reorder_kernel,
        grid=(nl, GATE_TILE + 1),
        in_specs=[
            pl.BlockSpec((pl.Element(1), pl.Element(REORDER_TILE), pl.Element(d)), src_row),
            pl.BlockSpec((pl.Element(1), pl.Element(8), pl.Element(d)), lambda l, j: (l, O_I, 0)),
            pl.BlockSpec((pl.Element(1), pl.Element(8), pl.Element(d)), lambda l, j: (l, O_DT, 0)),
        ],
        out_specs=pl.BlockSpec((None, d, REORDER_TILE), lambda l, j: (l, 0, j)),
        out_shape=jax.ShapeDtypeStruct((nl, d, PROJ_W), BF16),
        compiler_params=pltpu.CompilerParams(
            dimension_semantics=("arbitrary", "arbitrary"), vmem_limit_bytes=VMEM_LIMIT),
        name="reorder_w_in",
    )(w_t, w_t, w_t)


def _block_diag(w):
    nl, nblk, d, e = w.shape
    eye = jnp.eye(nblk, dtype=w.dtype)
    return jnp.einsum('lnde,nm->lndme', w, eye).reshape(nl, nblk * d, nblk * e)


def _lane_row(parts, width=LANES):
    nl = parts[0][1].shape[0]
    row = jnp.zeros((nl, width), F32)
    for off, p in parts:
        row = lax.dynamic_update_slice(row, p.astype(F32), (0, off))
    return row[:, None, :]


def kernel(x_prompt, x_sample, state_mlstm_C, state_mlstm_n, state_mlstm_m, state_ssd, state_ssd_conv, state_sconv_conv, state_rglru_h, state_rglru_conv, norm_w, w_in, ml_i_bias, ml_f_bias, ml_norm_w, ssd_conv_w, ssd_conv_b, ssd_dt_bias, ssd_A_log, ssd_D, ssd_norm_w, sc_conv_w, rg_conv_w, rg_conv_b, rg_wa, rg_ba, rg_wx, rg_bx, rg_lambda, w_out, final_norm_w):
    bp, tp, _ = x_prompt.shape
    bs, ts, _ = x_sample.shape

    w_in_r = _reorder_w_in(w_in)
    w_out_b = w_out.astype(BF16)
    rg_w = jnp.concatenate([_block_diag(rg_wa), _block_diag(rg_wx)], axis=-1).astype(BF16)
    bias_row = _lane_row([(0, ml_i_bias), (4, ml_f_bias), (8, ssd_dt_bias), (16, ssd_dt_bias)])
    alog_row = _lane_row([(8, ssd_A_log)])
    d_row = jnp.repeat(ssd_D, SSD_P, axis=-1)[:, None, :]
    row = lambda p: p[:, None, :]
    params = [bias_row, alog_row, row(ml_norm_w), ssd_conv_w, row(ssd_conv_b), d_row, row(ssd_norm_w),
              sc_conv_w, rg_conv_w, row(rg_conv_b), rg_w, row(rg_ba), row(rg_bx), row(rg_lambda)]
    norm_rows = row(norm_w)

    def pad_m(m):
        return jnp.pad(m, ((0, 0), (0, 0), (4, LANES - 4 - ML_HEADS)))[:, :, None, :]

    sample_states = [state_mlstm_C, state_mlstm_n, pad_m(state_mlstm_m), state_ssd, state_ssd_conv,
                     state_sconv_conv, state_rglru_h[:, :, None, :], state_rglru_conv]
    prompt_states = [
        jnp.zeros((1, bp, ML_HEADS, ML_D, ML_D), F32),
        jnp.zeros((1, bp, ML_HEADS, ML_D), F32),
        pad_m(jnp.full((1, bp, ML_HEADS), NEG_BIG, F32)),
        jnp.zeros((1, bp, SSD_HEADS, SSD_P, SSD_N), F32),
        jnp.zeros((1, bp, SSD_CONV - 1, SSD_CONV_DIM), F32),
        jnp.zeros((1, bp, SC_WIDTH - 1, BRANCH), F32),
        jnp.zeros((1, bp, 1, BRANCH), F32),
        jnp.zeros((1, bp, RG_CONV - 1, BRANCH), F32),
    ]

    xp = x_prompt
    xs = x_sample.reshape(bs * ts, D_MODEL)
    fnw = final_norm_w[None, :]
    st_p, st_s = None, None
    for l in range(DEPTH):
        final = l == DEPTH - 1
        xp, st_p = _layer_prompt(xp, norm_rows, w_in_r, w_out_b, fnw, prompt_states, params, l, st_p, final)
        proj_s = _inproj(xs, norm_rows, w_in_r, l, 512)
        mix_s, st_s = _mixer(proj_s, sample_states, l, params, l, st_s, ts, SAMPLE_SEQS, F32)
        xs = _outproj(mix_s, xs, w_out_b, fnw, l, final, 512)

    def unpack(states):
        states = list(states)
        states[2] = states[2][:, :, 0, 4:4 + ML_HEADS]
        states[6] = states[6][:, :, 0, :]
        return tuple(states)

    return (xp, xs.reshape(bs, ts, D_MODEL)) + unpack(st_p) + unpack(st_s)
```

```python
import functools

import jax
import jax.numpy as jnp
from jax import lax
from jax.experimental import pallas as pl
from jax.experimental.pallas import tpu as pltpu

F32 = jnp.float32
BF16 = jnp.bfloat16

D_MODEL = 2048
DEPTH = 2
BRANCH = 512
CHUNK = 128
EPS = 1e-6
ML_HEADS = 4
ML_D = 128
NEG_BIG = -1e30
SSD_HEADS = 8
SSD_P = 64
SSD_GROUPS = 2
SSD_N = 128
SSD_CONV = 4
SSD_CONV_DIM = 1024
SC_WIDTH = 3
RG_CONV = 4
RG_C = 8.0
RG_BLOCKS = 8

C_Q, C_K, C_V, C_O, C_Z = 0, 512, 1024, 1536, 2048
C_SSD_Z, C_XBC = 2560, 3072
C_SC_B, C_SC_C, C_SC_H, C_SC_Z = 4096, 4608, 5120, 5632
C_RG_X, C_RG_Z = 6144, 6656
C_SMALL = 7168
PROJ_W = 7296
PROJ_TILE = 256
MAJOR = "major"
LANES = 128
CONV_BASE = 8
SAMPLE_SEQS = 8

V7X_VMEM_BYTES = 64 * 1024 * 1024
VMEM_LIMIT = V7X_VMEM_BYTES - 4 * 1024 * 1024

NT_DIMS = (((1,), (1,)), ((), ()))
TN_DIMS = (((0,), (0,)), ((), ()))

N_STATES = 8
N_PARAMS = 14
STATE_BLOCKS = [
    (1, 1, ML_HEADS, ML_D, ML_D), (1, 1, ML_HEADS, ML_D), (1, 1, 1, LANES),
    (1, 1, SSD_HEADS, SSD_P, SSD_N), (1, 1, SSD_CONV - 1, SSD_CONV_DIM),
    (1, 1, SC_WIDTH - 1, BRANCH), (1, 1, 1, BRANCH), (1, 1, RG_CONV - 1, BRANCH),
]


def _softplus(x):
    return jnp.maximum(x, 0.0) + jnp.log1p(jnp.exp(-jnp.abs(x)))


def _silu(x):
    return x * jax.nn.sigmoid(x)


def _dot(a, b):
    return jnp.dot(a, b, preferred_element_type=F32)


def _dot_nt(a, b):
    return lax.dot_general(a, b, NT_DIMS, preferred_element_type=F32)


def _dot_tn(a, b):
    return lax.dot_general(a, b, TN_DIMS, preferred_element_type=F32)


def _rms_norm(x, w):
    return x * lax.rsqrt(jnp.mean(x * x, axis=-1, keepdims=True) + EPS) * w


def _inproj_kernel(x_ref, nw_ref, w_ref, o_ref, xn_ref):
    @pl.when(pl.program_id(1) == 0)
    def _():
        xn_ref[...] = _rms_norm(x_ref[...], nw_ref[...]).astype(BF16)

    o_ref[...] = _dot(xn_ref[...], w_ref[...])


def _inproj(x, norm_w, w, layer, tm):
    m = x.shape[0]
    tn = PROJ_W // 3
    return pl.pallas_call(
        _inproj_kernel,
        grid=(m // tm, PROJ_W // tn),
        in_specs=[
            pl.BlockSpec((tm, D_MODEL), lambda i, j: (i, 0)),
            pl.BlockSpec((None, 1, D_MODEL), lambda i, j: (layer, 0, 0)),
            pl.BlockSpec((None, D_MODEL, tn), lambda i, j: (layer, 0, j)),
        ],
        out_specs=pl.BlockSpec((tm, tn), lambda i, j: (i, j)),
        out_shape=jax.ShapeDtypeStruct((m, PROJ_W), F32),
        scratch_shapes=[pltpu.VMEM((tm, D_MODEL), BF16)],
        compiler_params=pltpu.CompilerParams(
            dimension_semantics=("arbitrary", "arbitrary"), vmem_limit_bytes=VMEM_LIMIT),
        name="inproj",
    )(x, norm_w, w)


def _outproj_kernel(final, mix_ref, x_ref, w_ref, fnw_ref, o_ref):
    y = x_ref[...] + _dot(mix_ref[...].astype(BF16), w_ref[...])
    if final:
        y = _rms_norm(y, fnw_ref[...])
    o_ref[...] = y


def _outproj(mix, x, w, fnw, layer, final, tm):
    m = x.shape[0]
    return pl.pallas_call(
        functools.partial(_outproj_kernel, final),
        grid=(m // tm,),
        in_specs=[
            pl.BlockSpec((tm, D_MODEL), lambda i: (i, 0)),
            pl.BlockSpec((tm, D_MODEL), lambda i: (i, 0)),
            pl.BlockSpec((None, D_MODEL, D_MODEL), lambda i: (layer, 0, 0)),
            pl.BlockSpec((1, D_MODEL), lambda i: (0, 0)),
        ],
        out_specs=pl.BlockSpec((tm, D_MODEL), lambda i: (i, 0)),
        out_shape=jax.ShapeDtypeStruct((m, D_MODEL), F32),
        compiler_params=pltpu.CompilerParams(
            dimension_semantics=("arbitrary",), vmem_limit_bytes=VMEM_LIMIT),
        name="outproj",
    )(mix, x, w, fnw)


def _causal_conv(ext_ref, u, w_ref, width, rows):
    lo = CONV_BASE - (width - 1)
    if rows < CONV_BASE:
        ext_ref[CONV_BASE:CONV_BASE + rows, :] = u
        acc = ext_ref[lo:lo + rows, :] * w_ref[0:1, :]
        for j in range(1, width):
            acc = acc + ext_ref[lo + j:lo + j + rows, :] * w_ref[j:j + 1, :]
        tail = ext_ref[lo + rows:CONV_BASE + rows, :]
        ext_ref[lo:CONV_BASE, :] = tail
        return acc, tail
    prev = ext_ref[0:CONV_BASE, :]
    row = lax.broadcasted_iota(jnp.int32, prev.shape, 0)
    acc = None
    for j in range(width):
        shift = width - 1 - j
        if shift:
            moved = pltpu.roll(u, shift, 0)
            first = jnp.where(row < shift, pltpu.roll(prev, shift, 0), moved[0:CONV_BASE, :])
            moved = jnp.concatenate([first, moved[CONV_BASE:, :]], axis=0)
        else:
            moved = u
        term = moved * w_ref[j:j + 1, :]
        acc = term if acc is None else acc + term
    ext_ref[0:CONV_BASE, :] = u[rows - CONV_BASE:rows, :]
    return acc, ext_ref[lo:CONV_BASE, :]


def _linear_scan(a, b, h0, rows):
    if rows < 8:
        h = h0
        out = []
        for t in range(rows):
            h = a[t:t + 1, :] * h + b[t:t + 1, :]
            out.append(h)
        return jnp.concatenate(out, axis=0)
    row = lax.broadcasted_iota(jnp.int32, a.shape, 0)
    d = 1
    while d < rows:
        keep = row >= d
        a_prev = jnp.where(keep, pltpu.roll(a, d, 0), 1.0)
        b_prev = jnp.where(keep, pltpu.roll(b, d, 0), 0.0)
        b = a * b_prev + b
        a = a * a_prev
        d *= 2
    return b + a * h0


def _cumsum_rows(z, rows):
    if rows < 8:
        out = [z[0:1, :]]
        for t in range(1, rows):
            out.append(out[-1] + z[t:t + 1, :])
        return jnp.concatenate(out, axis=0)
    row = lax.broadcasted_iota(jnp.int32, z.shape, 0)
    d = 1
    while d < rows:
        z = z + jnp.where(row >= d, pltpu.roll(z, d, 0), 0.0)
        d *= 2
    return z


def _load_state(state_in, state_out, ssd_ext, sc_ext, rg_ext):
    c_in, n_in, m_in, s_in, ssdcv_in, sccv_in, rgh_in, rgcv_in = state_in
    c_out, n_out, m_out, s_out, ssdcv_out, sccv_out, rgh_out, rgcv_out = state_out
    c_out[0:1] = c_in[...]
    n_out[0:1] = n_in[...]
    m_out[0:1] = m_in[...]
    s_out[0:1] = s_in[...]
    rgh_out[0:1] = rgh_in[...]
    for g in range(ssd_ext.shape[0]):
        for ext in (ssd_ext, sc_ext, rg_ext):
            ext[g, 0:CONV_BASE, :] = jnp.zeros((CONV_BASE, ext.shape[2]), F32)
        ssd_ext[g, CONV_BASE - (SSD_CONV - 1):CONV_BASE, :] = ssdcv_in[0, g]
        sc_ext[g, CONV_BASE - (SC_WIDTH - 1):CONV_BASE, :] = sccv_in[0, g]
        rg_ext[g, CONV_BASE - (RG_CONV - 1):CONV_BASE, :] = rgcv_in[0, g]
    for out in state_out:
        if out.shape[0] > 1:
            out[1:] = jnp.zeros((out.shape[0] - 1,) + out.shape[1:], out.dtype)


def _gate_prep(L, pcols, bias_ref, alog_ref):
    small = pcols(C_SMALL, LANES) + bias_ref[...]
    lane = lax.broadcasted_iota(jnp.int32, (L, LANES), 1)
    sp = _softplus(small)
    logsig = -_softplus(-small)
    a_coef = -jnp.exp(alog_ref[...])
    z = jnp.where((lane >= 4) & (lane < 8), logsig,
                  jnp.where((lane >= 8) & (lane < 16), sp * a_coef, 0.0))
    rr = lax.broadcasted_iota(jnp.int32, (L, L), 0)
    cc = lax.broadcasted_iota(jnp.int32, (L, L), 1)
    causal = cc <= rr
    cum = _cumsum_rows(z, L)
    packed = jnp.where(lane < 4, small,
                       jnp.where(lane < 16, cum, jnp.where(lane < 24, sp, 0.0)))
    packed_t = packed.T
    return small, sp, cum, packed_t, causal


def _group_mlstm(L, pcols, mix_store, gates, c_out, n_out, m_out, mlnw_ref):
    small, _, cum, packed_t, causal = gates
    m_row = m_out[...]
    m_news = [None] * ML_HEADS

    def head(h):
        q = pcols(C_Q + h * ML_D, ML_D)
        k = pcols(C_K + h * ML_D, ML_D) * (ML_D ** -0.5)
        v = pcols(C_V + h * ML_D, ML_D)
        qb, kb, vb = q.astype(BF16), k.astype(BF16), v.astype(BF16)
        yield
        c_h = c_out[h]
        if L < 8:
            qk = _dot_nt(qb, kb)
            qc = _dot(qb, c_h.astype(BF16))
        yield
        b_col = cum[:, 4 + h:5 + h]
        i_col = small[:, h:h + 1]
        b_row = packed_t[4 + h:5 + h, :]
        i_row = packed_t[h:h + 1, :]
        m_h = m_row[:, 4 + h:5 + h]
        dmat = jnp.where(causal, b_col - b_row + i_row, -jnp.inf)
        inter = b_col + m_h
        yield
        m_t = jnp.maximum(inter, jnp.max(dmat, axis=1, keepdims=True))
        yield
        w_intra = jnp.exp(dmat - m_t)
        w_inter = jnp.exp(inter - m_t)
        yield
        if L >= 8:
            qk = _dot_nt(qb, kb)
        s = qk * w_intra
        n_h = n_out[h:h + 1, :]
        yield
        if L >= 8:
            qc = _dot(qb, c_h.astype(BF16))
        num = _dot(s.astype(BF16), vb) + w_inter * qc
        yield
        den = jnp.sum(s, axis=1, keepdims=True) + w_inter * jnp.sum(q * n_h, axis=1, keepdims=True)
        yield
        hh = num / jnp.maximum(jnp.abs(den), jnp.exp(-m_t))
        yield
        b_last = cum[L - 1:L, 4 + h:5 + h]
        g = b_last - b_col + i_col
        inter_end = b_last + m_h
        m_new = jnp.maximum(inter_end, jnp.max(g, axis=0, keepdims=True))
        yield
        wg = jnp.exp(g - m_new)
        we = jnp.exp(inter_end - m_new)
        kw = k * wg
        yield
        c_out[h] = we * c_h + _dot_tn(kw.astype(BF16), vb)
        yield
        n_out[h:h + 1, :] = we * n_h + jnp.sum(kw, axis=0, keepdims=True)
        m_news[h] = m_new
        yield
        mu = jnp.mean(hh, axis=-1, keepdims=True)
        yield
        hc = hh - mu
        var = jnp.mean(hc * hc, axis=-1, keepdims=True)
        yield
        hn = hc * lax.rsqrt(var + EPS) * mlnw_ref[:, h * ML_D:(h + 1) * ML_D]
        o = pcols(C_O + h * ML_D, ML_D)
        zg = pcols(C_Z + h * ML_D, ML_D)
        mix_store(h * ML_D, jax.nn.sigmoid(o) * hn * _silu(zg))
        yield MAJOR

    for h in range(ML_HEADS):
        yield from head(h)
    lane1 = lax.broadcasted_iota(jnp.int32, (1, LANES), 1)
    for h in range(ML_HEADS):
        m_row = jnp.where(lane1 == 4 + h, m_news[h], m_row)
    m_out[...] = m_row


def _group_ssd(L, pcols, mix_store, gates, s_out, ssdcv_out, ssd_ext, yssd_ref,
               ssdcw_ref, ssdcb_ref, drow_ref, ssdnw_ref):
    _, sp, cum, packed_t, causal = gates
    conv, tail = _causal_conv(ssd_ext, pcols(C_XBC, SSD_CONV_DIM), ssdcw_ref, SSD_CONV, L)
    ssdcv_out[...] = tail
    yield
    xbc = _silu(conv + ssdcb_ref[...])
    xs_all = xbc[:, 0:BRANCH]
    yield
    cb, bm_b, cm_b = [], [], []
    for g in range(SSD_GROUPS):
        bm_g = xbc[:, BRANCH + g * SSD_N:BRANCH + (g + 1) * SSD_N].astype(BF16)
        cm_g = xbc[:, BRANCH + (SSD_GROUPS + g) * SSD_N:BRANCH + (SSD_GROUPS + g + 1) * SSD_N].astype(BF16)
        bm_b.append(bm_g)
        cm_b.append(cm_g)
        cb.append(_dot_nt(cm_g, bm_g))
        yield MAJOR if g == SSD_GROUPS - 1 else None
    rep = SSD_HEADS // SSD_GROUPS

    def head(h):
        g = h // rep
        xs_h = xs_all[:, h * SSD_P:(h + 1) * SSD_P]
        s_h = s_out[h]
        if L < 8:
            cs = _dot_nt(cm_b[g], s_h.astype(BF16))
        yield
        acs_col = cum[:, 8 + h:9 + h]
        acs_row = packed_t[8 + h:9 + h, :]
        dt_row = packed_t[16 + h:17 + h, :]
        dt_col = sp[:, 8 + h:9 + h]
        decay = jnp.exp(jnp.where(causal, acs_col - acs_row, -jnp.inf))
        yield
        scores = cb[g] * decay * dt_row
        yield
        if L >= 8:
            cs = _dot_nt(cm_b[g], s_h.astype(BF16))
        y = _dot(scores.astype(BF16), xs_h.astype(BF16)) + jnp.exp(acs_col) * cs
        yield
        a_last = cum[L - 1:L, 8 + h:9 + h]
        w = jnp.exp(a_last - acs_col) * dt_col
        yield
        s_out[h] = jnp.exp(a_last) * s_h + _dot_tn((xs_h * w).astype(BF16), bm_b[g])
        yield
        yssd_ref[:, h * SSD_P:(h + 1) * SSD_P] = y + drow_ref[:, h * SSD_P:(h + 1) * SSD_P] * xs_h
        yield MAJOR

    for h in range(SSD_HEADS):
        yield from head(h)
    yz = yssd_ref[...] * _silu(pcols(C_SSD_Z, BRANCH))
    yield
    gw = BRANCH // SSD_GROUPS
    for g in range(SSD_GROUPS):
        seg = yz[:, g * gw:(g + 1) * gw]
        seg = seg * lax.rsqrt(jnp.mean(seg * seg, axis=-1, keepdims=True) + EPS)
        mix_store(BRANCH + g * gw, seg * ssdnw_ref[:, g * gw:(g + 1) * gw])
        yield MAJOR if g == SSD_GROUPS - 1 else None


def _group_sconv(L, pcols, mix_store, sccv_out, sc_ext, sccw_ref):
    u = pcols(C_SC_C, BRANCH) * pcols(C_SC_H, BRANCH)
    yield
    cu, tail = _causal_conv(sc_ext, u, sccw_ref, SC_WIDTH, L)
    sccv_out[...] = tail
    yield
    mix_store(2 * BRANCH, pcols(C_SC_B, BRANCH) * cu * _silu(pcols(C_SC_Z, BRANCH)))
    yield MAJOR


def _group_rglru(L, pcols, mix_store, rgh_out, rgcv_out, rg_ext,
                 rgcw_ref, rgcb_ref, rgw_ref, rgba_ref, rgbx_ref, rglam_ref):
    conv, tail = _causal_conv(rg_ext, pcols(C_RG_X, BRANCH), rgcw_ref, RG_CONV, L)
    rgcv_out[...] = tail
    xr = conv + rgcb_ref[...]
    yield
    pre = _dot(xr.astype(BF16), rgw_ref[...])
    yield
    r = jax.nn.sigmoid(pre[:, 0:BRANCH] + rgba_ref[...])
    ig = jax.nn.sigmoid(pre[:, BRANCH:2 * BRANCH] + rgbx_ref[...])
    yield
    log_a = (-RG_C * r) * _softplus(-rglam_ref[...])
    a = jnp.exp(log_a)
    bterm = jnp.sqrt(-jnp.tanh(log_a) * (a * a + 1.0)) * (ig * xr)
    yield MAJOR
    h_rg = _linear_scan(a, bterm, rgh_out[...], L)
    rgh_out[...] = h_rg[L - 1:L, :]
    yield
    mix_store(3 * BRANCH, h_rg * _silu(pcols(C_RG_Z, BRANCH)))
    yield MAJOR


def _split_refs(refs, n_lead, n_carried, n_out_lead):
    lead = refs[:n_lead]
    state_in = refs[n_lead:n_lead + N_STATES]
    params = refs[n_lead + N_STATES:n_lead + N_STATES + N_PARAMS]
    rest = refs[n_lead + N_STATES + N_PARAMS + n_carried:]
    out_lead = rest[:n_out_lead]
    state_out = rest[n_out_lead:n_out_lead + N_STATES]
    scratch = rest[n_out_lead + N_STATES:]
    return lead, state_in, params, out_lead, state_out, scratch


def _seq_groups(L, seq, pcols, mix_store, state_out, scratch, params):
    (bias_ref, alog_ref, mlnw_ref, ssdcw_ref, ssdcb_ref, drow_ref, ssdnw_ref, sccw_ref, rgcw_ref,
     rgcb_ref, rgw_ref, rgba_ref, rgbx_ref, rglam_ref) = params
    c_out, n_out, m_out, s_out, ssdcv_out, sccv_out, rgh_out, rgcv_out = [r.at[0, seq] for r in state_out]
    ssd_ext, sc_ext, rg_ext, yssd_ref = [r.at[seq] for r in scratch]
    gates = _gate_prep(L, pcols, bias_ref, alog_ref)
    return [
        _group_mlstm(L, pcols, mix_store, gates, c_out, n_out, m_out, mlnw_ref),
        _group_ssd(L, pcols, mix_store, gates, s_out, ssdcv_out, ssd_ext, yssd_ref,
                   ssdcw_ref, ssdcb_ref, drow_ref, ssdnw_ref),
        _group_sconv(L, pcols, mix_store, sccv_out, sc_ext, sccw_ref),
        _group_rglru(L, pcols, mix_store, rgh_out, rgcv_out, rg_ext,
                     rgcw_ref, rgcb_ref, rgw_ref, rgba_ref, rgbx_ref, rglam_ref),
    ]


def _mixer_kernel(rows, nseq, n_carried, *refs):
    (proj_ref,), state_in, params, (mix_ref,), state_out, scratch = _split_refs(refs, 1, n_carried, 1)

    @pl.when(pl.program_id(1) == 0)
    def _():
        _load_state(state_in, state_out, *scratch[:3])

    def chain(seq):
        lo, hi = seq * rows, (seq + 1) * rows

        def pcols(start, width):
            return proj_ref[lo:hi, start:start + width]

        def mix_store(start, val):
            mix_ref[lo:hi, start:start + val.shape[1]] = val.astype(mix_ref.dtype)

        for group in _seq_groups(rows, seq, pcols, mix_store, state_out, scratch, params):
            yield from group

    chains = [chain(seq) for seq in range(nseq)]
    done = object()
    while chains:
        chains = [c for c in chains if next(c, done) is not done]


def _state_specs(state_layer, layer, carried, nseq):
    def at_layer(lyr, blk):
        blk = (blk[0], nseq) + blk[2:]
        tail = (0,) * (len(blk) - 2)
        return pl.BlockSpec(blk, lambda b, c: (lyr, b) + tail)

    in_specs = [at_layer(state_layer, blk) for blk in STATE_BLOCKS]
    if carried:
        out_specs = [at_layer(layer, blk) for blk in STATE_BLOCKS]
    else:
        assert layer == 0
        out_specs = [at_layer(0, (DEPTH,) + blk[1:]) for blk in STATE_BLOCKS]
    return in_specs, out_specs


def _state_shapes(nb):
    return [jax.ShapeDtypeStruct((DEPTH, nb) + blk[2:], F32) for blk in STATE_BLOCKS]


def _mix_scratch(rows, nseq):
    return [
        pltpu.VMEM((nseq, CONV_BASE + rows, SSD_CONV_DIM), F32),
        pltpu.VMEM((nseq, CONV_BASE + rows, BRANCH), F32),
        pltpu.VMEM((nseq, CONV_BASE + rows, BRANCH), F32),
        pltpu.VMEM((nseq, rows, BRANCH), F32),
    ]


def _mixer(proj, states, state_layer, params, layer, carried, rows, nseq, mix_dtype):
    nb = proj.shape[0] // rows
    carried = list(carried) if carried is not None else []
    st_in, st_out = _state_specs(state_layer, layer, carried, nseq)
    in_specs = [pl.BlockSpec((nseq * rows, PROJ_W), lambda b, c: (b, 0))] + st_in
    in_specs += [pl.BlockSpec((None,) + p.shape[1:], lambda b, c: (layer, 0, 0)) for p in params]
    in_specs += [pl.BlockSpec(memory_space=pl.ANY) for _ in carried]
    first_carried = 1 + N_STATES + N_PARAMS
    outs = pl.pallas_call(
        functools.partial(_mixer_kernel, rows, nseq, len(carried)),
        grid=(nb // nseq, 1),
        in_specs=in_specs,
        out_specs=[pl.BlockSpec((nseq * rows, D_MODEL), lambda b, c: (b, 0))] + st_out,
        out_shape=[jax.ShapeDtypeStruct((nb * rows, D_MODEL), mix_dtype)] + _state_shapes(nb),
        input_output_aliases={first_carried + i: 1 + i for i in range(len(carried))},
        scratch_shapes=_mix_scratch(rows, nseq),
        compiler_params=pltpu.CompilerParams(
            dimension_semantics=("arbitrary", "arbitrary"), vmem_limit_bytes=VMEM_LIMIT),
        name=f"mixer_rows{rows}",
    )(proj, *states, *params, *carried)
    return outs[0], list(outs[1:])


def _layer_kernel(final, n_carried, *refs):
    lead, state_in, params, (y_ref,), state_out, scratch = _split_refs(refs, 6, n_carried, 1)
    x_ref, xnext_ref, nw_ref, win_ref, wout_ref, fnw_ref = lead
    proj_ref, mix_ref = scratch[:2]
    mix_scratch = scratch[2:]
    step = pl.program_id(0) * pl.num_programs(1) + pl.program_id(1)

    @pl.when(pl.program_id(1) == 0)
    def _():
        _load_state(state_in, state_out, *mix_scratch[:3])

    @pl.when(step == 0)
    def _():
        xn0 = _rms_norm(x_ref[0], nw_ref[...]).astype(BF16)
        for start in range(0, PROJ_W, PROJ_TILE):
            stop = min(start + PROJ_TILE, PROJ_W)
            proj_ref[0, :, start:stop] = _dot(xn0, win_ref[:, start:stop])

    tiles = [(s, min(s + PROJ_TILE, PROJ_W)) for s in range(0, PROJ_W, PROJ_TILE)]
    n_pieces = 18

    def chunk_step(cur, nxt):
        x = x_ref[0]
        xn_next = _rms_norm(xnext_ref[0], nw_ref[...]).astype(BF16)

        def project_some(piece):
            lo = (piece * len(tiles)) // n_pieces
            hi = ((piece + 1) * len(tiles)) // n_pieces
            for start, stop in tiles[lo:hi]:
                proj_ref[nxt, :, start:stop] = _dot(xn_next, win_ref[:, start:stop])

        def pcols(start, width):
            return proj_ref[cur, :, start:start + width]

        def mix_store(start, val):
            mix_ref[:, start:start + val.shape[1]] = val.astype(BF16)

        def out_project(group):
            lo, hi = group * BRANCH, (group + 1) * BRANCH
            return _dot(mix_ref[:, lo:hi], wout_ref[lo:hi, :])

        project_some(0)
        groups = _seq_groups(CHUNK, 0, pcols, mix_store, state_out, mix_scratch, params)
        piece = 1
        y = x
        for gi, group in enumerate(groups):
            for mark in group:
                if mark is MAJOR:
                    project_some(piece)
                    piece += 1
            y = y + out_project(gi)
        assert piece == n_pieces, piece
        if final:
            y = _rms_norm(y, fnw_ref[...])
        y_ref[0] = y

    for parity in range(2):
        pl.when(step % 2 == parity)(functools.partial(chunk_step, parity, 1 - parity))


def _layer_prompt(x, norm_w, w_in, w_out, fnw, states, params, layer, carried, final):
    nb, t, _ = x.shape
    nc = t // CHUNK
    carried = list(carried) if carried is not None else []
    st_in, st_out = _state_specs(0, layer, carried, 1)
    resident = dict(pipeline_mode=pl.Buffered(1))

    def next_chunk(b, c):
        step = jnp.minimum(b * nc + c + 1, nb * nc - 1)
        return (step // nc, step % nc, 0)

    in_specs = [
        pl.BlockSpec((1, CHUNK, D_MODEL), lambda b, c: (b, c, 0)),
        pl.BlockSpec((1, CHUNK, D_MODEL), next_chunk),
        pl.BlockSpec((None, 1, D_MODEL), lambda b, c: (layer, 0, 0)),
        pl.BlockSpec((None, D_MODEL, PROJ_W), lambda b, c: (layer, 0, 0), **resident),
        pl.BlockSpec((None, D_MODEL, D_MODEL), lambda b, c: (layer, 0, 0), **resident),
        pl.BlockSpec((1, D_MODEL), lambda b, c: (0, 0)),
    ] + st_in
    in_specs += [pl.BlockSpec((None,) + p.shape[1:], lambda b, c: (layer, 0, 0)) for p in params]
    in_specs += [pl.BlockSpec(memory_space=pl.ANY) for _ in carried]
    first_carried = 6 + N_STATES + N_PARAMS
    outs = pl.pallas_call(
        functools.partial(_layer_kernel, final, len(carried)),
        grid=(nb, nc),
        in_specs=in_specs,
        out_specs=[pl.BlockSpec((1, CHUNK, D_MODEL), lambda b, c: (b, c, 0))] + st_out,
        out_shape=[jax.ShapeDtypeStruct((nb, t, D_MODEL), F32)] + _state_shapes(nb),
        input_output_aliases={first_carried + i: 1 + i for i in range(len(carried))},
        scratch_shapes=[pltpu.VMEM((2, CHUNK, PROJ_W), F32), pltpu.VMEM((CHUNK, D_MODEL), BF16)]
        + _mix_scratch(CHUNK, 1),
        compiler_params=pltpu.CompilerParams(
            dimension_semantics=("arbitrary", "arbitrary"), vmem_limit_bytes=VMEM_LIMIT),
        name="layer_prompt",
    )(x, x, norm_w, w_in, w_out, fnw, *states, *params, *carried)
    return outs[0], list(outs[1:])


O_I, O_SSD_Z, O_XBC, O_DT, O_SC, O_RG, O_END = 2560, 2568, 3080, 4104, 4112, 6160, 7184
REORDER_TILE = 512
GATE_TILE = C_SMALL // REORDER_TILE


def _reorder_kernel(src_ref, if_ref, dt_ref, o_ref):
    j = pl.program_id(1)

    @pl.when(j < GATE_TILE)
    def _():
        o_ref[...] = src_ref[0].T.astype(BF16)

    @pl.when(j == GATE_TILE)
    def _():
        zeros = jnp.zeros((REORDER_TILE - 24, D_MODEL), F32)
        small = jnp.concatenate([if_ref[0], dt_ref[0], dt_ref[0], zeros], axis=0)
        o_ref[...] = small.T.astype(BF16)


def _reorder_w_in(w_in):
    nl, d, n = w_in.shape
    assert n == O_END and d == D_MODEL
    assert C_SSD_Z % REORDER_TILE == 0 and C_SC_B % REORDER_TILE == 0 and C_SMALL % REORDER_TILE == 0
    w_t = jnp.swapaxes(w_in, 1, 2)

    def src_row(l, j):
        t_ssd, t_sc = C_SSD_Z // REORDER_TILE, C_SC_B // REORDER_TILE
        row = jnp.where(j < t_ssd, j * REORDER_TILE,
                        jnp.where(j < t_sc, O_SSD_Z + (j - t_ssd) * REORDER_TILE,
                                  O_SC + (j - t_sc) * REORDER_TILE))
        row = jnp.where(j < GATE_TILE, row, 0)
        return (l, pl.multiple_of(row, 8), 0)

    return pl.pallas_call(
        _reorder_kernel,
        grid=(nl, GATE_TILE + 1),
        in_specs=[
            pl.BlockSpec((pl.Element(1), pl.Element(REORDER_TILE), pl.Element(d)), src_row),
            pl.BlockSpec((pl.Element(1), pl.Element(8), pl.Element(d)), lambda l, j: (l, O_I, 0)),
            pl.BlockSpec((pl.Element(1), pl.Element(8), pl.Element(d)), lambda l, j: (l, O_DT, 0)),
        ],
        out_specs=pl.BlockSpec((None, d, REORDER_TILE), lambda l, j: (l, 0, j)),
        out_shape=jax.ShapeDtypeStruct((nl, d, PROJ_W), BF16),
        compiler_params=pltpu.CompilerParams(
            dimension_semantics=("arbitrary", "arbitrary"), vmem_limit_bytes=VMEM_LIMIT),
        name="reorder_w_in",
    )(w_t, w_t, w_t)


def _block_diag(w):
    nl, nblk, d, e = w.shape
    eye = jnp.eye(nblk, dtype=w.dtype)
    return jnp.einsum('lnde,nm->lndme', w, eye).reshape(nl, nblk * d, nblk * e)


def _lane_row(parts, width=LANES):
    nl = parts[0][1].shape[0]
    row = jnp.zeros((nl, width), F32)
    for off, p in parts:
        row = lax.dynamic_update_slice(row, p.astype(F32), (0, off))
    return row[:, None, :]


def kernel(x_prompt, x_sample, state_mlstm_C, state_mlstm_n, state_mlstm_m, state_ssd, state_ssd_conv, state_sconv_conv, state_rglru_h, state_rglru_conv, norm_w, w_in, ml_i_bias, ml_f_bias, ml_norm_w, ssd_conv_w, ssd_conv_b, ssd_dt_bias, ssd_A_log, ssd_D, ssd_norm_w, sc_conv_w, rg_conv_w, rg_conv_b, rg_wa, rg_ba, rg_wx, rg_bx, rg_lambda, w_out, final_norm_w):
    bp, tp, _ = x_prompt.shape
    bs, ts, _ = x_sample.shape

    w_in_r = _reorder_w_in(w_in)
    w_out_b = w_out.astype(BF16)
    rg_w = jnp.concatenate([_block_diag(rg_wa), _block_diag(rg_wx)], axis=-1).astype(BF16)
    bias_row = _lane_row([(0, ml_i_bias), (4, ml_f_bias), (8, ssd_dt_bias), (16, ssd_dt_bias)])
    alog_row = _lane_row([(8, ssd_A_log)])
    d_row = jnp.repeat(ssd_D, SSD_P, axis=-1)[:, None, :]
    row = lambda p: p[:, None, :]
    params = [bias_row, alog_row, row(ml_norm_w), ssd_conv_w, row(ssd_conv_b), d_row, row(ssd_norm_w),
              sc_conv_w, rg_conv_w, row(rg_conv_b), rg_w, row(rg_ba), row(rg_bx), row(rg_lambda)]
    norm_rows = row(norm_w)

    def pad_m(m):
        return jnp.pad(m, ((0, 0), (0, 0), (4, LANES - 4 - ML_HEADS)))[:, :, None, :]

    sample_states = [state_mlstm_C, state_mlstm_n, pad_m(state_mlstm_m), state_ssd, state_ssd_conv,
                     state_sconv_conv, state_rglru_h[:, :, None, :], state_rglru_conv]
    prompt_states = [
        jnp.zeros((1, bp, ML_HEADS, ML_D, ML_D), F32),
        jnp.zeros((1, bp, ML_HEADS, ML_D), F32),
        pad_m(jnp.full((1, bp, ML_HEADS), NEG_BIG, F32)),
        jnp.zeros((1, bp, SSD_HEADS, SSD_P, SSD_N), F32),
        jnp.zeros((1, bp, SSD_CONV - 1, SSD_CONV_DIM), F32),
        jnp.zeros((1, bp, SC_WIDTH - 1, BRANCH), F32),
        jnp.zeros((1, bp, 1, BRANCH), F32),
        jnp.zeros((1, bp, RG_CONV - 1, BRANCH), F32),
    ]

    xp = x_prompt
    xs = x_sample.reshape(bs * ts, D_MODEL)
    fnw = final_norm_w[None, :]
    st_p, st_s = None, None
    for l in range(DEPTH):
        final = l == DEPTH - 1
        xp, st_p = _layer_prompt(xp, norm_rows, w_in_r, w_out_b, fnw, prompt_states, params, l, st_p, final)
        proj_s = _inproj(xs, norm_rows, w_in_r, l, 512)
        mix_s, st_s = _mixer(proj_s, sample_states, l, params, l, st_s, ts, SAMPLE_SEQS, F32)
        xs = _outproj(mix_s, xs, w_out_b, fnw, l, final, 512)

    def unpack(states):
        states = list(states)
        states[2] = states[2][:, :, 0, 4:4 + ML_HEADS]
        states[6] = states[6][:, :, 0, :]
        return tuple(states)

    return (xp, xs.reshape(bs, ts, D_MODEL)) + unpack(st_p) + unpack(st_s)
```

```python
import functools

import jax
import jax.numpy as jnp
from jax import lax
from jax.experimental import pallas as pl
from jax.experimental.pallas import tpu as pltpu

F32 = jnp.float32
BF16 = jnp.bfloat16

D_MODEL = 2048
DEPTH = 2
BRANCH = 512
CHUNK = 128
EPS = 1e-6
ML_HEADS = 4
ML_D = 128
NEG_BIG = -1e30
SSD_HEADS = 8
SSD_P = 64
SSD_GROUPS = 2
SSD_N = 128
SSD_CONV = 4
SSD_CONV_DIM = 1024
SC_WIDTH = 3
RG_CONV = 4
RG_C = 8.0
RG_BLOCKS = 8

C_Q, C_K, C_V, C_O, C_Z = 0, 512, 1024, 1536, 2048
C_SSD_Z, C_XBC = 2560, 3072
C_SC_B, C_SC_C, C_SC_H, C_SC_Z = 4096, 4608, 5120, 5632
C_RG_X, C_RG_Z = 6144, 6656
C_SMALL = 7168
PROJ_W = 7296
PROJ_TILE = 256
MAJOR = "major"
LANES = 128
CONV_BASE = 8
SAMPLE_SEQS = 8

V7X_VMEM_BYTES = 64 * 1024 * 1024
VMEM_LIMIT = V7X_VMEM_BYTES - 4 * 1024 * 1024

NT_DIMS = (((1,), (1,)), ((), ()))
TN_DIMS = (((0,), (0,)), ((), ()))

N_STATES = 8
N_PARAMS = 14
STATE_BLOCKS = [
    (1, 1, ML_HEADS, ML_D, ML_D), (1, 1, ML_HEADS, ML_D), (1, 1, 1, LANES),
    (1, 1, SSD_HEADS, SSD_P, SSD_N), (1, 1, SSD_CONV - 1, SSD_CONV_DIM),
    (1, 1, SC_WIDTH - 1, BRANCH), (1, 1, 1, BRANCH), (1, 1, RG_CONV - 1, BRANCH),
]


def _softplus(x):
    return jnp.maximum(x, 0.0) + jnp.log1p(jnp.exp(-jnp.abs(x)))


def _silu(x):
    return x * jax.nn.sigmoid(x)


def _dot(a, b):
    return jnp.dot(a, b, preferred_element_type=F32)


def _dot_nt(a, b):
    return lax.dot_general(a, b, NT_DIMS, preferred_element_type=F32)


def _dot_tn(a, b):
    return lax.dot_general(a, b, TN_DIMS, preferred_element_type=F32)


def _rms_norm(x, w):
    return x * lax.rsqrt(jnp.mean(x * x, axis=-1, keepdims=True) + EPS) * w


def _inproj_kernel(x_ref, nw_ref, w_ref, o_ref, xn_ref):
    @pl.when(pl.program_id(1) == 0)
    def _():
        xn_ref[...] = _rms_norm(x_ref[...], nw_ref[...]).astype(BF16)

    o_ref[...] = _dot(xn_ref[...], w_ref[...])


def _inproj(x, norm_w, w, layer, tm):
    m = x.shape[0]
    tn = PROJ_W // 3
    return pl.pallas_call(
        _inproj_kernel,
        grid=(m // tm, PROJ_W // tn),
        in_specs=[
            pl.BlockSpec((tm, D_MODEL), lambda i, j: (i, 0)),
            pl.BlockSpec((None, 1, D_MODEL), lambda i, j: (layer, 0, 0)),
            pl.BlockSpec((None, D_MODEL, tn), lambda i, j: (layer, 0, j)),
        ],
        out_specs=pl.BlockSpec((tm, tn), lambda i, j: (i, j)),
        out_shape=jax.ShapeDtypeStruct((m, PROJ_W), F32),
        scratch_shapes=[pltpu.VMEM((tm, D_MODEL), BF16)],
        compiler_params=pltpu.CompilerParams(
            dimension_semantics=("arbitrary", "arbitrary"), vmem_limit_bytes=VMEM_LIMIT),
        name="inproj",
    )(x, norm_w, w)


def _outproj_kernel(final, mix_ref, x_ref, w_ref, fnw_ref, o_ref):
    y = x_ref[...] + _dot(mix_ref[...].astype(BF16), w_ref[...])
    if final:
        y = _rms_norm(y, fnw_ref[...])
    o_ref[...] = y


def _outproj(mix, x, w, fnw, layer, final, tm):
    m = x.shape[0]
    return pl.pallas_call(
        functools.partial(_outproj_kernel, final),
        grid=(m // tm,),
        in_specs=[
            pl.BlockSpec((tm, D_MODEL), lambda i: (i, 0)),
            pl.BlockSpec((tm, D_MODEL), lambda i: (i, 0)),
            pl.BlockSpec((None, D_MODEL, D_MODEL), lambda i: (layer, 0, 0)),
            pl.BlockSpec((1, D_MODEL), lambda i: (0, 0)),
        ],
        out_specs=pl.BlockSpec((tm, D_MODEL), lambda i: (i, 0)),
        out_shape=jax.ShapeDtypeStruct((m, D_MODEL), F32),
        compiler_params=pltpu.CompilerParams(
            dimension_semantics=("arbitrary",), vmem_limit_bytes=VMEM_LIMIT),
        name="outproj",
    )(mix, x, w, fnw)


def _causal_conv(ext_ref, u, w_ref, width, rows):
    lo = CONV_BASE - (width - 1)
    if rows < CONV_BASE:
        ext_ref[CONV_BASE:CONV_BASE + rows, :] = u
        acc = ext_ref[lo:lo + rows, :] * w_ref[0:1, :]
        for j in range(1, width):
            acc = acc + ext_ref[lo + j:lo + j + rows, :] * w_ref[j:j + 1, :]
        tail = ext_ref[lo + rows:CONV_BASE + rows, :]
        ext_ref[lo:CONV_BASE, :] = tail
        return acc, tail
    prev = ext_ref[0:CONV_BASE, :]
    row = lax.broadcasted_iota(jnp.int32, prev.shape, 0)
    acc = None
    for j in range(width):
        shift = width - 1 - j
        if shift:
            moved = pltpu.roll(u, shift, 0)
            first = jnp.where(row < shift, pltpu.roll(prev, shift, 0), moved[0:CONV_BASE, :])
            moved = jnp.concatenate([first, moved[CONV_BASE:, :]], axis=0)
        else:
            moved = u
        term = moved * w_ref[j:j + 1, :]
        acc = term if acc is None else acc + term
    ext_ref[0:CONV_BASE, :] = u[rows - CONV_BASE:rows, :]
    return acc, ext_ref[lo:CONV_BASE, :]


def _linear_scan(a, b, h0, rows):
    if rows < 8:
        h = h0
        out = []
        for t in range(rows):
            h = a[t:t + 1, :] * h + b[t:t + 1, :]
            out.append(h)
        return jnp.concatenate(out, axis=0)
    row = lax.broadcasted_iota(jnp.int32, a.shape, 0)
    d = 1
    while d < rows:
        keep = row >= d
        a_prev = jnp.where(keep, pltpu.roll(a, d, 0), 1.0)
        b_prev = jnp.where(keep, pltpu.roll(b, d, 0), 0.0)
        b = a * b_prev + b
        a = a * a_prev
        d *= 2
    return b + a * h0


def _cumsum_rows(z, rows):
    if rows < 8:
        out = [z[0:1, :]]
        for t in range(1, rows):
            out.append(out[-1] + z[t:t + 1, :])
        return jnp.concatenate(out, axis=0)
    row = lax.broadcasted_iota(jnp.int32, z.shape, 0)
    d = 1
    while d < rows:
        z = z + jnp.where(row >= d, pltpu.roll(z, d, 0), 0.0)
        d *= 2
    return z


def _load_state(state_in, state_out, ssd_ext, sc_ext, rg_ext):
    c_in, n_in, m_in, s_in, ssdcv_in, sccv_in, rgh_in, rgcv_in = state_in
    c_out, n_out, m_out, s_out, ssdcv_out, sccv_out, rgh_out, rgcv_out = state_out
    c_out[0:1] = c_in[...]
    n_out[0:1] = n_in[...]
    m_out[0:1] = m_in[...]
    s_out[0:1] = s_in[...]
    rgh_out[0:1] = rgh_in[...]
    for g in range(ssd_ext.shape[0]):
        for ext in (ssd_ext, sc_ext, rg_ext):
            ext[g, 0:CONV_BASE, :] = jnp.zeros((CONV_BASE, ext.shape[2]), F32)
        ssd_ext[g, CONV_BASE - (SSD_CONV - 1):CONV_BASE, :] = ssdcv_in[0, g]
        sc_ext[g, CONV_BASE - (SC_WIDTH - 1):CONV_BASE, :] = sccv_in[0, g]
        rg_ext[g, CONV_BASE - (RG_CONV - 1):CONV_BASE, :] = rgcv_in[0, g]
    for out in state_out:
        if out.shape[0] > 1:
            out[1:] = jnp.zeros((out.shape[0] - 1,) + out.shape[1:], out.dtype)


def _gate_prep(L, pcols, bias_ref, alog_ref):
    small = pcols(C_SMALL, LANES) + bias_ref[...]
    lane = lax.broadcasted_iota(jnp.int32, (L, LANES), 1)
    sp = _softplus(small)
    logsig = -_softplus(-small)
    a_coef = -jnp.exp(alog_ref[...])
    z = jnp.where((lane >= 4) & (lane < 8), logsig,
                  jnp.where((lane >= 8) & (lane < 16), sp * a_coef, 0.0))
    rr = lax.broadcasted_iota(jnp.int32, (L, L), 0)
    cc = lax.broadcasted_iota(jnp.int32, (L, L), 1)
    causal = cc <= rr
    cum = _cumsum_rows(z, L)
    packed = jnp.where(lane < 4, small,
                       jnp.where(lane < 16, cum, jnp.where(lane < 24, sp, 0.0)))
    packed_t = packed.T
    return small, sp, cum, packed_t, causal


def _group_mlstm(L, pcols, mix_store, gates, c_out, n_out, m_out, mlnw_ref):
    small, _, cum, packed_t, causal = gates
    m_row = m_out[...]
    m_news = [None] * ML_HEADS

    def head(h):
        q = pcols(C_Q + h * ML_D, ML_D)
        k = pcols(C_K + h * ML_D, ML_D) * (ML_D ** -0.5)
        v = pcols(C_V + h * ML_D, ML_D)
        qb, kb, vb = q.astype(BF16), k.astype(BF16), v.astype(BF16)
        yield
        c_h = c_out[h]
        if L < 8:
            qk = _dot_nt(qb, kb)
            qc = _dot(qb, c_h.astype(BF16))
        yield
        b_col = cum[:, 4 + h:5 + h]
        i_col = small[:, h:h + 1]
        b_row = packed_t[4 + h:5 + h, :]
        i_row = packed_t[h:h + 1, :]
        m_h = m_row[:, 4 + h:5 + h]
        dmat = jnp.where(causal, b_col - b_row + i_row, -jnp.inf)
        inter = b_col + m_h
        yield
        m_t = jnp.maximum(inter, jnp.max(dmat, axis=1, keepdims=True))
        yield
        w_intra = jnp.exp(dmat - m_t)
        w_inter = jnp.exp(inter - m_t)
        yield
        if L >= 8:
            qk = _dot_nt(qb, kb)
        s = qk * w_intra
        n_h = n_out[h:h + 1, :]
        yield
        if L >= 8:
            qc = _dot(qb, c_h.astype(BF16))
        num = _dot(s.astype(BF16), vb) + w_inter * qc
        yield
        den = jnp.sum(s, axis=1, keepdims=True) + w_inter * jnp.sum(q * n_h, axis=1, keepdims=True)
        yield
        hh = num / jnp.maximum(jnp.abs(den), jnp.exp(-m_t))
        yield
        b_last = cum[L - 1:L, 4 + h:5 + h]
        g = b_last - b_col + i_col
        inter_end = b_last + m_h
        m_new = jnp.maximum(inter_end, jnp.max(g, axis=0, keepdims=True))
        yield
        wg = jnp.exp(g - m_new)
        we = jnp.exp(inter_end - m_new)
        kw = k * wg
        yield
        c_out[h] = we * c_h + _dot_tn(kw.astype(BF16), vb)
        yield
        n_out[h:h + 1, :] = we * n_h + jnp.sum(kw, axis=0, keepdims=True)
        m_news[h] = m_new
        yield
        mu = jnp.mean(hh, axis=-1, keepdims=True)
        yield
        hc = hh - mu
        var = jnp.mean(hc * hc, axis=-1, keepdims=True)
        yield
        hn = hc * lax.rsqrt(var + EPS) * mlnw_ref[:, h * ML_D:(h + 1) * ML_D]
        o = pcols(C_O + h * ML_D, ML_D)
        zg = pcols(C_Z + h * ML_D, ML_D)
        mix_store(h * ML_D, jax.nn.sigmoid(o) * hn * _silu(zg))
        yield MAJOR

    for h in range(ML_HEADS):
        yield from head(h)
    lane1 = lax.broadcasted_iota(jnp.int32, (1, LANES), 1)
    for h in range(ML_HEADS):
        m_row = jnp.where(lane1 == 4 + h, m_news[h], m_row)
    m_out[...] = m_row


def _group_ssd(L, pcols, mix_store, gates, s_out, ssdcv_out, ssd_ext, yssd_ref, erep_ref, wrep_ref,
               ssdcw_ref, ssdcb_ref, drow_ref, ssdnw_ref):
    _, sp, cum, packed_t, causal = gates
    conv, tail = _causal_conv(ssd_ext, pcols(C_XBC, SSD_CONV_DIM), ssdcw_ref, SSD_CONV, L)
    ssdcv_out[...] = tail
    yield
    xbc = _silu(conv + ssdcb_ref[...])
    xs_all = xbc[:, 0:BRANCH]
    yield
    rep = SSD_HEADS // SSD_GROUPS
    gw = rep * SSD_P
    cb, bm_b, s_grp, cs = [], [], [], []
    for g in range(SSD_GROUPS):
        bm_g = xbc[:, BRANCH + g * SSD_N:BRANCH + (g + 1) * SSD_N].astype(BF16)
        cm_g = xbc[:, BRANCH + (SSD_GROUPS + g) * SSD_N:BRANCH + (SSD_GROUPS + g + 1) * SSD_N].astype(BF16)
        bm_b.append(bm_g)
        cb.append(_dot_nt(cm_g, bm_g))
        s_g = s_out[g * rep:(g + 1) * rep].reshape(gw, SSD_N)
        s_grp.append(s_g)
        cs.append(_dot_nt(cm_g, s_g.astype(BF16)))
        yield MAJOR if g == SSD_GROUPS - 1 else None

    def head(h):
        g = h // rep
        cols = slice(h * SSD_P, (h + 1) * SSD_P)
        xs_h = xs_all[:, cols]
        acs_col = cum[:, 8 + h:9 + h]
        acs_row = packed_t[8 + h:9 + h, :]
        dt_row = packed_t[16 + h:17 + h, :]
        dt_col = sp[:, 8 + h:9 + h]
        decay = jnp.exp(jnp.where(causal, acs_col - acs_row, -jnp.inf))
        yield
        scores = cb[g] * decay * dt_row
        yield
        yssd_ref[:, cols] = _dot(scores.astype(BF16), xs_h.astype(BF16))
        yield
        erep_ref[:, cols] = jnp.broadcast_to(jnp.exp(acs_col), (L, SSD_P))
        a_last = cum[L - 1:L, 8 + h:9 + h]
        wrep_ref[:, cols] = jnp.broadcast_to(jnp.exp(a_last - acs_col) * dt_col, (L, SSD_P))
        yield MAJOR

    for h in range(SSD_HEADS):
        yield from head(h)
    for g in range(SSD_GROUPS):
        cols = slice(g * gw, (g + 1) * gw)
        xs_g = xs_all[:, cols]
        y_g = yssd_ref[:, cols] + erep_ref[:, cols] * cs[g] + drow_ref[:, cols] * xs_g
        yield
        decay_end = jnp.concatenate(
            [jnp.broadcast_to(jnp.exp(cum[L - 1:L, 8 + h:9 + h]), (SSD_P, SSD_N))
             for h in range(g * rep, (g + 1) * rep)], axis=0)
        upd = _dot_tn((xs_g * wrep_ref[:, cols]).astype(BF16), bm_b[g])
        s_out[g * rep:(g + 1) * rep] = (decay_end * s_grp[g] + upd).reshape(rep, SSD_P, SSD_N)
        yield
        seg = y_g * _silu(pcols(C_SSD_Z + g * gw, gw))
        seg = seg * lax.rsqrt(jnp.mean(seg * seg, axis=-1, keepdims=True) + EPS)
        mix_store(BRANCH + g * gw, seg * ssdnw_ref[:, cols])
        yield MAJOR if g == SSD_GROUPS - 1 else None


def _group_sconv(L, pcols, mix_store, sccv_out, sc_ext, sccw_ref):
    u = pcols(C_SC_C, BRANCH) * pcols(C_SC_H, BRANCH)
    yield
    cu, tail = _causal_conv(sc_ext, u, sccw_ref, SC_WIDTH, L)
    sccv_out[...] = tail
    yield
    mix_store(2 * BRANCH, pcols(C_SC_B, BRANCH) * cu * _silu(pcols(C_SC_Z, BRANCH)))
    yield MAJOR


def _group_rglru(L, pcols, mix_store, rgh_out, rgcv_out, rg_ext,
                 rgcw_ref, rgcb_ref, rgw_ref, rgba_ref, rgbx_ref, rglam_ref):
    conv, tail = _causal_conv(rg_ext, pcols(C_RG_X, BRANCH), rgcw_ref, RG_CONV, L)
    rgcv_out[...] = tail
    xr = conv + rgcb_ref[...]
    yield
    pre = _dot(xr.astype(BF16), rgw_ref[...])
    yield
    r = jax.nn.sigmoid(pre[:, 0:BRANCH] + rgba_ref[...])
    ig = jax.nn.sigmoid(pre[:, BRANCH:2 * BRANCH] + rgbx_ref[...])
    yield
    log_a = (-RG_C * r) * _softplus(-rglam_ref[...])
    a = jnp.exp(log_a)
    bterm = jnp.sqrt(-jnp.tanh(log_a) * (a * a + 1.0)) * (ig * xr)
    yield MAJOR
    h_rg = _linear_scan(a, bterm, rgh_out[...], L)
    rgh_out[...] = h_rg[L - 1:L, :]
    yield
    mix_store(3 * BRANCH, h_rg * _silu(pcols(C_RG_Z, BRANCH)))
    yield MAJOR


def _split_refs(refs, n_lead, n_carried, n_out_lead):
    lead = refs[:n_lead]
    state_in = refs[n_lead:n_lead + N_STATES]
    params = refs[n_lead + N_STATES:n_lead + N_STATES + N_PARAMS]
    rest = refs[n_lead + N_STATES + N_PARAMS + n_carried:]
    out_lead = rest[:n_out_lead]
    state_out = rest[n_out_lead:n_out_lead + N_STATES]
    scratch = rest[n_out_lead + N_STATES:]
    return lead, state_in, params, out_lead, state_out, scratch


def _seq_groups(L, seq, pcols, mix_store, state_out, scratch, params):
    (bias_ref, alog_ref, mlnw_ref, ssdcw_ref, ssdcb_ref, drow_ref, ssdnw_ref, sccw_ref, rgcw_ref,
     rgcb_ref, rgw_ref, rgba_ref, rgbx_ref, rglam_ref) = params
    c_out, n_out, m_out, s_out, ssdcv_out, sccv_out, rgh_out, rgcv_out = [r.at[0, seq] for r in state_out]
    ssd_ext, sc_ext, rg_ext, yssd_ref, erep_ref, wrep_ref = [r.at[seq] for r in scratch]
    gates = _gate_prep(L, pcols, bias_ref, alog_ref)
    return [
        _group_mlstm(L, pcols, mix_store, gates, c_out, n_out, m_out, mlnw_ref),
        _group_ssd(L, pcols, mix_store, gates, s_out, ssdcv_out, ssd_ext, yssd_ref, erep_ref, wrep_ref,
                   ssdcw_ref, ssdcb_ref, drow_ref, ssdnw_ref),
        _group_sconv(L, pcols, mix_store, sccv_out, sc_ext, sccw_ref),
        _group_rglru(L, pcols, mix_store, rgh_out, rgcv_out, rg_ext,
                     rgcw_ref, rgcb_ref, rgw_ref, rgba_ref, rgbx_ref, rglam_ref),
    ]


def _mixer_kernel(rows, nseq, n_carried, *refs):
    (proj_ref,), state_in, params, (mix_ref,), state_out, scratch = _split_refs(refs, 1, n_carried, 1)

    @pl.when(pl.program_id(1) == 0)
    def _():
        _load_state(state_in, state_out, *scratch[:3])

    def chain(seq):
        lo, hi = seq * rows, (seq + 1) * rows

        def pcols(start, width):
            return proj_ref[lo:hi, start:start + width]

        def mix_store(start, val):
            mix_ref[lo:hi, start:start + val.shape[1]] = val.astype(mix_ref.dtype)

        for group in _seq_groups(rows, seq, pcols, mix_store, state_out, scratch, params):
            yield from group

    chains = [chain(seq) for seq in range(nseq)]
    done = object()
    while chains:
        chains = [c for c in chains if next(c, done) is not done]


def _state_specs(state_layer, layer, carried, nseq):
    def at_layer(lyr, blk):
        blk = (blk[0], nseq) + blk[2:]
        tail = (0,) * (len(blk) - 2)
        return pl.BlockSpec(blk, lambda b, c: (lyr, b) + tail)

    in_specs = [at_layer(state_layer, blk) for blk in STATE_BLOCKS]
    if carried:
        out_specs = [at_layer(layer, blk) for blk in STATE_BLOCKS]
    else:
        assert layer == 0
        out_specs = [at_layer(0, (DEPTH,) + blk[1:]) for blk in STATE_BLOCKS]
    return in_specs, out_specs


def _state_shapes(nb):
    return [jax.ShapeDtypeStruct((DEPTH, nb) + blk[2:], F32) for blk in STATE_BLOCKS]


def _mix_scratch(rows, nseq):
    return [
        pltpu.VMEM((nseq, CONV_BASE + rows, SSD_CONV_DIM), F32),
        pltpu.VMEM((nseq, CONV_BASE + rows, BRANCH), F32),
        pltpu.VMEM((nseq, CONV_BASE + rows, BRANCH), F32),
        pltpu.VMEM((nseq, rows, BRANCH), F32),
        pltpu.VMEM((nseq, rows, BRANCH), F32),
        pltpu.VMEM((nseq, rows, BRANCH), F32),
    ]


def _mixer(proj, states, state_layer, params, layer, carried, rows, nseq, mix_dtype):
    nb = proj.shape[0] // rows
    carried = list(carried) if carried is not None else []
    st_in, st_out = _state_specs(state_layer, layer, carried, nseq)
    in_specs = [pl.BlockSpec((nseq * rows, PROJ_W), lambda b, c: (b, 0))] + st_in
    in_specs += [pl.BlockSpec((None,) + p.shape[1:], lambda b, c: (layer, 0, 0)) for p in params]
    in_specs += [pl.BlockSpec(memory_space=pl.ANY) for _ in carried]
    first_carried = 1 + N_STATES + N_PARAMS
    outs = pl.pallas_call(
        functools.partial(_mixer_kernel, rows, nseq, len(carried)),
        grid=(nb // nseq, 1),
        in_specs=in_specs,
        out_specs=[pl.BlockSpec((nseq * rows, D_MODEL), lambda b, c: (b, 0))] + st_out,
        out_shape=[jax.ShapeDtypeStruct((nb * rows, D_MODEL), mix_dtype)] + _state_shapes(nb),
        input_output_aliases={first_carried + i: 1 + i for i in range(len(carried))},
        scratch_shapes=_mix_scratch(rows, nseq),
        compiler_params=pltpu.CompilerParams(
            dimension_semantics=("arbitrary", "arbitrary"), vmem_limit_bytes=VMEM_LIMIT),
        name=f"mixer_rows{rows}",
    )(proj, *states, *params, *carried)
    return outs[0], list(outs[1:])


def _layer_kernel(final, n_carried, *refs):
    lead, state_in, params, (y_ref,), state_out, scratch = _split_refs(refs, 6, n_carried, 1)
    x_ref, xnext_ref, nw_ref, win_ref, wout_ref, fnw_ref = lead
    proj_ref, mix_ref = scratch[:2]
    mix_scratch = scratch[2:]
    step = pl.program_id(0) * pl.num_programs(1) + pl.program_id(1)

    @pl.when(pl.program_id(1) == 0)
    def _():
        _load_state(state_in, state_out, *mix_scratch[:3])

    @pl.when(step == 0)
    def _():
        xn0 = _rms_norm(x_ref[0], nw_ref[...]).astype(BF16)
        for start in range(0, PROJ_W, PROJ_TILE):
            stop = min(start + PROJ_TILE, PROJ_W)
            proj_ref[0, :, start:stop] = _dot(xn0, win_ref[:, start:stop])

    tiles = [(s, min(s + PROJ_TILE, PROJ_W)) for s in range(0, PROJ_W, PROJ_TILE)]
    n_pieces = 18

    def chunk_step(cur, nxt):
        x = x_ref[0]
        xn_next = _rms_norm(xnext_ref[0], nw_ref[...]).astype(BF16)

        def project_some(piece):
            lo = (piece * len(tiles)) // n_pieces
            hi = ((piece + 1) * len(tiles)) // n_pieces
            for start, stop in tiles[lo:hi]:
                proj_ref[nxt, :, start:stop] = _dot(xn_next, win_ref[:, start:stop])

        def pcols(start, width):
            return proj_ref[cur, :, start:start + width]

        def mix_store(start, val):
            mix_ref[:, start:start + val.shape[1]] = val.astype(BF16)

        def out_project(group):
            lo, hi = group * BRANCH, (group + 1) * BRANCH
            return _dot(mix_ref[:, lo:hi], wout_ref[lo:hi, :])

        project_some(0)
        groups = _seq_groups(CHUNK, 0, pcols, mix_store, state_out, mix_scratch, params)
        piece = 1
        y = x
        for gi, group in enumerate(groups):
            for mark in group:
                if mark is MAJOR:
                    project_some(piece)
                    piece += 1
            y = y + out_project(gi)
        assert piece == n_pieces, piece
        if final:
            y = _rms_norm(y, fnw_ref[...])
        y_ref[0] = y

    for parity in range(2):
        pl.when(step % 2 == parity)(functools.partial(chunk_step, parity, 1 - parity))


def _layer_prompt(x, norm_w, w_in, w_out, fnw, states, params, layer, carried, final):
    nb, t, _ = x.shape
    nc = t // CHUNK
    carried = list(carried) if carried is not None else []
    st_in, st_out = _state_specs(0, layer, carried, 1)
    resident = dict(pipeline_mode=pl.Buffered(1))

    def next_chunk(b, c):
        step = jnp.minimum(b * nc + c + 1, nb * nc - 1)
        return (step // nc, step % nc, 0)

    in_specs = [
        pl.BlockSpec((1, CHUNK, D_MODEL), lambda b, c: (b, c, 0)),
        pl.BlockSpec((1, CHUNK, D_MODEL), next_chunk),
        pl.BlockSpec((None, 1, D_MODEL), lambda b, c: (layer, 0, 0)),
        pl.BlockSpec((None, D_MODEL, PROJ_W), lambda b, c: (layer, 0, 0), **resident),
        pl.BlockSpec((None, D_MODEL, D_MODEL), lambda b, c: (layer, 0, 0), **resident),
        pl.BlockSpec((1, D_MODEL), lambda b, c: (0, 0)),
    ] + st_in
    in_specs += [pl.BlockSpec((None,) + p.shape[1:], lambda b, c: (layer, 0, 0)) for p in params]
    in_specs += [pl.BlockSpec(memory_space=pl.ANY) for _ in carried]
    first_carried = 6 + N_STATES + N_PARAMS
    outs = pl.pallas_call(
        functools.partial(_layer_kernel, final, len(carried)),
        grid=(nb, nc),
        in_specs=in_specs,
        out_specs=[pl.BlockSpec((1, CHUNK, D_MODEL), lambda b, c: (b, c, 0))] + st_out,
        out_shape=[jax.ShapeDtypeStruct((nb, t, D_MODEL), F32)] + _state_shapes(nb),
        input_output_aliases={first_carried + i: 1 + i for i in range(len(carried))},
        scratch_shapes=[pltpu.VMEM((2, CHUNK, PROJ_W), F32), pltpu.VMEM((CHUNK, D_MODEL), BF16)]
        + _mix_scratch(CHUNK, 1),
        compiler_params=pltpu.CompilerParams(
            dimension_semantics=("arbitrary", "arbitrary"), vmem_limit_bytes=VMEM_LIMIT),
        name="layer_prompt",
    )(x, x, norm_w, w_in, w_out, fnw, *states, *params, *carried)
    return outs[0], list(outs[1:])


O_I, O_SSD_Z, O_XBC, O_DT, O_SC, O_RG, O_END = 2560, 2568, 3080, 4104, 4112, 6160, 7184
REORDER_TILE = 512
GATE_TILE = C_SMALL // REORDER_TILE


def _reorder_kernel(src_ref, if_ref, dt_ref, o_ref):
    j = pl.program_id(1)

    @pl.when(j < GATE_TILE)
    def _():
        o_ref[...] = src_ref[0].T.astype(BF16)

    @pl.when(j == GATE_TILE)
    def _():
        zeros = jnp.zeros((REORDER_TILE - 24, D_MODEL), F32)
        small = jnp.concatenate([if_ref[0], dt_ref[0], dt_ref[0], zeros], axis=0)
        o_ref[...] = small.T.astype(BF16)


def _reorder_w_in(w_in):
    nl, d, n = w_in.shape
    assert n == O_END and d == D_MODEL
    assert C_SSD_Z % REORDER_TILE == 0 and C_SC_B % REORDER_TILE == 0 and C_SMALL % REORDER_TILE == 0
    w_t = jnp.swapaxes(w_in, 1, 2)

    def src_row(l, j):
        t_ssd, t_sc = C_SSD_Z // REORDER_TILE, C_SC_B // REORDER_TILE
        row = jnp.where(j < t_ssd, j * REORDER_TILE,
                        jnp.where(j < t_sc, O_SSD_Z + (j - t_ssd) * REORDER_TILE,
                                  O_SC + (j - t_sc) * REORDER_TILE))
        row = jnp.where(j < GATE_TILE, row, 0)
        return (l, pl.multiple_of(row, 8), 0)

    return pl.pallas_call(
        _reorder_kernel,
        grid=(nl, GATE_TILE + 1),
        in_specs=[
            pl.BlockSpec((pl.Element(1), pl.Element(REORDER_TILE), pl.Element(d)), src_row),
            pl.BlockSpec((pl.Element(1), pl.Element(8), pl.Element(d)), lambda l, j: (l, O_I, 0)),
            pl.BlockSpec((pl.Element(1), pl.Element(8), pl.Element(d)), lambda l, j: (l, O_DT, 0)),
        ],
        out_specs=pl.BlockSpec((None, d, REORDER_TILE), lambda l, j: (l, 0, j)),
        out_shape=jax.ShapeDtypeStruct((nl, d, PROJ_W), BF16),
        compiler_params=pltpu.CompilerParams(
            dimension_semantics=("arbitrary", "arbitrary"), vmem_limit_bytes=VMEM_LIMIT),
        name="reorder_w_in",
    )(w_t, w_t, w_t)


def _block_diag(w):
    nl, nblk, d, e = w.shape
    eye = jnp.eye(nblk, dtype=w.dtype)
    return jnp.einsum('lnde,nm->lndme', w, eye).reshape(nl, nblk * d, nblk * e)


def _lane_row(parts, width=LANES):
    nl = parts[0][1].shape[0]
    row = jnp.zeros((nl, width), F32)
    for off, p in parts:
        row = lax.dynamic_update_slice(row, p.astype(F32), (0, off))
    return row[:, None, :]


def kernel(x_prompt, x_sample, state_mlstm_C, state_mlstm_n, state_mlstm_m, state_ssd, state_ssd_conv, state_sconv_conv, state_rglru_h, state_rglru_conv, norm_w, w_in, ml_i_bias, ml_f_bias, ml_norm_w, ssd_conv_w, ssd_conv_b, ssd_dt_bias, ssd_A_log, ssd_D, ssd_norm_w, sc_conv_w, rg_conv_w, rg_conv_b, rg_wa, rg_ba, rg_wx, rg_bx, rg_lambda, w_out, final_norm_w):
    bp, tp, _ = x_prompt.shape
    bs, ts, _ = x_sample.shape

    w_in_r = _reorder_w_in(w_in)
    w_out_b = w_out.astype(BF16)
    rg_w = jnp.concatenate([_block_diag(rg_wa), _block_diag(rg_wx)], axis=-1).astype(BF16)
    bias_row = _lane_row([(0, ml_i_bias), (4, ml_f_bias), (8, ssd_dt_bias), (16, ssd_dt_bias)])
    alog_row = _lane_row([(8, ssd_A_log)])
    d_row = jnp.repeat(ssd_D, SSD_P, axis=-1)[:, None, :]
    row = lambda p: p[:, None, :]
    params = [bias_row, alog_row, row(ml_norm_w), ssd_conv_w, row(ssd_conv_b), d_row, row(ssd_norm_w),
              sc_conv_w, rg_conv_w, row(rg_conv_b), rg_w, row(rg_ba), row(rg_bx), row(rg_lambda)]
    norm_rows = row(norm_w)

    def pad_m(m):
        return jnp.pad(m, ((0, 0), (0, 0), (4, LANES - 4 - ML_HEADS)))[:, :, None, :]

    sample_states = [state_mlstm_C, state_mlstm_n, pad_m(state_mlstm_m), state_ssd, state_ssd_conv,
                     state_sconv_conv, state_rglru_h[:, :, None, :], state_rglru_conv]
    prompt_states = [
        jnp.zeros((1, bp, ML_HEADS, ML_D, ML_D), F32),
        jnp.zeros((1, bp, ML_HEADS, ML_D), F32),
        pad_m(jnp.full((1, bp, ML_HEADS), NEG_BIG, F32)),
        jnp.zeros((1, bp, SSD_HEADS, SSD_P, SSD_N), F32),
        jnp.zeros((1, bp, SSD_CONV - 1, SSD_CONV_DIM), F32),
        jnp.zeros((1, bp, SC_WIDTH - 1, BRANCH), F32),
        jnp.zeros((1, bp, 1, BRANCH), F32),
        jnp.zeros((1, bp, RG_CONV - 1, BRANCH), F32),
    ]

    xp = x_prompt
    xs = x_sample.reshape(bs * ts, D_MODEL)
    fnw = final_norm_w[None, :]
    st_p, st_s = None, None
    for l in range(DEPTH):
        final = l == DEPTH - 1
        xp, st_p = _layer_prompt(xp, norm_rows, w_in_r, w_out_b, fnw, prompt_states, params, l, st_p, final)
        proj_s = _inproj(xs, norm_rows, w_in_r, l, 512)
        mix_s, st_s = _mixer(proj_s, sample_states, l, params, l, st_s, ts, SAMPLE_SEQS, F32)
        xs = _outproj(mix_s, xs, w_out_b, fnw, l, final, 512)

    def unpack(states):
        states = list(states)
        states[2] = states[2][:, :, 0, 4:4 + ML_HEADS]
        states[6] = states[6][:, :, 0, :]
        return tuple(states)

    return (xp, xs.reshape(bs, ts, D_MODEL)) + unpack(st_p) + unpack(st_s)
```

```python
import functools

import jax
import jax.numpy as jnp
from jax import lax
from jax.experimental import pallas as pl
from jax.experimental.pallas import tpu as pltpu

F32 = jnp.float32
BF16 = jnp.bfloat16

D_MODEL = 2048
DEPTH = 2
BRANCH = 512
CHUNK = 128
EPS = 1e-6
ML_HEADS = 4
ML_D = 128
NEG_BIG = -1e30
SSD_HEADS = 8
SSD_P = 64
SSD_GROUPS = 2
SSD_N = 128
SSD_CONV = 4
SSD_CONV_DIM = 1024
SC_WIDTH = 3
RG_CONV = 4
RG_C = 8.0

C_Q, C_K, C_V, C_O, C_Z = 0, 512, 1024, 1536, 2048
C_SSD_Z, C_XBC = 2560, 3072
C_SC_B, C_SC_C, C_SC_H, C_SC_Z = 4096, 4608, 5120, 5632
C_RG_X, C_RG_Z = 6144, 6656
C_SMALL = 7168
PROJ_W = 7296
PROJ_TILE = 256
MAJOR = "major"
LANES = 128
CONV_BASE = 8
SAMPLE_SEQS = 8

V7X_VMEM_BYTES = 64 * 1024 * 1024
VMEM_LIMIT = V7X_VMEM_BYTES - 4 * 1024 * 1024

NT_DIMS = (((1,), (1,)), ((), ()))
TN_DIMS = (((0,), (0,)), ((), ()))

N_STATES = 8
N_PARAMS = 14
STATE_BLOCKS = [
    (1, 1, ML_HEADS, ML_D, ML_D), (1, 1, ML_HEADS, ML_D), (1, 1, 1, LANES),
    (1, 1, SSD_HEADS, SSD_P, SSD_N), (1, 1, SSD_CONV - 1, SSD_CONV_DIM),
    (1, 1, SC_WIDTH - 1, BRANCH), (1, 1, 1, BRANCH), (1, 1, RG_CONV - 1, BRANCH),
]


def _softplus(x):
    return jnp.maximum(x, 0.0) + jnp.log1p(jnp.exp(-jnp.abs(x)))


def _silu(x):
    return x * jax.nn.sigmoid(x)


def _dot(a, b):
    return jnp.dot(a, b, preferred_element_type=F32)


def _dot_nt(a, b):
    return lax.dot_general(a, b, NT_DIMS, preferred_element_type=F32)


def _dot_tn(a, b):
    return lax.dot_general(a, b, TN_DIMS, preferred_element_type=F32)


def _rms_norm(x, w):
    return x * lax.rsqrt(jnp.mean(x * x, axis=-1, keepdims=True) + EPS) * w


def _inproj_kernel(x_ref, nw_ref, w_ref, o_ref, xn_ref):
    @pl.when(pl.program_id(1) == 0)
    def _():
        xn_ref[...] = _rms_norm(x_ref[...], nw_ref[...]).astype(BF16)

    o_ref[...] = _dot(xn_ref[...], w_ref[...])


def _inproj(x, norm_w, w, layer, tm):
    m = x.shape[0]
    tn = PROJ_W // 3
    return pl.pallas_call(
        _inproj_kernel,
        grid=(m // tm, PROJ_W // tn),
        in_specs=[
            pl.BlockSpec((tm, D_MODEL), lambda i, j: (i, 0)),
            pl.BlockSpec((None, 1, D_MODEL), lambda i, j: (layer, 0, 0)),
            pl.BlockSpec((None, D_MODEL, tn), lambda i, j: (layer, 0, j)),
        ],
        out_specs=pl.BlockSpec((tm, tn), lambda i, j: (i, j)),
        out_shape=jax.ShapeDtypeStruct((m, PROJ_W), F32),
        scratch_shapes=[pltpu.VMEM((tm, D_MODEL), BF16)],
        compiler_params=pltpu.CompilerParams(
            dimension_semantics=("arbitrary", "arbitrary"), vmem_limit_bytes=VMEM_LIMIT),
        name="inproj",
    )(x, norm_w, w)


def _outproj_kernel(final, mix_ref, x_ref, w_ref, fnw_ref, o_ref):
    y = x_ref[...] + _dot(mix_ref[...].astype(BF16), w_ref[...])
    if final:
        y = _rms_norm(y, fnw_ref[...])
    o_ref[...] = y


def _outproj(mix, x, w, fnw, layer, final, tm):
    m = x.shape[0]
    return pl.pallas_call(
        functools.partial(_outproj_kernel, final),
        grid=(m // tm,),
        in_specs=[
            pl.BlockSpec((tm, D_MODEL), lambda i: (i, 0)),
            pl.BlockSpec((tm, D_MODEL), lambda i: (i, 0)),
            pl.BlockSpec((None, D_MODEL, D_MODEL), lambda i: (layer, 0, 0)),
            pl.BlockSpec((1, D_MODEL), lambda i: (0, 0)),
        ],
        out_specs=pl.BlockSpec((tm, D_MODEL), lambda i: (i, 0)),
        out_shape=jax.ShapeDtypeStruct((m, D_MODEL), F32),
        compiler_params=pltpu.CompilerParams(
            dimension_semantics=("arbitrary",), vmem_limit_bytes=VMEM_LIMIT),
        name="outproj",
    )(mix, x, w, fnw)


def _causal_conv(ext_ref, u, w_ref, width, rows):
    lo = CONV_BASE - (width - 1)
    if rows < CONV_BASE:
        ext_ref[CONV_BASE:CONV_BASE + rows, :] = u
        acc = ext_ref[lo:lo + rows, :] * w_ref[0:1, :]
        for j in range(1, width):
            acc = acc + ext_ref[lo + j:lo + j + rows, :] * w_ref[j:j + 1, :]
        tail = ext_ref[lo + rows:CONV_BASE + rows, :]
        ext_ref[lo:CONV_BASE, :] = tail
        return acc, tail
    prev = ext_ref[0:CONV_BASE, :]
    row = lax.broadcasted_iota(jnp.int32, prev.shape, 0)
    acc = None
    for j in range(width):
        shift = width - 1 - j
        if shift:
            moved = pltpu.roll(u, shift, 0)
            first = jnp.where(row < shift, pltpu.roll(prev, shift, 0), moved[0:CONV_BASE, :])
            moved = jnp.concatenate([first, moved[CONV_BASE:, :]], axis=0)
        else:
            moved = u
        term = moved * w_ref[j:j + 1, :]
        acc = term if acc is None else acc + term
    ext_ref[0:CONV_BASE, :] = u[rows - CONV_BASE:rows, :]
    return acc, ext_ref[lo:CONV_BASE, :]


def _linear_scan(a, b, h0, rows):
    if rows < 8:
        h = h0
        out = []
        for t in range(rows):
            h = a[t:t + 1, :] * h + b[t:t + 1, :]
            out.append(h)
        return jnp.concatenate(out, axis=0)
    row = lax.broadcasted_iota(jnp.int32, a.shape, 0)
    d = 1
    while d < rows:
        keep = row >= d
        a_prev = jnp.where(keep, pltpu.roll(a, d, 0), 1.0)
        b_prev = jnp.where(keep, pltpu.roll(b, d, 0), 0.0)
        b = a * b_prev + b
        a = a * a_prev
        d *= 2
    return b + a * h0


def _cumsum_rows(z, rows):
    if rows < 8:
        out = [z[0:1, :]]
        for t in range(1, rows):
            out.append(out[-1] + z[t:t + 1, :])
        return jnp.concatenate(out, axis=0)
    row = lax.broadcasted_iota(jnp.int32, z.shape, 0)
    d = 1
    while d < rows:
        z = z + jnp.where(row >= d, pltpu.roll(z, d, 0), 0.0)
        d *= 2
    return z


def _load_state(state_in, state_out, ssd_ext, sc_ext, rg_ext):
    c_in, n_in, m_in, s_in, ssdcv_in, sccv_in, rgh_in, rgcv_in = state_in
    c_out, n_out, m_out, s_out, ssdcv_out, sccv_out, rgh_out, rgcv_out = state_out
    c_out[0:1] = c_in[...]
    n_out[0:1] = n_in[...]
    m_out[0:1] = m_in[...]
    s_out[0:1] = s_in[...]
    rgh_out[0:1] = rgh_in[...]
    for g in range(ssd_ext.shape[0]):
        for ext in (ssd_ext, sc_ext, rg_ext):
            ext[g, 0:CONV_BASE, :] = jnp.zeros((CONV_BASE, ext.shape[2]), F32)
        ssd_ext[g, CONV_BASE - (SSD_CONV - 1):CONV_BASE, :] = ssdcv_in[0, g]
        sc_ext[g, CONV_BASE - (SC_WIDTH - 1):CONV_BASE, :] = sccv_in[0, g]
        rg_ext[g, CONV_BASE - (RG_CONV - 1):CONV_BASE, :] = rgcv_in[0, g]
    for out in state_out:
        if out.shape[0] > 1:
            out[1:] = jnp.zeros((out.shape[0] - 1,) + out.shape[1:], out.dtype)


def _gate_prep(L, pcols, bias_ref, alog_ref):
    small = pcols(C_SMALL, LANES) + bias_ref[...]
    lane = lax.broadcasted_iota(jnp.int32, (L, LANES), 1)
    sp = _softplus(small)
    logsig = -_softplus(-small)
    a_coef = -jnp.exp(alog_ref[...])
    z = jnp.where((lane >= 4) & (lane < 8), logsig,
                  jnp.where((lane >= 8) & (lane < 16), sp * a_coef, 0.0))
    rr = lax.broadcasted_iota(jnp.int32, (L, L), 0)
    cc = lax.broadcasted_iota(jnp.int32, (L, L), 1)
    causal = cc <= rr
    cum = _cumsum_rows(z, L)
    packed = jnp.where(lane < 4, small,
                       jnp.where(lane < 16, cum, jnp.where(lane < 24, sp, 0.0)))
    packed_t = packed.T
    return small, sp, cum, packed_t, causal


def _group_mlstm(L, pcols, mix_store, gates, c_out, n_out, m_out, mlnw_ref):
    small, _, cum, packed_t, causal = gates
    m_row = m_out[...]
    m_news = [None] * ML_HEADS

    def head(h):
        q = pcols(C_Q + h * ML_D, ML_D)
        k = pcols(C_K + h * ML_D, ML_D) * (ML_D ** -0.5)
        v = pcols(C_V + h * ML_D, ML_D)
        qb, kb, vb = q.astype(BF16), k.astype(BF16), v.astype(BF16)
        yield
        c_h = c_out[h]
        qk = _dot_nt(qb, kb)
        qc = _dot(qb, c_h.astype(BF16))
        yield
        b_col = cum[:, 4 + h:5 + h]
        i_col = small[:, h:h + 1]
        b_row = packed_t[4 + h:5 + h, :]
        i_row = packed_t[h:h + 1, :]
        m_h = m_row[:, 4 + h:5 + h]
        dmat = jnp.where(causal, b_col - b_row + i_row, -jnp.inf)
        inter = b_col + m_h
        yield
        m_t = jnp.maximum(inter, jnp.max(dmat, axis=1, keepdims=True))
        yield
        w_intra = jnp.exp(dmat - m_t)
        w_inter = jnp.exp(inter - m_t)
        yield
        s = qk * w_intra
        n_h = n_out[h:h + 1, :]
        yield
        num = _dot(s.astype(BF16), vb) + w_inter * qc
        yield
        den = jnp.sum(s, axis=1, keepdims=True) + w_inter * jnp.sum(q * n_h, axis=1, keepdims=True)
        yield
        hh = num / jnp.maximum(jnp.abs(den), jnp.exp(-m_t))
        yield
        b_last = cum[L - 1:L, 4 + h:5 + h]
        g = b_last - b_col + i_col
        inter_end = b_last + m_h
        m_new = jnp.maximum(inter_end, jnp.max(g, axis=0, keepdims=True))
        yield
        wg = jnp.exp(g - m_new)
        we = jnp.exp(inter_end - m_new)
        kw = k * wg
        yield
        c_out[h] = we * c_h + _dot_tn(kw.astype(BF16), vb)
        yield
        n_out[h:h + 1, :] = we * n_h + jnp.sum(kw, axis=0, keepdims=True)
        m_news[h] = m_new
        yield
        mu = jnp.mean(hh, axis=-1, keepdims=True)
        yield
        hc = hh - mu
        var = jnp.mean(hc * hc, axis=-1, keepdims=True)
        yield
        hn = hc * lax.rsqrt(var + EPS) * mlnw_ref[:, h * ML_D:(h + 1) * ML_D]
        o = pcols(C_O + h * ML_D, ML_D)
        zg = pcols(C_Z + h * ML_D, ML_D)
        mix_store(h * ML_D, jax.nn.sigmoid(o) * hn * _silu(zg))
        yield MAJOR

    for h in range(ML_HEADS):
        yield from head(h)
    lane1 = lax.broadcasted_iota(jnp.int32, (1, LANES), 1)
    for h in range(ML_HEADS):
        m_row = jnp.where(lane1 == 4 + h, m_news[h], m_row)
    m_out[...] = m_row


def _group_ssd(L, pcols, mix_store, gates, s_out, ssdcv_out, ssd_ext, yssd_ref, erep_ref, wrep_ref,
               ssdcw_ref, ssdcb_ref, drow_ref, ssdnw_ref):
    _, sp, cum, packed_t, causal = gates
    conv, tail = _causal_conv(ssd_ext, pcols(C_XBC, SSD_CONV_DIM), ssdcw_ref, SSD_CONV, L)
    ssdcv_out[...] = tail
    yield
    xbc = _silu(conv + ssdcb_ref[...])
    xs_all = xbc[:, 0:BRANCH]
    yield
    rep = SSD_HEADS // SSD_GROUPS
    gw = rep * SSD_P
    cb, bm_b, s_grp, cs = [], [], [], []
    for g in range(SSD_GROUPS):
        bm_g = xbc[:, BRANCH + g * SSD_N:BRANCH + (g + 1) * SSD_N].astype(BF16)
        cm_g = xbc[:, BRANCH + (SSD_GROUPS + g) * SSD_N:BRANCH + (SSD_GROUPS + g + 1) * SSD_N].astype(BF16)
        bm_b.append(bm_g)
        cb.append(_dot_nt(cm_g, bm_g))
        s_g = s_out[g * rep:(g + 1) * rep].reshape(gw, SSD_N)
        s_grp.append(s_g)
        cs.append(_dot_nt(cm_g, s_g.astype(BF16)))
        yield MAJOR if g == SSD_GROUPS - 1 else None

    def head(h):
        g = h // rep
        cols = slice(h * SSD_P, (h + 1) * SSD_P)
        xs_h = xs_all[:, cols]
        acs_col = cum[:, 8 + h:9 + h]
        acs_row = packed_t[8 + h:9 + h, :]
        dt_row = packed_t[16 + h:17 + h, :]
        dt_col = sp[:, 8 + h:9 + h]
        decay = jnp.exp(jnp.where(causal, acs_col - acs_row, -jnp.inf))
        yield
        scores = cb[g] * decay * dt_row
        yield
        yssd_ref[:, cols] = _dot(scores.astype(BF16), xs_h.astype(BF16))
        yield
        erep_ref[:, cols] = jnp.broadcast_to(jnp.exp(acs_col), (L, SSD_P))
        a_last = cum[L - 1:L, 8 + h:9 + h]
        wrep_ref[:, cols] = jnp.broadcast_to(jnp.exp(a_last - acs_col) * dt_col, (L, SSD_P))
        yield MAJOR

    for h in range(SSD_HEADS):
        yield from head(h)
    for g in range(SSD_GROUPS):
        cols = slice(g * gw, (g + 1) * gw)
        xs_g = xs_all[:, cols]
        y_g = yssd_ref[:, cols] + erep_ref[:, cols] * cs[g] + drow_ref[:, cols] * xs_g
        yield
        decay_end = jnp.concatenate(
            [jnp.broadcast_to(jnp.exp(cum[L - 1:L, 8 + h:9 + h]), (SSD_P, SSD_N))
             for h in range(g * rep, (g + 1) * rep)], axis=0)
        upd = _dot_tn((xs_g * wrep_ref[:, cols]).astype(BF16), bm_b[g])
        s_out[g * rep:(g + 1) * rep] = (decay_end * s_grp[g] + upd).reshape(rep, SSD_P, SSD_N)
        yield
        seg = y_g * _silu(pcols(C_SSD_Z + g * gw, gw))
        seg = seg * lax.rsqrt(jnp.mean(seg * seg, axis=-1, keepdims=True) + EPS)
        mix_store(BRANCH + g * gw, seg * ssdnw_ref[:, cols])
        yield MAJOR if g == SSD_GROUPS - 1 else None


def _group_sconv(L, pcols, mix_store, sccv_out, sc_ext, sccw_ref):
    u = pcols(C_SC_C, BRANCH) * pcols(C_SC_H, BRANCH)
    yield
    cu, tail = _causal_conv(sc_ext, u, sccw_ref, SC_WIDTH, L)
    sccv_out[...] = tail
    yield
    mix_store(2 * BRANCH, pcols(C_SC_B, BRANCH) * cu * _silu(pcols(C_SC_Z, BRANCH)))
    yield MAJOR


def _group_rglru(L, pcols, mix_store, rgh_out, rgcv_out, rg_ext,
                 rgcw_ref, rgcb_ref, rgw_ref, rgba_ref, rgbx_ref, rglam_ref):
    conv, tail = _causal_conv(rg_ext, pcols(C_RG_X, BRANCH), rgcw_ref, RG_CONV, L)
    rgcv_out[...] = tail
    xr = conv + rgcb_ref[...]
    yield
    pre = _dot(xr.astype(BF16), rgw_ref[...])
    yield
    r = jax.nn.sigmoid(pre[:, 0:BRANCH] + rgba_ref[...])
    ig = jax.nn.sigmoid(pre[:, BRANCH:2 * BRANCH] + rgbx_ref[...])
    yield
    log_a = (-RG_C * r) * _softplus(-rglam_ref[...])
    a = jnp.exp(log_a)
    bterm = jnp.sqrt(-jnp.tanh(log_a) * (a * a + 1.0)) * (ig * xr)
    yield MAJOR
    h_rg = _linear_scan(a, bterm, rgh_out[...], L)
    rgh_out[...] = h_rg[L - 1:L, :]
    yield
    mix_store(3 * BRANCH, h_rg * _silu(pcols(C_RG_Z, BRANCH)))
    yield MAJOR


def _split_refs(refs, n_lead, n_carried, n_out_lead):
    lead = refs[:n_lead]
    state_in = refs[n_lead:n_lead + N_STATES]
    params = refs[n_lead + N_STATES:n_lead + N_STATES + N_PARAMS]
    rest = refs[n_lead + N_STATES + N_PARAMS + n_carried:]
    out_lead = rest[:n_out_lead]
    state_out = rest[n_out_lead:n_out_lead + N_STATES]
    scratch = rest[n_out_lead + N_STATES:]
    return lead, state_in, params, out_lead, state_out, scratch


def _seq_groups(L, seq, pcols, mix_store, state_out, scratch, params):
    (bias_ref, alog_ref, mlnw_ref, ssdcw_ref, ssdcb_ref, drow_ref, ssdnw_ref, sccw_ref, rgcw_ref,
     rgcb_ref, rgw_ref, rgba_ref, rgbx_ref, rglam_ref) = params
    c_out, n_out, m_out, s_out, ssdcv_out, sccv_out, rgh_out, rgcv_out = [r.at[0, seq] for r in state_out]
    ssd_ext, sc_ext, rg_ext, yssd_ref, erep_ref, wrep_ref = [r.at[seq] for r in scratch]
    gates = _gate_prep(L, pcols, bias_ref, alog_ref)
    return [
        _group_mlstm(L, pcols, mix_store, gates, c_out, n_out, m_out, mlnw_ref),
        _group_ssd(L, pcols, mix_store, gates, s_out, ssdcv_out, ssd_ext, yssd_ref, erep_ref, wrep_ref,
                   ssdcw_ref, ssdcb_ref, drow_ref, ssdnw_ref),
        _group_sconv(L, pcols, mix_store, sccv_out, sc_ext, sccw_ref),
        _group_rglru(L, pcols, mix_store, rgh_out, rgcv_out, rg_ext,
                     rgcw_ref, rgcb_ref, rgw_ref, rgba_ref, rgbx_ref, rglam_ref),
    ]


def _mixer_kernel(rows, nseq, n_carried, *refs):
    (proj_ref,), state_in, params, (mix_ref,), state_out, scratch = _split_refs(refs, 1, n_carried, 1)

    @pl.when(pl.program_id(1) == 0)
    def _():
        _load_state(state_in, state_out, *scratch[:3])

    def chain(seq):
        lo, hi = seq * rows, (seq + 1) * rows

        def pcols(start, width):
            return proj_ref[lo:hi, start:start + width]

        def mix_store(start, val):
            mix_ref[lo:hi, start:start + val.shape[1]] = val.astype(mix_ref.dtype)

        for group in _seq_groups(rows, seq, pcols, mix_store, state_out, scratch, params):
            yield from group

    chains = [chain(seq) for seq in range(nseq)]
    done = object()
    while chains:
        chains = [c for c in chains if next(c, done) is not done]


def _state_specs(state_layer, layer, carried, nseq):
    def at_layer(lyr, blk):
        blk = (blk[0], nseq) + blk[2:]
        tail = (0,) * (len(blk) - 2)
        return pl.BlockSpec(blk, lambda b, c: (lyr, b) + tail)

    in_specs = [at_layer(state_layer, blk) for blk in STATE_BLOCKS]
    if carried:
        out_specs = [at_layer(layer, blk) for blk in STATE_BLOCKS]
    else:
        assert layer == 0
        out_specs = [at_layer(0, (DEPTH,) + blk[1:]) for blk in STATE_BLOCKS]
    return in_specs, out_specs


def _state_shapes(nb):
    return [jax.ShapeDtypeStruct((DEPTH, nb) + blk[2:], F32) for blk in STATE_BLOCKS]


def _mix_scratch(rows, nseq):
    return [
        pltpu.VMEM((nseq, CONV_BASE + rows, SSD_CONV_DIM), F32),
        pltpu.VMEM((nseq, CONV_BASE + rows, BRANCH), F32),
        pltpu.VMEM((nseq, CONV_BASE + rows, BRANCH), F32),
        pltpu.VMEM((nseq, rows, BRANCH), F32),
        pltpu.VMEM((nseq, rows, BRANCH), F32),
        pltpu.VMEM((nseq, rows, BRANCH), F32),
    ]


def _mixer(proj, states, state_layer, params, layer, carried, rows, nseq, mix_dtype):
    nb = proj.shape[0] // rows
    carried = list(carried) if carried is not None else []
    st_in, st_out = _state_specs(state_layer, layer, carried, nseq)
    in_specs = [pl.BlockSpec((nseq * rows, PROJ_W), lambda b, c: (b, 0))] + st_in
    in_specs += [pl.BlockSpec((None,) + p.shape[1:], lambda b, c: (layer, 0, 0)) for p in params]
    in_specs += [pl.BlockSpec(memory_space=pl.ANY) for _ in carried]
    first_carried = 1 + N_STATES + N_PARAMS
    outs = pl.pallas_call(
        functools.partial(_mixer_kernel, rows, nseq, len(carried)),
        grid=(nb // nseq, 1),
        in_specs=in_specs,
        out_specs=[pl.BlockSpec((nseq * rows, D_MODEL), lambda b, c: (b, 0))] + st_out,
        out_shape=[jax.ShapeDtypeStruct((nb * rows, D_MODEL), mix_dtype)] + _state_shapes(nb),
        input_output_aliases={first_carried + i: 1 + i for i in range(len(carried))},
        scratch_shapes=_mix_scratch(rows, nseq),
        compiler_params=pltpu.CompilerParams(
            dimension_semantics=("arbitrary", "arbitrary"), vmem_limit_bytes=VMEM_LIMIT),
        name=f"mixer_rows{rows}",
    )(proj, *states, *params, *carried)
    return outs[0], list(outs[1:])


def _layer_kernel(final, n_carried, *refs):
    lead, state_in, params, (y_ref,), state_out, scratch = _split_refs(refs, 6, n_carried, 1)
    x_ref, xnext_ref, nw_ref, win_ref, wout_ref, fnw_ref = lead
    proj_ref, mix_ref = scratch[:2]
    mix_scratch = scratch[2:]
    step = pl.program_id(0) * pl.num_programs(1) + pl.program_id(1)

    @pl.when(pl.program_id(1) == 0)
    def _():
        _load_state(state_in, state_out, *mix_scratch[:3])

    @pl.when(step == 0)
    def _():
        xn0 = _rms_norm(x_ref[0], nw_ref[...]).astype(BF16)
        for start in range(0, PROJ_W, PROJ_TILE):
            stop = min(start + PROJ_TILE, PROJ_W)
            proj_ref[0, :, start:stop] = _dot(xn0, win_ref[:, start:stop])

    tiles = [(s, min(s + PROJ_TILE, PROJ_W)) for s in range(0, PROJ_W, PROJ_TILE)]
    n_pieces = 18

    def chunk_step(cur, nxt):
        x = x_ref[0]
        xn_next = _rms_norm(xnext_ref[0], nw_ref[...]).astype(BF16)

        def project_some(piece):
            lo = (piece * len(tiles)) // n_pieces
            hi = ((piece + 1) * len(tiles)) // n_pieces
            for start, stop in tiles[lo:hi]:
                proj_ref[nxt, :, start:stop] = _dot(xn_next, win_ref[:, start:stop])

        def pcols(start, width):
            return proj_ref[cur, :, start:start + width]

        def mix_store(start, val):
            mix_ref[:, start:start + val.shape[1]] = val.astype(BF16)

        def out_project(group):
            lo, hi = group * BRANCH, (group + 1) * BRANCH
            return _dot(mix_ref[:, lo:hi], wout_ref[lo:hi, :])

        project_some(0)
        groups = _seq_groups(CHUNK, 0, pcols, mix_store, state_out, mix_scratch, params)
        piece = 1
        y = x
        for gi, group in enumerate(groups):
            for mark in group:
                if mark is MAJOR:
                    project_some(piece)
                    piece += 1
            y = y + out_project(gi)
        assert piece == n_pieces, piece
        if final:
            y = _rms_norm(y, fnw_ref[...])
        y_ref[0] = y

    for parity in range(2):
        pl.when(step % 2 == parity)(functools.partial(chunk_step, parity, 1 - parity))


def _layer_prompt(x, norm_w, w_in, w_out, fnw, states, params, layer, carried, final):
    nb, t, _ = x.shape
    nc = t // CHUNK
    carried = list(carried) if carried is not None else []
    st_in, st_out = _state_specs(0, layer, carried, 1)
    resident = dict(pipeline_mode=pl.Buffered(1))

    def next_chunk(b, c):
        step = jnp.minimum(b * nc + c + 1, nb * nc - 1)
        return (step // nc, step % nc, 0)

    in_specs = [
        pl.BlockSpec((1, CHUNK, D_MODEL), lambda b, c: (b, c, 0)),
        pl.BlockSpec((1, CHUNK, D_MODEL), next_chunk),
        pl.BlockSpec((None, 1, D_MODEL), lambda b, c: (layer, 0, 0)),
        pl.BlockSpec((None, D_MODEL, PROJ_W), lambda b, c: (layer, 0, 0), **resident),
        pl.BlockSpec((None, D_MODEL, D_MODEL), lambda b, c: (layer, 0, 0), **resident),
        pl.BlockSpec((1, D_MODEL), lambda b, c: (0, 0)),
    ] + st_in
    in_specs += [pl.BlockSpec((None,) + p.shape[1:], lambda b, c: (layer, 0, 0)) for p in params]
    in_specs += [pl.BlockSpec(memory_space=pl.ANY) for _ in carried]
    first_carried = 6 + N_STATES + N_PARAMS
    outs = pl.pallas_call(
        functools.partial(_layer_kernel, final, len(carried)),
        grid=(nb, nc),
        in_specs=in_specs,
        out_specs=[pl.BlockSpec((1, CHUNK, D_MODEL), lambda b, c: (b, c, 0))] + st_out,
        out_shape=[jax.ShapeDtypeStruct((nb, t, D_MODEL), F32)] + _state_shapes(nb),
        input_output_aliases={first_carried + i: 1 + i for i in range(len(carried))},
        scratch_shapes=[pltpu.VMEM((2, CHUNK, PROJ_W), F32), pltpu.VMEM((CHUNK, D_MODEL), BF16)]
        + _mix_scratch(CHUNK, 1),
        compiler_params=pltpu.CompilerParams(
            dimension_semantics=("arbitrary", "arbitrary"), vmem_limit_bytes=VMEM_LIMIT),
        name="layer_prompt",
    )(x, x, norm_w, w_in, w_out, fnw, *states, *params, *carried)
    return outs[0], list(outs[1:])


O_I, O_SSD_Z, O_XBC, O_DT, O_SC, O_RG, O_END = 2560, 2568, 3080, 4104, 4112, 6160, 7184
REORDER_TILE = 512
GATE_TILE = C_SMALL // REORDER_TILE


def _reorder_kernel(src_ref, if_ref, dt_ref, o_ref):
    j = pl.program_id(1)

    @pl.when(j < GATE_TILE)
    def _():
        o_ref[...] = src_ref[0].T.astype(BF16)

    @pl.when(j == GATE_TILE)
    def _():
        zeros = jnp.zeros((REORDER_TILE - 24, D_MODEL), F32)
        small = jnp.concatenate([if_ref[0], dt_ref[0], dt_ref[0], zeros], axis=0)
        o_ref[...] = small.T.astype(BF16)


def _reorder_w_in(w_in):
    nl, d, n = w_in.shape
    assert n == O_END and d == D_MODEL
    assert C_SSD_Z % REORDER_TILE == 0 and C_SC_B % REORDER_TILE == 0 and C_SMALL % REORDER_TILE == 0
    w_t = jnp.swapaxes(w_in, 1, 2)

    def src_row(l, j):
        t_ssd, t_sc = C_SSD_Z // REORDER_TILE, C_SC_B // REORDER_TILE
        row = jnp.where(j < t_ssd, j * REORDER_TILE,
                        jnp.where(j < t_sc, O_SSD_Z + (j - t_ssd) * REORDER_TILE,
                                  O_SC + (j - t_sc) * REORDER_TILE))
        row = jnp.where(j < GATE_TILE, row, 0)
        return (l, pl.multiple_of(row, 8), 0)

    return pl.pallas_call(
        _reorder_kernel,
        grid=(nl, GATE_TILE + 1),
        in_specs=[
            pl.BlockSpec((pl.Element(1), pl.Element(REORDER_TILE), pl.Element(d)), src_row),
            pl.BlockSpec((pl.Element(1), pl.Element(8), pl.Element(d)), lambda l, j: (l, O_I, 0)),
            pl.BlockSpec((pl.Element(1), pl.Element(8), pl.Element(d)), lambda l, j: (l, O_DT, 0)),
        ],
        out_specs=pl.BlockSpec((None, d, REORDER_TILE), lambda l, j: (l, 0, j)),
        out_shape=jax.ShapeDtypeStruct((nl, d, PROJ_W), BF16),
        compiler_params=pltpu.CompilerParams(
            dimension_semantics=("arbitrary", "arbitrary"), vmem_limit_bytes=VMEM_LIMIT),
        name="reorder_w_in",
    )(w_t, w_t, w_t)


def _block_diag(w):
    nl, nblk, d, e = w.shape
    eye = jnp.eye(nblk, dtype=w.dtype)
    return jnp.einsum('lnde,nm->lndme', w, eye).reshape(nl, nblk * d, nblk * e)


def _lane_row(parts, width=LANES):
    nl = parts[0][1].shape[0]
    row = jnp.zeros((nl, width), F32)
    for off, p in parts:
        row = lax.dynamic_update_slice(row, p.astype(F32), (0, off))
    return row[:, None, :]


def kernel(x_prompt, x_sample, state_mlstm_C, state_mlstm_n, state_mlstm_m, state_ssd, state_ssd_conv, state_sconv_conv, state_rglru_h, state_rglru_conv, norm_w, w_in, ml_i_bias, ml_f_bias, ml_norm_w, ssd_conv_w, ssd_conv_b, ssd_dt_bias, ssd_A_log, ssd_D, ssd_norm_w, sc_conv_w, rg_conv_w, rg_conv_b, rg_wa, rg_ba, rg_wx, rg_bx, rg_lambda, w_out, final_norm_w):
    bp, tp, _ = x_prompt.shape
    bs, ts, _ = x_sample.shape

    w_in_r = _reorder_w_in(w_in)
    w_out_b = w_out.astype(BF16)
    rg_w = jnp.concatenate([_block_diag(rg_wa), _block_diag(rg_wx)], axis=-1).astype(BF16)
    bias_row = _lane_row([(0, ml_i_bias), (4, ml_f_bias), (8, ssd_dt_bias), (16, ssd_dt_bias)])
    alog_row = _lane_row([(8, ssd_A_log)])
    d_row = jnp.repeat(ssd_D, SSD_P, axis=-1)[:, None, :]
    row = lambda p: p[:, None, :]
    params = [bias_row, alog_row, row(ml_norm_w), ssd_conv_w, row(ssd_conv_b), d_row, row(ssd_norm_w),
              sc_conv_w, rg_conv_w, row(rg_conv_b), rg_w, row(rg_ba), row(rg_bx), row(rg_lambda)]
    norm_rows = row(norm_w)

    def pad_m(m):
        return jnp.pad(m, ((0, 0), (0, 0), (4, LANES - 4 - ML_HEADS)))[:, :, None, :]

    sample_states = [state_mlstm_C, state_mlstm_n, pad_m(state_mlstm_m), state_ssd, state_ssd_conv,
                     state_sconv_conv, state_rglru_h[:, :, None, :], state_rglru_conv]
    prompt_states = [
        jnp.zeros((1, bp, ML_HEADS, ML_D, ML_D), F32),
        jnp.zeros((1, bp, ML_HEADS, ML_D), F32),
        pad_m(jnp.full((1, bp, ML_HEADS), NEG_BIG, F32)),
        jnp.zeros((1, bp, SSD_HEADS, SSD_P, SSD_N), F32),
        jnp.zeros((1, bp, SSD_CONV - 1, SSD_CONV_DIM), F32),
        jnp.zeros((1, bp, SC_WIDTH - 1, BRANCH), F32),
        jnp.zeros((1, bp, 1, BRANCH), F32),
        jnp.zeros((1, bp, RG_CONV - 1, BRANCH), F32),
    ]

    xp = x_prompt
    xs = x_sample.reshape(bs * ts, D_MODEL)
    fnw = final_norm_w[None, :]
    st_p, st_s = None, None
    for l in range(DEPTH):
        final = l == DEPTH - 1
        xp, st_p = _layer_prompt(xp, norm_rows, w_in_r, w_out_b, fnw, prompt_states, params, l, st_p, final)
        proj_s = _inproj(xs, norm_rows, w_in_r, l, 512)
        mix_s, st_s = _mixer(proj_s, sample_states, l, params, l, st_s, ts, SAMPLE_SEQS, F32)
        xs = _outproj(mix_s, xs, w_out_b, fnw, l, final, 512)

    def unpack(states):
        states = list(states)
        states[2] = states[2][:, :, 0, 4:4 + ML_HEADS]
        states[6] = states[6][:, :, 0, :]
        return tuple(states)

    return (xp, xs.reshape(bs, ts, D_MODEL)) + unpack(st_p) + unpack(st_s)
```

```python
import functools

import jax
import jax.numpy as jnp
from jax import lax
from jax.experimental import pallas as pl
from jax.experimental.pallas import tpu as pltpu

F32 = jnp.float32
BF16 = jnp.bfloat16

D_MODEL = 2048
DEPTH = 2
BRANCH = 512
CHUNK = 128
EPS = 1e-6
ML_HEADS = 4
ML_D = 128
NEG_BIG = -1e30
SSD_HEADS = 8
SSD_P = 64
SSD_GROUPS = 2
SSD_N = 128
SSD_CONV = 4
SSD_CONV_DIM = 1024
SC_WIDTH = 3
RG_CONV = 4
RG_C = 8.0
RG_BLOCKS = 8

C_Q, C_K, C_V, C_O, C_Z = 0, 512, 1024, 1536, 2048
C_SSD_Z, C_XBC = 2560, 3072
C_SC_B, C_SC_C, C_SC_H, C_SC_Z = 4096, 4608, 5120, 5632
C_RG_X, C_RG_Z = 6144, 6656
C_SMALL = 7168
PROJ_W = 7296
PROJ_TILE = 256
MAJOR = "major"
LANES = 128
CONV_BASE = 8
SAMPLE_SEQS = 8

V7X_VMEM_BYTES = 64 * 1024 * 1024
VMEM_LIMIT = V7X_VMEM_BYTES - 4 * 1024 * 1024

NT_DIMS = (((1,), (1,)), ((), ()))
TN_DIMS = (((0,), (0,)), ((), ()))

N_STATES = 8
N_PARAMS = 14
STATE_BLOCKS = [
    (1, 1, ML_HEADS, ML_D, ML_D), (1, 1, ML_HEADS, ML_D), (1, 1, 1, LANES),
    (1, 1, SSD_HEADS, SSD_P, SSD_N), (1, 1, SSD_CONV - 1, SSD_CONV_DIM),
    (1, 1, SC_WIDTH - 1, BRANCH), (1, 1, 1, BRANCH), (1, 1, RG_CONV - 1, BRANCH),
]


def _softplus(x):
    return jnp.maximum(x, 0.0) + jnp.log1p(jnp.exp(-jnp.abs(x)))


def _silu(x):
    return x * jax.nn.sigmoid(x)


def _dot(a, b):
    return jnp.dot(a, b, preferred_element_type=F32)


def _dot_nt(a, b):
    return lax.dot_general(a, b, NT_DIMS, preferred_element_type=F32)


def _dot_tn(a, b):
    return lax.dot_general(a, b, TN_DIMS, preferred_element_type=F32)


def _rms_norm(x, w):
    return x * lax.rsqrt(jnp.mean(x * x, axis=-1, keepdims=True) + EPS) * w


def _inproj_kernel(x_ref, nw_ref, w_ref, o_ref, xn_ref):
    @pl.when(pl.program_id(1) == 0)
    def _():
        xn_ref[...] = _rms_norm(x_ref[...], nw_ref[...]).astype(BF16)

    o_ref[...] = _dot(xn_ref[...], w_ref[...])


def _inproj(x, norm_w, w, layer, tm):
    m = x.shape[0]
    tn = PROJ_W // 3
    return pl.pallas_call(
        _inproj_kernel,
        grid=(m // tm, PROJ_W // tn),
        in_specs=[
            pl.BlockSpec((tm, D_MODEL), lambda i, j: (i, 0)),
            pl.BlockSpec((None, 1, D_MODEL), lambda i, j: (layer, 0, 0)),
            pl.BlockSpec((None, D_MODEL, tn), lambda i, j: (layer, 0, j)),
        ],
        out_specs=pl.BlockSpec((tm, tn), lambda i, j: (i, j)),
        out_shape=jax.ShapeDtypeStruct((m, PROJ_W), F32),
        scratch_shapes=[pltpu.VMEM((tm, D_MODEL), BF16)],
        compiler_params=pltpu.CompilerParams(
            dimension_semantics=("arbitrary", "arbitrary"), vmem_limit_bytes=VMEM_LIMIT),
        name="inproj",
    )(x, norm_w, w)


def _outproj_kernel(final, mix_ref, x_ref, w_ref, fnw_ref, o_ref):
    y = x_ref[...] + _dot(mix_ref[...].astype(BF16), w_ref[...])
    if final:
        y = _rms_norm(y, fnw_ref[...])
    o_ref[...] = y


def _outproj(mix, x, w, fnw, layer, final, tm):
    m = x.shape[0]
    return pl.pallas_call(
        functools.partial(_outproj_kernel, final),
        grid=(m // tm,),
        in_specs=[
            pl.BlockSpec((tm, D_MODEL), lambda i: (i, 0)),
            pl.BlockSpec((tm, D_MODEL), lambda i: (i, 0)),
            pl.BlockSpec((None, D_MODEL, D_MODEL), lambda i: (layer, 0, 0)),
            pl.BlockSpec((1, D_MODEL), lambda i: (0, 0)),
        ],
        out_specs=pl.BlockSpec((tm, D_MODEL), lambda i: (i, 0)),
        out_shape=jax.ShapeDtypeStruct((m, D_MODEL), F32),
        compiler_params=pltpu.CompilerParams(
            dimension_semantics=("arbitrary",), vmem_limit_bytes=VMEM_LIMIT),
        name="outproj",
    )(mix, x, w, fnw)


def _causal_conv(ext_ref, u, w_ref, width, rows):
    lo = CONV_BASE - (width - 1)
    if rows < CONV_BASE:
        ext_ref[CONV_BASE:CONV_BASE + rows, :] = u
        acc = ext_ref[lo:lo + rows, :] * w_ref[0:1, :]
        for j in range(1, width):
            acc = acc + ext_ref[lo + j:lo + j + rows, :] * w_ref[j:j + 1, :]
        tail = ext_ref[lo + rows:CONV_BASE + rows, :]
        ext_ref[lo:CONV_BASE, :] = tail
        return acc, tail
    prev = ext_ref[0:CONV_BASE, :]
    row = lax.broadcasted_iota(jnp.int32, prev.shape, 0)
    acc = None
    for j in range(width):
        shift = width - 1 - j
        if shift:
            moved = pltpu.roll(u, shift, 0)
            first = jnp.where(row < shift, pltpu.roll(prev, shift, 0), moved[0:CONV_BASE, :])
            moved = jnp.concatenate([first, moved[CONV_BASE:, :]], axis=0)
        else:
            moved = u
        term = moved * w_ref[j:j + 1, :]
        acc = term if acc is None else acc + term
    ext_ref[0:CONV_BASE, :] = u[rows - CONV_BASE:rows, :]
    return acc, ext_ref[lo:CONV_BASE, :]


def _linear_scan(a, b, h0, rows):
    if rows < 8:
        h = h0
        out = []
        for t in range(rows):
            h = a[t:t + 1, :] * h + b[t:t + 1, :]
            out.append(h)
        return jnp.concatenate(out, axis=0)
    row = lax.broadcasted_iota(jnp.int32, a.shape, 0)
    d = 1
    while d < rows:
        keep = row >= d
        a_prev = jnp.where(keep, pltpu.roll(a, d, 0), 1.0)
        b_prev = jnp.where(keep, pltpu.roll(b, d, 0), 0.0)
        b = a * b_prev + b
        a = a * a_prev
        d *= 2
    return b + a * h0


def _cumsum_rows(z, rows):
    if rows < 8:
        out = [z[0:1, :]]
        for t in range(1, rows):
            out.append(out[-1] + z[t:t + 1, :])
        return jnp.concatenate(out, axis=0)
    row = lax.broadcasted_iota(jnp.int32, z.shape, 0)
    d = 1
    while d < rows:
        z = z + jnp.where(row >= d, pltpu.roll(z, d, 0), 0.0)
        d *= 2
    return z


def _load_state(state_in, state_out, ssd_ext, sc_ext, rg_ext):
    c_in, n_in, m_in, s_in, ssdcv_in, sccv_in, rgh_in, rgcv_in = state_in
    c_out, n_out, m_out, s_out, ssdcv_out, sccv_out, rgh_out, rgcv_out = state_out
    c_out[0:1] = c_in[...]
    n_out[0:1] = n_in[...]
    m_out[0:1] = m_in[...]
    s_out[0:1] = s_in[...]
    rgh_out[0:1] = rgh_in[...]
    for g in range(ssd_ext.shape[0]):
        for ext in (ssd_ext, sc_ext, rg_ext):
            ext[g, 0:CONV_BASE, :] = jnp.zeros((CONV_BASE, ext.shape[2]), F32)
        ssd_ext[g, CONV_BASE - (SSD_CONV - 1):CONV_BASE, :] = ssdcv_in[0, g]
        sc_ext[g, CONV_BASE - (SC_WIDTH - 1):CONV_BASE, :] = sccv_in[0, g]
        rg_ext[g, CONV_BASE - (RG_CONV - 1):CONV_BASE, :] = rgcv_in[0, g]
    for out in state_out:
        if out.shape[0] > 1:
            out[1:] = jnp.zeros((out.shape[0] - 1,) + out.shape[1:], out.dtype)


def _gate_prep(L, pcols, bias_ref, alog_ref):
    small = pcols(C_SMALL, LANES) + bias_ref[...]
    lane = lax.broadcasted_iota(jnp.int32, (L, LANES), 1)
    sp = _softplus(small)
    logsig = -_softplus(-small)
    a_coef = -jnp.exp(alog_ref[...])
    z = jnp.where((lane >= 4) & (lane < 8), logsig,
                  jnp.where((lane >= 8) & (lane < 16), sp * a_coef, 0.0))
    rr = lax.broadcasted_iota(jnp.int32, (L, L), 0)
    cc = lax.broadcasted_iota(jnp.int32, (L, L), 1)
    causal = cc <= rr
    cum = _cumsum_rows(z, L)
    packed = jnp.where(lane < 4, small,
                       jnp.where(lane < 16, cum, jnp.where(lane < 24, sp, 0.0)))
    packed_t = packed.T
    return small, sp, cum, packed_t, causal


def _group_mlstm(L, pcols, mix_store, gates, c_out, n_out, m_out, mlnw_ref):
    small, _, cum, packed_t, causal = gates
    m_row = m_out[...]
    m_news = [None] * ML_HEADS

    def head(h):
        q = pcols(C_Q + h * ML_D, ML_D)
        k = pcols(C_K + h * ML_D, ML_D) * (ML_D ** -0.5)
        v = pcols(C_V + h * ML_D, ML_D)
        qb, kb, vb = q.astype(BF16), k.astype(BF16), v.astype(BF16)
        yield
        c_h = c_out[h]
        if L < 8:
            qk = _dot_nt(qb, kb)
            qc = _dot(qb, c_h.astype(BF16))
        yield
        b_col = cum[:, 4 + h:5 + h]
        i_col = small[:, h:h + 1]
        b_row = packed_t[4 + h:5 + h, :]
        i_row = packed_t[h:h + 1, :]
        m_h = m_row[:, 4 + h:5 + h]
        dmat = jnp.where(causal, b_col - b_row + i_row, -jnp.inf)
        inter = b_col + m_h
        yield
        m_t = jnp.maximum(inter, jnp.max(dmat, axis=1, keepdims=True))
        yield
        w_intra = jnp.exp(dmat - m_t)
        w_inter = jnp.exp(inter - m_t)
        yield
        if L >= 8:
            qk = _dot_nt(qb, kb)
        s = qk * w_intra
        n_h = n_out[h:h + 1, :]
        yield
        if L >= 8:
            qc = _dot(qb, c_h.astype(BF16))
        num = _dot(s.astype(BF16), vb) + w_inter * qc
        yield
        den = jnp.sum(s, axis=1, keepdims=True) + w_inter * jnp.sum(q * n_h, axis=1, keepdims=True)
        yield
        hh = num / jnp.maximum(jnp.abs(den), jnp.exp(-m_t))
        yield
        b_last = cum[L - 1:L, 4 + h:5 + h]
        g = b_last - b_col + i_col
        inter_end = b_last + m_h
        m_new = jnp.maximum(inter_end, jnp.max(g, axis=0, keepdims=True))
        yield
        wg = jnp.exp(g - m_new)
        we = jnp.exp(inter_end - m_new)
        kw = k * wg
        yield
        c_out[h] = we * c_h + _dot_tn(kw.astype(BF16), vb)
        yield
        n_out[h:h + 1, :] = we * n_h + jnp.sum(kw, axis=0, keepdims=True)
        m_news[h] = m_new
        yield
        mu = jnp.mean(hh, axis=-1, keepdims=True)
        yield
        hc = hh - mu
        var = jnp.mean(hc * hc, axis=-1, keepdims=True)
        yield
        hn = hc * lax.rsqrt(var + EPS) * mlnw_ref[:, h * ML_D:(h + 1) * ML_D]
        o = pcols(C_O + h * ML_D, ML_D)
        zg = pcols(C_Z + h * ML_D, ML_D)
        mix_store(h * ML_D, jax.nn.sigmoid(o) * hn * _silu(zg))
        yield MAJOR

    for h in range(ML_HEADS):
        yield from head(h)
    lane1 = lax.broadcasted_iota(jnp.int32, (1, LANES), 1)
    for h in range(ML_HEADS):
        m_row = jnp.where(lane1 == 4 + h, m_news[h], m_row)
    m_out[...] = m_row


def _group_ssd(L, pcols, mix_store, gates, s_out, ssdcv_out, ssd_ext, yssd_ref, erep_ref, wrep_ref,
               ssdcw_ref, ssdcb_ref, drow_ref, ssdnw_ref):
    _, sp, cum, packed_t, causal = gates
    conv, tail = _causal_conv(ssd_ext, pcols(C_XBC, SSD_CONV_DIM), ssdcw_ref, SSD_CONV, L)
    ssdcv_out[...] = tail
    yield
    xbc = _silu(conv + ssdcb_ref[...])
    xs_all = xbc[:, 0:BRANCH]
    yield
    rep = SSD_HEADS // SSD_GROUPS
    gw = rep * SSD_P
    cb, bm_b, s_grp, cs = [], [], [], []
    for g in range(SSD_GROUPS):
        bm_g = xbc[:, BRANCH + g * SSD_N:BRANCH + (g + 1) * SSD_N].astype(BF16)
        cm_g = xbc[:, BRANCH + (SSD_GROUPS + g) * SSD_N:BRANCH + (SSD_GROUPS + g + 1) * SSD_N].astype(BF16)
        bm_b.append(bm_g)
        cb.append(_dot_nt(cm_g, bm_g))
        s_g = s_out[g * rep:(g + 1) * rep].reshape(gw, SSD_N)
        s_grp.append(s_g)
        cs.append(_dot_nt(cm_g, s_g.astype(BF16)))
        yield MAJOR if g == SSD_GROUPS - 1 else None

    def head(h):
        g = h // rep
        cols = slice(h * SSD_P, (h + 1) * SSD_P)
        xs_h = xs_all[:, cols]
        acs_col = cum[:, 8 + h:9 + h]
        acs_row = packed_t[8 + h:9 + h, :]
        dt_row = packed_t[16 + h:17 + h, :]
        dt_col = sp[:, 8 + h:9 + h]
        decay = jnp.exp(jnp.where(causal, acs_col - acs_row, -jnp.inf))
        yield
        scores = cb[g] * decay * dt_row
        yield
        yssd_ref[:, cols] = _dot(scores.astype(BF16), xs_h.astype(BF16))
        yield
        erep_ref[:, cols] = jnp.broadcast_to(jnp.exp(acs_col), (L, SSD_P))
        a_last = cum[L - 1:L, 8 + h:9 + h]
        wrep_ref[:, cols] = jnp.broadcast_to(jnp.exp(a_last - acs_col) * dt_col, (L, SSD_P))
        yield MAJOR

    for h in range(SSD_HEADS):
        yield from head(h)
    for g in range(SSD_GROUPS):
        cols = slice(g * gw, (g + 1) * gw)
        xs_g = xs_all[:, cols]
        y_g = yssd_ref[:, cols] + erep_ref[:, cols] * cs[g] + drow_ref[:, cols] * xs_g
        yield
        decay_end = jnp.concatenate(
            [jnp.broadcast_to(jnp.exp(cum[L - 1:L, 8 + h:9 + h]), (SSD_P, SSD_N))
             for h in range(g * rep, (g + 1) * rep)], axis=0)
        upd = _dot_tn((xs_g * wrep_ref[:, cols]).astype(BF16), bm_b[g])
        s_out[g * rep:(g + 1) * rep] = (decay_end * s_grp[g] + upd).reshape(rep, SSD_P, SSD_N)
        yield
        seg = y_g * _silu(pcols(C_SSD_Z + g * gw, gw))
        seg = seg * lax.rsqrt(jnp.mean(seg * seg, axis=-1, keepdims=True) + EPS)
        mix_store(BRANCH + g * gw, seg * ssdnw_ref[:, cols])
        yield MAJOR if g == SSD_GROUPS - 1 else None


def _group_sconv(L, pcols, mix_store, sccv_out, sc_ext, sccw_ref):
    u = pcols(C_SC_C, BRANCH) * pcols(C_SC_H, BRANCH)
    yield
    cu, tail = _causal_conv(sc_ext, u, sccw_ref, SC_WIDTH, L)
    sccv_out[...] = tail
    yield
    mix_store(2 * BRANCH, pcols(C_SC_B, BRANCH) * cu * _silu(pcols(C_SC_Z, BRANCH)))
    yield MAJOR


def _group_rglru(L, pcols, mix_store, rgh_out, rgcv_out, rg_ext,
                 rgcw_ref, rgcb_ref, rgw_ref, rgba_ref, rgbx_ref, rglam_ref):
    conv, tail = _causal_conv(rg_ext, pcols(C_RG_X, BRANCH), rgcw_ref, RG_CONV, L)
    rgcv_out[...] = tail
    xr = conv + rgcb_ref[...]
    yield
    pre = _dot(xr.astype(BF16), rgw_ref[...])
    yield
    r = jax.nn.sigmoid(pre[:, 0:BRANCH] + rgba_ref[...])
    ig = jax.nn.sigmoid(pre[:, BRANCH:2 * BRANCH] + rgbx_ref[...])
    yield
    log_a = (-RG_C * r) * _softplus(-rglam_ref[...])
    a = jnp.exp(log_a)
    bterm = jnp.sqrt(-jnp.tanh(log_a) * (a * a + 1.0)) * (ig * xr)
    yield MAJOR
    h_rg = _linear_scan(a, bterm, rgh_out[...], L)
    rgh_out[...] = h_rg[L - 1:L, :]
    yield
    mix_store(3 * BRANCH, h_rg * _silu(pcols(C_RG_Z, BRANCH)))
    yield MAJOR


def _split_refs(refs, n_lead, n_carried, n_out_lead):
    lead = refs[:n_lead]
    state_in = refs[n_lead:n_lead + N_STATES]
    params = refs[n_lead + N_STATES:n_lead + N_STATES + N_PARAMS]
    rest = refs[n_lead + N_STATES + N_PARAMS + n_carried:]
    out_lead = rest[:n_out_lead]
    state_out = rest[n_out_lead:n_out_lead + N_STATES]
    scratch = rest[n_out_lead + N_STATES:]
    return lead, state_in, params, out_lead, state_out, scratch


def _seq_groups(L, seq, pcols, mix_store, state_out, scratch, params):
    (bias_ref, alog_ref, mlnw_ref, ssdcw_ref, ssdcb_ref, drow_ref, ssdnw_ref, sccw_ref, rgcw_ref,
     rgcb_ref, rgw_ref, rgba_ref, rgbx_ref, rglam_ref) = params
    c_out, n_out, m_out, s_out, ssdcv_out, sccv_out, rgh_out, rgcv_out = [r.at[0, seq] for r in state_out]
    ssd_ext, sc_ext, rg_ext, yssd_ref, erep_ref, wrep_ref = [r.at[seq] for r in scratch]
    gates = _gate_prep(L, pcols, bias_ref, alog_ref)
    return [
        _group_mlstm(L, pcols, mix_store, gates, c_out, n_out, m_out, mlnw_ref),
        _group_ssd(L, pcols, mix_store, gates, s_out, ssdcv_out, ssd_ext, yssd_ref, erep_ref, wrep_ref,
                   ssdcw_ref, ssdcb_ref, drow_ref, ssdnw_ref),
        _group_sconv(L, pcols, mix_store, sccv_out, sc_ext, sccw_ref),
        _group_rglru(L, pcols, mix_store, rgh_out, rgcv_out, rg_ext,
                     rgcw_ref, rgcb_ref, rgw_ref, rgba_ref, rgbx_ref, rglam_ref),
    ]


def _mixer_kernel(rows, nseq, n_carried, *refs):
    (proj_ref,), state_in, params, (mix_ref,), state_out, scratch = _split_refs(refs, 1, n_carried, 1)

    @pl.when(pl.program_id(1) == 0)
    def _():
        _load_state(state_in, state_out, *scratch[:3])

    def chain(seq):
        lo, hi = seq * rows, (seq + 1) * rows

        def pcols(start, width):
            return proj_ref[lo:hi, start:start + width]

        def mix_store(start, val):
            mix_ref[lo:hi, start:start + val.shape[1]] = val.astype(mix_ref.dtype)

        for group in _seq_groups(rows, seq, pcols, mix_store, state_out, scratch, params):
            yield from group

    chains = [chain(seq) for seq in range(nseq)]
    done = object()
    while chains:
        chains = [c for c in chains if next(c, done) is not done]


def _state_specs(state_layer, layer, carried, nseq):
    def at_layer(lyr, blk):
        blk = (blk[0], nseq) + blk[2:]
        tail = (0,) * (len(blk) - 2)
        return pl.BlockSpec(blk, lambda b, c: (lyr, b) + tail)

    in_specs = [at_layer(state_layer, blk) for blk in STATE_BLOCKS]
    if carried:
        out_specs = [at_layer(layer, blk) for blk in STATE_BLOCKS]
    else:
        assert layer == 0
        out_specs = [at_layer(0, (DEPTH,) + blk[1:]) for blk in STATE_BLOCKS]
    return in_specs, out_specs


def _state_shapes(nb):
    return [jax.ShapeDtypeStruct((DEPTH, nb) + blk[2:], F32) for blk in STATE_BLOCKS]


def _mix_scratch(rows, nseq):
    return [
        pltpu.VMEM((nseq, CONV_BASE + rows, SSD_CONV_DIM), F32),
        pltpu.VMEM((nseq, CONV_BASE + rows, BRANCH), F32),
        pltpu.VMEM((nseq, CONV_BASE + rows, BRANCH), F32),
        pltpu.VMEM((nseq, rows, BRANCH), F32),
        pltpu.VMEM((nseq, rows, BRANCH), F32),
        pltpu.VMEM((nseq, rows, BRANCH), F32),
    ]


def _mixer(proj, states, state_layer, params, layer, carried, rows, nseq, mix_dtype):
    nb = proj.shape[0] // rows
    carried = list(carried) if carried is not None else []
    st_in, st_out = _state_specs(state_layer, layer, carried, nseq)
    in_specs = [pl.BlockSpec((nseq * rows, PROJ_W), lambda b, c: (b, 0))] + st_in
    in_specs += [pl.BlockSpec((None,) + p.shape[1:], lambda b, c: (layer, 0, 0)) for p in params]
    in_specs += [pl.BlockSpec(memory_space=pl.ANY) for _ in carried]
    first_carried = 1 + N_STATES + N_PARAMS
    outs = pl.pallas_call(
        functools.partial(_mixer_kernel, rows, nseq, len(carried)),
        grid=(nb // nseq, 1),
        in_specs=in_specs,
        out_specs=[pl.BlockSpec((nseq * rows, D_MODEL), lambda b, c: (b, 0))] + st_out,
        out_shape=[jax.ShapeDtypeStruct((nb * rows, D_MODEL), mix_dtype)] + _state_shapes(nb),
        input_output_aliases={first_carried + i: 1 + i for i in range(len(carried))},
        scratch_shapes=_mix_scratch(rows, nseq),
        compiler_params=pltpu.CompilerParams(
            dimension_semantics=("arbitrary", "arbitrary"), vmem_limit_bytes=VMEM_LIMIT),
        name=f"mixer_rows{rows}",
    )(proj, *states, *params, *carried)
    return outs[0], list(outs[1:])


def _layer_kernel(final, n_carried, *refs):
    lead, state_in, params, (y_ref,), state_out, scratch = _split_refs(refs, 6, n_carried, 1)
    x_ref, xnext_ref, nw_ref, win_ref, wout_ref, fnw_ref = lead
    proj_ref, mix_ref = scratch[:2]
    mix_scratch = scratch[2:]
    step = pl.program_id(0) * pl.num_programs(1) + pl.program_id(1)

    @pl.when(pl.program_id(1) == 0)
    def _():
        _load_state(state_in, state_out, *mix_scratch[:3])

    @pl.when(step == 0)
    def _():
        xn0 = _rms_norm(x_ref[0], nw_ref[...]).astype(BF16)
        for start in range(0, PROJ_W, PROJ_TILE):
            stop = min(start + PROJ_TILE, PROJ_W)
            proj_ref[0, :, start:stop] = _dot(xn0, win_ref[:, start:stop])

    tiles = [(s, min(s + PROJ_TILE, PROJ_W)) for s in range(0, PROJ_W, PROJ_TILE)]
    n_pieces = 18

    def chunk_step(cur, nxt):
        x = x_ref[0]
        xn_next = _rms_norm(xnext_ref[0], nw_ref[...]).astype(BF16)

        def project_some(piece):
            lo = (piece * len(tiles)) // n_pieces
            hi = ((piece + 1) * len(tiles)) // n_pieces
            for start, stop in tiles[lo:hi]:
                proj_ref[nxt, :, start:stop] = _dot(xn_next, win_ref[:, start:stop])

        def pcols(start, width):
            return proj_ref[cur, :, start:start + width]

        def mix_store(start, val):
            mix_ref[:, start:start + val.shape[1]] = val.astype(BF16)

        def out_project(group):
            if group % 2 == 0:
                return 0.0
            lo, hi = (group - 1) * BRANCH, (group + 1) * BRANCH
            return _dot(mix_ref[:, lo:hi], wout_ref[lo:hi, :])

        project_some(0)
        groups = _seq_groups(CHUNK, 0, pcols, mix_store, state_out, mix_scratch, params)
        piece = 1
        y = x
        for gi, group in enumerate(groups):
            for mark in group:
                if mark is MAJOR:
                    project_some(piece)
                    piece += 1
            y = y + out_project(gi)
        assert piece == n_pieces, piece
        if final:
            y = _rms_norm(y, fnw_ref[...])
        y_ref[0] = y

    for parity in range(2):
        pl.when(step % 2 == parity)(functools.partial(chunk_step, parity, 1 - parity))


def _layer_prompt(x, norm_w, w_in, w_out, fnw, states, params, layer, carried, final):
    nb, t, _ = x.shape
    nc = t // CHUNK
    carried = list(carried) if carried is not None else []
    st_in, st_out = _state_specs(0, layer, carried, 1)
    resident = dict(pipeline_mode=pl.Buffered(1))

    def next_chunk(b, c):
        step = jnp.minimum(b * nc + c + 1, nb * nc - 1)
        return (step // nc, step % nc, 0)

    in_specs = [
        pl.BlockSpec((1, CHUNK, D_MODEL), lambda b, c: (b, c, 0)),
        pl.BlockSpec((1, CHUNK, D_MODEL), next_chunk),
        pl.BlockSpec((None, 1, D_MODEL), lambda b, c: (layer, 0, 0)),
        pl.BlockSpec((None, D_MODEL, PROJ_W), lambda b, c: (layer, 0, 0), **resident),
        pl.BlockSpec((None, D_MODEL, D_MODEL), lambda b, c: (layer, 0, 0), **resident),
        pl.BlockSpec((1, D_MODEL), lambda b, c: (0, 0)),
    ] + st_in
    in_specs += [pl.BlockSpec((None,) + p.shape[1:], lambda b, c: (layer, 0, 0)) for p in params]
    in_specs += [pl.BlockSpec(memory_space=pl.ANY) for _ in carried]
    first_carried = 6 + N_STATES + N_PARAMS
    outs = pl.pallas_call(
        functools.partial(_layer_kernel, final, len(carried)),
        grid=(nb, nc),
        in_specs=in_specs,
        out_specs=[pl.BlockSpec((1, CHUNK, D_MODEL), lambda b, c: (b, c, 0))] + st_out,
        out_shape=[jax.ShapeDtypeStruct((nb, t, D_MODEL), F32)] + _state_shapes(nb),
        input_output_aliases={first_carried + i: 1 + i for i in range(len(carried))},
        scratch_shapes=[pltpu.VMEM((2, CHUNK, PROJ_W), F32), pltpu.VMEM((CHUNK, D_MODEL), BF16)]
        + _mix_scratch(CHUNK, 1),
        compiler_params=pltpu.CompilerParams(
            dimension_semantics=("arbitrary", "arbitrary"), vmem_limit_bytes=VMEM_LIMIT),
        name="layer_prompt",
    )(x, x, norm_w, w_in, w_out, fnw, *states, *params, *carried)
    return outs[0], list(outs[1:])


O_I, O_SSD_Z, O_XBC, O_DT, O_SC, O_RG, O_END = 2560, 2568, 3080, 4104, 4112, 6160, 7184
REORDER_TILE = 512
GATE_TILE = C_SMALL // REORDER_TILE


def _reorder_kernel(src_ref, if_ref, dt_ref, o_ref):
    j = pl.program_id(1)

    @pl.when(j < GATE_TILE)
    def _():
        o_ref[...] = src_ref[0].T.astype(BF16)

    @pl.when(j == GATE_TILE)
    def _():
        zeros = jnp.zeros((REORDER_TILE - 24, D_MODEL), F32)
        small = jnp.concatenate([if_ref[0], dt_ref[0], dt_ref[0], zeros], axis=0)
        o_ref[...] = small.T.astype(BF16)


def _reorder_w_in(w_in):
    nl, d, n = w_in.shape
    assert n == O_END and d == D_MODEL
    assert C_SSD_Z % REORDER_TILE == 0 and C_SC_B % REORDER_TILE == 0 and C_SMALL % REORDER_TILE == 0
    w_t = jnp.swapaxes(w_in, 1, 2)

    def src_row(l, j):
        t_ssd, t_sc = C_SSD_Z // REORDER_TILE, C_SC_B // REORDER_TILE
        row = jnp.where(j < t_ssd, j * REORDER_TILE,
                        jnp.where(j < t_sc, O_SSD_Z + (j - t_ssd) * REORDER_TILE,
                                  O_SC + (j - t_sc) * REORDER_TILE))
        row = jnp.where(j < GATE_TILE, row, 0)
        return (l, pl.multiple_of(row, 8), 0)

    return pl.pallas_call(
        _reorder_kernel,
        grid=(nl, GATE_TILE + 1),
        in_specs=[
            pl.BlockSpec((pl.Element(1), pl.Element(REORDER_TILE), pl.Element(d)), src_row),
            pl.BlockSpec((pl.Element(1), pl.Element(8), pl.Element(d)), lambda l, j: (l, O_I, 0)),
            pl.BlockSpec((pl.Element(1), pl.Element(8), pl.Element(d)), lambda l, j: (l, O_DT, 0)),
        ],
        out_specs=pl.BlockSpec((None, d, REORDER_TILE), lambda l, j: (l, 0, j)),
        out_shape=jax.ShapeDtypeStruct((nl, d, PROJ_W), BF16),
        compiler_params=pltpu.CompilerParams(
            dimension_semantics=("arbitrary", "arbitrary"), vmem_limit_bytes=VMEM_LIMIT),
        name="reorder_w_in",
    )(w_t, w_t, w_t)


def _block_diag(w):
    nl, nblk, d, e = w.shape
    eye = jnp.eye(nblk, dtype=w.dtype)
    return jnp.einsum('lnde,nm->lndme', w, eye).reshape(nl, nblk * d, nblk * e)


def _lane_row(parts, width=LANES):
    nl = parts[0][1].shape[0]
    row = jnp.zeros((nl, width), F32)
    for off, p in parts:
        row = lax.dynamic_update_slice(row, p.astype(F32), (0, off))
    return row[:, None, :]


def kernel(x_prompt, x_sample, state_mlstm_C, state_mlstm_n, state_mlstm_m, state_ssd, state_ssd_conv, state_sconv_conv, state_rglru_h, state_rglru_conv, norm_w, w_in, ml_i_bias, ml_f_bias, ml_norm_w, ssd_conv_w, ssd_conv_b, ssd_dt_bias, ssd_A_log, ssd_D, ssd_norm_w, sc_conv_w, rg_conv_w, rg_conv_b, rg_wa, rg_ba, rg_wx, rg_bx, rg_lambda, w_out, final_norm_w):
    bp, tp, _ = x_prompt.shape
    bs, ts, _ = x_sample.shape

    w_in_r = _reorder_w_in(w_in)
    w_out_b = w_out.astype(BF16)
    rg_w = jnp.concatenate([_block_diag(rg_wa), _block_diag(rg_wx)], axis=-1).astype(BF16)
    bias_row = _lane_row([(0, ml_i_bias), (4, ml_f_bias), (8, ssd_dt_bias), (16, ssd_dt_bias)])
    alog_row = _lane_row([(8, ssd_A_log)])
    d_row = jnp.repeat(ssd_D, SSD_P, axis=-1)[:, None, :]
    row = lambda p: p[:, None, :]
    params = [bias_row, alog_row, row(ml_norm_w), ssd_conv_w, row(ssd_conv_b), d_row, row(ssd_norm_w),
              sc_conv_w, rg_conv_w, row(rg_conv_b), rg_w, row(rg_ba), row(rg_bx), row(rg_lambda)]
    norm_rows = row(norm_w)

    def pad_m(m):
        return jnp.pad(m, ((0, 0), (0, 0), (4, LANES - 4 - ML_HEADS)))[:, :, None, :]

    sample_states = [state_mlstm_C, state_mlstm_n, pad_m(state_mlstm_m), state_ssd, state_ssd_conv,
                     state_sconv_conv, state_rglru_h[:, :, None, :], state_rglru_conv]
    prompt_states = [
        jnp.zeros((1, bp, ML_HEADS, ML_D, ML_D), F32),
        jnp.zeros((1, bp, ML_HEADS, ML_D), F32),
        pad_m(jnp.full((1, bp, ML_HEADS), NEG_BIG, F32)),
        jnp.zeros((1, bp, SSD_HEADS, SSD_P, SSD_N), F32),
        jnp.zeros((1, bp, SSD_CONV - 1, SSD_CONV_DIM), F32),
        jnp.zeros((1, bp, SC_WIDTH - 1, BRANCH), F32),
        jnp.zeros((1, bp, 1, BRANCH), F32),
        jnp.zeros((1, bp, RG_CONV - 1, BRANCH), F32),
    ]

    xp = x_prompt
    xs = x_sample.reshape(bs * ts, D_MODEL)
    fnw = final_norm_w[None, :]
    st_p, st_s = None, None
    for l in range(DEPTH):
        final = l == DEPTH - 1
        xp, st_p = _layer_prompt(xp, norm_rows, w_in_r, w_out_b, fnw, prompt_states, params, l, st_p, final)
        proj_s = _inproj(xs, norm_rows, w_in_r, l, 512)
        mix_s, st_s = _mixer(proj_s, sample_states, l, params, l, st_s, ts, SAMPLE_SEQS, F32)
        xs = _outproj(mix_s, xs, w_out_b, fnw, l, final, 512)

    def unpack(states):
        states = list(states)
        states[2] = states[2][:, :, 0, 4:4 + ML_HEADS]
        states[6] = states[6][:, :, 0, :]
        return tuple(states)

    return (xp, xs.reshape(bs, ts, D_MODEL)) + unpack(st_p) + unpack(st_s)
```

```python
import functools

import jax
import jax.numpy as jnp
from jax import lax
from jax.experimental import pallas as pl
from jax.experimental.pallas import tpu as pltpu

F32 = jnp.float32
BF16 = jnp.bfloat16

D_MODEL = 2048
DEPTH = 2
BRANCH = 512
CHUNK = 128
EPS = 1e-6
ML_HEADS = 4
ML_D = 128
NEG_BIG = -1e30
SSD_HEADS = 8
SSD_P = 64
SSD_GROUPS = 2
SSD_N = 128
SSD_CONV = 4
SSD_CONV_DIM = 1024
SC_WIDTH = 3
RG_CONV = 4
RG_C = 8.0

C_Q, C_K, C_V, C_O, C_Z = 0, 512, 1024, 1536, 2048
C_SSD_Z, C_XBC = 2560, 3072
C_SC_B, C_SC_C, C_SC_H, C_SC_Z = 4096, 4608, 5120, 5632
C_RG_X, C_RG_Z = 6144, 6656
C_SMALL = 7168
PROJ_W = 7296
PROJ_TILE = 256
MAJOR = "major"
LANES = 128
CONV_BASE = 8
SAMPLE_SEQS = 8

V7X_VMEM_BYTES = 64 * 1024 * 1024
VMEM_LIMIT = V7X_VMEM_BYTES - 4 * 1024 * 1024

NT_DIMS = (((1,), (1,)), ((), ()))
TN_DIMS = (((0,), (0,)), ((), ()))

N_STATES = 8
N_PARAMS = 14
STATE_BLOCKS = [
    (1, 1, ML_HEADS, ML_D, ML_D), (1, 1, ML_HEADS, ML_D), (1, 1, 1, LANES),
    (1, 1, SSD_HEADS, SSD_P, SSD_N), (1, 1, SSD_CONV - 1, SSD_CONV_DIM),
    (1, 1, SC_WIDTH - 1, BRANCH), (1, 1, 1, BRANCH), (1, 1, RG_CONV - 1, BRANCH),
]


def _softplus(x):
    return jnp.maximum(x, 0.0) + jnp.log1p(jnp.exp(-jnp.abs(x)))


def _silu(x):
    return x * jax.nn.sigmoid(x)


def _dot(a, b):
    return jnp.dot(a, b, preferred_element_type=F32)


def _dot_nt(a, b):
    return lax.dot_general(a, b, NT_DIMS, preferred_element_type=F32)


def _dot_tn(a, b):
    return lax.dot_general(a, b, TN_DIMS, preferred_element_type=F32)


def _rms_norm(x, w):
    return x * lax.rsqrt(jnp.mean(x * x, axis=-1, keepdims=True) + EPS) * w


def _inproj_kernel(x_ref, nw_ref, w_ref, o_ref, xn_ref):
    @pl.when(pl.program_id(1) == 0)
    def _():
        xn_ref[...] = _rms_norm(x_ref[...], nw_ref[...]).astype(BF16)

    o_ref[...] = _dot(xn_ref[...], w_ref[...])


def _inproj(x, norm_w, w, layer, tm):
    m = x.shape[0]
    tn = PROJ_W // 3
    return pl.pallas_call(
        _inproj_kernel,
        grid=(m // tm, PROJ_W // tn),
        in_specs=[
            pl.BlockSpec((tm, D_MODEL), lambda i, j: (i, 0)),
            pl.BlockSpec((None, 1, D_MODEL), lambda i, j: (layer, 0, 0)),
            pl.BlockSpec((None, D_MODEL, tn), lambda i, j: (layer, 0, j)),
        ],
        out_specs=pl.BlockSpec((tm, tn), lambda i, j: (i, j)),
        out_shape=jax.ShapeDtypeStruct((m, PROJ_W), F32),
        scratch_shapes=[pltpu.VMEM((tm, D_MODEL), BF16)],
        compiler_params=pltpu.CompilerParams(
            dimension_semantics=("arbitrary", "arbitrary"), vmem_limit_bytes=VMEM_LIMIT),
        name="inproj",
    )(x, norm_w, w)


def _outproj_kernel(final, mix_ref, x_ref, w_ref, fnw_ref, o_ref):
    y = x_ref[...] + _dot(mix_ref[...].astype(BF16), w_ref[...])
    if final:
        y = _rms_norm(y, fnw_ref[...])
    o_ref[...] = y


def _outproj(mix, x, w, fnw, layer, final, tm):
    m = x.shape[0]
    return pl.pallas_call(
        functools.partial(_outproj_kernel, final),
        grid=(m // tm,),
        in_specs=[
            pl.BlockSpec((tm, D_MODEL), lambda i: (i, 0)),
            pl.BlockSpec((tm, D_MODEL), lambda i: (i, 0)),
            pl.BlockSpec((None, D_MODEL, D_MODEL), lambda i: (layer, 0, 0)),
            pl.BlockSpec((1, D_MODEL), lambda i: (0, 0)),
        ],
        out_specs=pl.BlockSpec((tm, D_MODEL), lambda i: (i, 0)),
        out_shape=jax.ShapeDtypeStruct((m, D_MODEL), F32),
        compiler_params=pltpu.CompilerParams(
            dimension_semantics=("arbitrary",), vmem_limit_bytes=VMEM_LIMIT),
        name="outproj",
    )(mix, x, w, fnw)


def _causal_conv(ext_ref, u, w_ref, width, rows):
    lo = CONV_BASE - (width - 1)
    if rows < CONV_BASE:
        ext_ref[CONV_BASE:CONV_BASE + rows, :] = u
        acc = ext_ref[lo:lo + rows, :] * w_ref[0:1, :]
        for j in range(1, width):
            acc = acc + ext_ref[lo + j:lo + j + rows, :] * w_ref[j:j + 1, :]
        tail = ext_ref[lo + rows:CONV_BASE + rows, :]
        ext_ref[lo:CONV_BASE, :] = tail
        return acc, tail
    prev = ext_ref[0:CONV_BASE, :]
    row = lax.broadcasted_iota(jnp.int32, prev.shape, 0)
    acc = None
    for j in range(width):
        shift = width - 1 - j
        if shift:
            moved = pltpu.roll(u, shift, 0)
            first = jnp.where(row < shift, pltpu.roll(prev, shift, 0), moved[0:CONV_BASE, :])
            moved = jnp.concatenate([first, moved[CONV_BASE:, :]], axis=0)
        else:
            moved = u
        term = moved * w_ref[j:j + 1, :]
        acc = term if acc is None else acc + term
    ext_ref[0:CONV_BASE, :] = u[rows - CONV_BASE:rows, :]
    return acc, ext_ref[lo:CONV_BASE, :]


def _linear_scan(a, b, h0, rows):
    if rows < 8:
        h = h0
        out = []
        for t in range(rows):
            h = a[t:t + 1, :] * h + b[t:t + 1, :]
            out.append(h)
        return jnp.concatenate(out, axis=0)
    row = lax.broadcasted_iota(jnp.int32, a.shape, 0)
    d = 1
    while d < rows:
        keep = row >= d
        a_prev = jnp.where(keep, pltpu.roll(a, d, 0), 1.0)
        b_prev = jnp.where(keep, pltpu.roll(b, d, 0), 0.0)
        b = a * b_prev + b
        a = a * a_prev
        d *= 2
    return b + a * h0


def _cumsum_rows(z, rows):
    if rows < 8:
        out = [z[0:1, :]]
        for t in range(1, rows):
            out.append(out[-1] + z[t:t + 1, :])
        return jnp.concatenate(out, axis=0)
    row = lax.broadcasted_iota(jnp.int32, z.shape, 0)
    d = 1
    while d < rows:
        z = z + jnp.where(row >= d, pltpu.roll(z, d, 0), 0.0)
        d *= 2
    return z


def _load_state(state_in, state_out, ssd_ext, sc_ext, rg_ext):
    c_in, n_in, m_in, s_in, ssdcv_in, sccv_in, rgh_in, rgcv_in = state_in
    c_out, n_out, m_out, s_out, ssdcv_out, sccv_out, rgh_out, rgcv_out = state_out
    c_out[0:1] = c_in[...]
    n_out[0:1] = n_in[...]
    m_out[0:1] = m_in[...]
    s_out[0:1] = s_in[...]
    rgh_out[0:1] = rgh_in[...]
    for g in range(ssd_ext.shape[0]):
        for ext in (ssd_ext, sc_ext, rg_ext):
            ext[g, 0:CONV_BASE, :] = jnp.zeros((CONV_BASE, ext.shape[2]), F32)
        ssd_ext[g, CONV_BASE - (SSD_CONV - 1):CONV_BASE, :] = ssdcv_in[0, g]
        sc_ext[g, CONV_BASE - (SC_WIDTH - 1):CONV_BASE, :] = sccv_in[0, g]
        rg_ext[g, CONV_BASE - (RG_CONV - 1):CONV_BASE, :] = rgcv_in[0, g]
    for out in state_out:
        if out.shape[0] > 1:
            out[1:] = jnp.zeros((out.shape[0] - 1,) + out.shape[1:], out.dtype)


def _gate_prep(L, pcols, bias_ref, alog_ref):
    small = pcols(C_SMALL, LANES) + bias_ref[...]
    lane = lax.broadcasted_iota(jnp.int32, (L, LANES), 1)
    sp = _softplus(small)
    logsig = -_softplus(-small)
    a_coef = -jnp.exp(alog_ref[...])
    z = jnp.where((lane >= 4) & (lane < 8), logsig,
                  jnp.where((lane >= 8) & (lane < 16), sp * a_coef, 0.0))
    rr = lax.broadcasted_iota(jnp.int32, (L, L), 0)
    cc = lax.broadcasted_iota(jnp.int32, (L, L), 1)
    causal = cc <= rr
    cum = _cumsum_rows(z, L)
    packed = jnp.where(lane < 4, small,
                       jnp.where(lane < 16, cum, jnp.where(lane < 24, sp, 0.0)))
    packed_t = packed.T
    return small, sp, cum, packed_t, causal


def _group_mlstm(L, pcols, mix_store, gates, c_out, n_out, m_out, mlnw_ref):
    small, _, cum, packed_t, causal = gates
    m_row = m_out[...]
    m_news = [None] * ML_HEADS

    def head(h):
        q = pcols(C_Q + h * ML_D, ML_D)
        k = pcols(C_K + h * ML_D, ML_D) * (ML_D ** -0.5)
        v = pcols(C_V + h * ML_D, ML_D)
        qb, kb, vb = q.astype(BF16), k.astype(BF16), v.astype(BF16)
        yield
        c_h = c_out[h]
        qk = _dot_nt(qb, kb)
        qc = _dot(qb, c_h.astype(BF16))
        yield
        b_col = cum[:, 4 + h:5 + h]
        i_col = small[:, h:h + 1]
        b_row = packed_t[4 + h:5 + h, :]
        i_row = packed_t[h:h + 1, :]
        m_h = m_row[:, 4 + h:5 + h]
        dmat = jnp.where(causal, b_col - b_row + i_row, -jnp.inf)
        inter = b_col + m_h
        yield
        m_t = jnp.maximum(inter, jnp.max(dmat, axis=1, keepdims=True))
        yield
        w_intra = jnp.exp(dmat - m_t)
        w_inter = jnp.exp(inter - m_t)
        yield
        s = qk * w_intra
        n_h = n_out[h:h + 1, :]
        yield
        num = _dot(s.astype(BF16), vb) + w_inter * qc
        yield
        den = jnp.sum(s, axis=1, keepdims=True) + w_inter * jnp.sum(q * n_h, axis=1, keepdims=True)
        yield
        hh = num / jnp.maximum(jnp.abs(den), jnp.exp(-m_t))
        yield
        b_last = cum[L - 1:L, 4 + h:5 + h]
        g = b_last - b_col + i_col
        inter_end = b_last + m_h
        m_new = jnp.maximum(inter_end, jnp.max(g, axis=0, keepdims=True))
        yield
        wg = jnp.exp(g - m_new)
        we = jnp.exp(inter_end - m_new)
        kw = k * wg
        yield
        c_out[h] = we * c_h + _dot_tn(kw.astype(BF16), vb)
        yield
        n_out[h:h + 1, :] = we * n_h + jnp.sum(kw, axis=0, keepdims=True)
        m_news[h] = m_new
        yield
        mu = jnp.mean(hh, axis=-1, keepdims=True)
        yield
        hc = hh - mu
        var = jnp.mean(hc * hc, axis=-1, keepdims=True)
        yield
        hn = hc * lax.rsqrt(var + EPS) * mlnw_ref[:, h * ML_D:(h + 1) * ML_D]
        o = pcols(C_O + h * ML_D, ML_D)
        zg = pcols(C_Z + h * ML_D, ML_D)
        mix_store(h * ML_D, jax.nn.sigmoid(o) * hn * _silu(zg))
        yield MAJOR

    for h in range(ML_HEADS):
        yield from head(h)
    lane1 = lax.broadcasted_iota(jnp.int32, (1, LANES), 1)
    for h in range(ML_HEADS):
        m_row = jnp.where(lane1 == 4 + h, m_news[h], m_row)
    m_out[...] = m_row


def _group_ssd(L, pcols, mix_store, gates, s_out, ssdcv_out, ssd_ext, yssd_ref, erep_ref, wrep_ref,
               ssdcw_ref, ssdcb_ref, drow_ref, ssdnw_ref):
    _, sp, cum, packed_t, causal = gates
    conv, tail = _causal_conv(ssd_ext, pcols(C_XBC, SSD_CONV_DIM), ssdcw_ref, SSD_CONV, L)
    ssdcv_out[...] = tail
    yield
    xbc = _silu(conv + ssdcb_ref[...])
    xs_all = xbc[:, 0:BRANCH]
    yield
    rep = SSD_HEADS // SSD_GROUPS
    gw = rep * SSD_P
    cb, bm_b, s_grp, cs = [], [], [], []
    for g in range(SSD_GROUPS):
        bm_g = xbc[:, BRANCH + g * SSD_N:BRANCH + (g + 1) * SSD_N].astype(BF16)
        cm_g = xbc[:, BRANCH + (SSD_GROUPS + g) * SSD_N:BRANCH + (SSD_GROUPS + g + 1) * SSD_N].astype(BF16)
        bm_b.append(bm_g)
        cb.append(_dot_nt(cm_g, bm_g))
        s_g = s_out[g * rep:(g + 1) * rep].reshape(gw, SSD_N)
        s_grp.append(s_g)
        cs.append(_dot_nt(cm_g, s_g.astype(BF16)))
        yield MAJOR if g == SSD_GROUPS - 1 else None

    def head(h):
        g = h // rep
        cols = slice(h * SSD_P, (h + 1) * SSD_P)
        xs_h = xs_all[:, cols]
        acs_col = cum[:, 8 + h:9 + h]
        acs_row = packed_t[8 + h:9 + h, :]
        dt_row = packed_t[16 + h:17 + h, :]
        dt_col = sp[:, 8 + h:9 + h]
        decay = jnp.exp(jnp.where(causal, acs_col - acs_row, -jnp.inf))
        yield
        scores = cb[g] * decay * dt_row
        yield
        yssd_ref[:, cols] = _dot(scores.astype(BF16), xs_h.astype(BF16))
        yield
        erep_ref[:, cols] = jnp.broadcast_to(jnp.exp(acs_col), (L, SSD_P))
        a_last = cum[L - 1:L, 8 + h:9 + h]
        wrep_ref[:, cols] = jnp.broadcast_to(jnp.exp(a_last - acs_col) * dt_col, (L, SSD_P))
        yield MAJOR

    for h in range(SSD_HEADS):
        yield from head(h)
    for g in range(SSD_GROUPS):
        cols = slice(g * gw, (g + 1) * gw)
        xs_g = xs_all[:, cols]
        y_g = yssd_ref[:, cols] + erep_ref[:, cols] * cs[g] + drow_ref[:, cols] * xs_g
        yield
        decay_end = jnp.concatenate(
            [jnp.broadcast_to(jnp.exp(cum[L - 1:L, 8 + h:9 + h]), (SSD_P, SSD_N))
             for h in range(g * rep, (g + 1) * rep)], axis=0)
        upd = _dot_tn((xs_g * wrep_ref[:, cols]).astype(BF16), bm_b[g])
        s_out[g * rep:(g + 1) * rep] = (decay_end * s_grp[g] + upd).reshape(rep, SSD_P, SSD_N)
        yield
        seg = y_g * _silu(pcols(C_SSD_Z + g * gw, gw))
        seg = seg * lax.rsqrt(jnp.mean(seg * seg, axis=-1, keepdims=True) + EPS)
        mix_store(BRANCH + g * gw, seg * ssdnw_ref[:, cols])
        yield MAJOR if g == SSD_GROUPS - 1 else None


def _group_sconv(L, pcols, mix_store, sccv_out, sc_ext, sccw_ref):
    u = pcols(C_SC_C, BRANCH) * pcols(C_SC_H, BRANCH)
    yield
    cu, tail = _causal_conv(sc_ext, u, sccw_ref, SC_WIDTH, L)
    sccv_out[...] = tail
    yield
    mix_store(2 * BRANCH, pcols(C_SC_B, BRANCH) * cu * _silu(pcols(C_SC_Z, BRANCH)))
    yield MAJOR


def _group_rglru(L, pcols, mix_store, rgh_out, rgcv_out, rg_ext,
                 rgcw_ref, rgcb_ref, rgw_ref, rgba_ref, rgbx_ref, rglam_ref):
    conv, tail = _causal_conv(rg_ext, pcols(C_RG_X, BRANCH), rgcw_ref, RG_CONV, L)
    rgcv_out[...] = tail
    xr = conv + rgcb_ref[...]
    yield
    pre = _dot(xr.astype(BF16), rgw_ref[...])
    yield
    r = jax.nn.sigmoid(pre[:, 0:BRANCH] + rgba_ref[...])
    ig = jax.nn.sigmoid(pre[:, BRANCH:2 * BRANCH] + rgbx_ref[...])
    yield
    log_a = (-RG_C * r) * _softplus(-rglam_ref[...])
    a = jnp.exp(log_a)
    bterm = jnp.sqrt(-jnp.tanh(log_a) * (a * a + 1.0)) * (ig * xr)
    yield MAJOR
    h_rg = _linear_scan(a, bterm, rgh_out[...], L)
    rgh_out[...] = h_rg[L - 1:L, :]
    yield
    mix_store(3 * BRANCH, h_rg * _silu(pcols(C_RG_Z, BRANCH)))
    yield MAJOR


def _split_refs(refs, n_lead, n_carried, n_out_lead):
    lead = refs[:n_lead]
    state_in = refs[n_lead:n_lead + N_STATES]
    params = refs[n_lead + N_STATES:n_lead + N_STATES + N_PARAMS]
    rest = refs[n_lead + N_STATES + N_PARAMS + n_carried:]
    out_lead = rest[:n_out_lead]
    state_out = rest[n_out_lead:n_out_lead + N_STATES]
    scratch = rest[n_out_lead + N_STATES:]
    return lead, state_in, params, out_lead, state_out, scratch


def _seq_groups(L, seq, pcols, mix_store, state_out, scratch, params):
    (bias_ref, alog_ref, mlnw_ref, ssdcw_ref, ssdcb_ref, drow_ref, ssdnw_ref, sccw_ref, rgcw_ref,
     rgcb_ref, rgw_ref, rgba_ref, rgbx_ref, rglam_ref) = params
    c_out, n_out, m_out, s_out, ssdcv_out, sccv_out, rgh_out, rgcv_out = [r.at[0, seq] for r in state_out]
    ssd_ext, sc_ext, rg_ext, yssd_ref, erep_ref, wrep_ref = [r.at[seq] for r in scratch]
    gates = _gate_prep(L, pcols, bias_ref, alog_ref)
    return [
        _group_mlstm(L, pcols, mix_store, gates, c_out, n_out, m_out, mlnw_ref),
        _group_ssd(L, pcols, mix_store, gates, s_out, ssdcv_out, ssd_ext, yssd_ref, erep_ref, wrep_ref,
                   ssdcw_ref, ssdcb_ref, drow_ref, ssdnw_ref),
        _group_sconv(L, pcols, mix_store, sccv_out, sc_ext, sccw_ref),
        _group_rglru(L, pcols, mix_store, rgh_out, rgcv_out, rg_ext,
                     rgcw_ref, rgcb_ref, rgw_ref, rgba_ref, rgbx_ref, rglam_ref),
    ]


def _mixer_kernel(rows, nseq, n_carried, *refs):
    (proj_ref,), state_in, params, (mix_ref,), state_out, scratch = _split_refs(refs, 1, n_carried, 1)

    @pl.when(pl.program_id(1) == 0)
    def _():
        _load_state(state_in, state_out, *scratch[:3])

    def chain(seq):
        lo, hi = seq * rows, (seq + 1) * rows

        def pcols(start, width):
            return proj_ref[lo:hi, start:start + width]

        def mix_store(start, val):
            mix_ref[lo:hi, start:start + val.shape[1]] = val.astype(mix_ref.dtype)

        for group in _seq_groups(rows, seq, pcols, mix_store, state_out, scratch, params):
            yield from group

    chains = [chain(seq) for seq in range(nseq)]
    done = object()
    while chains:
        chains = [c for c in chains if next(c, done) is not done]


def _state_specs(state_layer, layer, carried, nseq):
    def at_layer(lyr, blk):
        blk = (blk[0], nseq) + blk[2:]
        tail = (0,) * (len(blk) - 2)
        return pl.BlockSpec(blk, lambda b, c: (lyr, b) + tail)

    in_specs = [at_layer(state_layer, blk) for blk in STATE_BLOCKS]
    if carried:
        out_specs = [at_layer(layer, blk) for blk in STATE_BLOCKS]
    else:
        assert layer == 0
        out_specs = [at_layer(0, (DEPTH,) + blk[1:]) for blk in STATE_BLOCKS]
    return in_specs, out_specs


def _state_shapes(nb):
    return [jax.ShapeDtypeStruct((DEPTH, nb) + blk[2:], F32) for blk in STATE_BLOCKS]


def _mix_scratch(rows, nseq):
    return [
        pltpu.VMEM((nseq, CONV_BASE + rows, SSD_CONV_DIM), F32),
        pltpu.VMEM((nseq, CONV_BASE + rows, BRANCH), F32),
        pltpu.VMEM((nseq, CONV_BASE + rows, BRANCH), F32),
        pltpu.VMEM((nseq, rows, BRANCH), F32),
        pltpu.VMEM((nseq, rows, BRANCH), F32),
        pltpu.VMEM((nseq, rows, BRANCH), F32),
    ]


def _mixer(proj, states, state_layer, params, layer, carried, rows, nseq, mix_dtype):
    nb = proj.shape[0] // rows
    carried = list(carried) if carried is not None else []
    st_in, st_out = _state_specs(state_layer, layer, carried, nseq)
    in_specs = [pl.BlockSpec((nseq * rows, PROJ_W), lambda b, c: (b, 0))] + st_in
    in_specs += [pl.BlockSpec((None,) + p.shape[1:], lambda b, c: (layer, 0, 0)) for p in params]
    in_specs += [pl.BlockSpec(memory_space=pl.ANY) for _ in carried]
    first_carried = 1 + N_STATES + N_PARAMS
    outs = pl.pallas_call(
        functools.partial(_mixer_kernel, rows, nseq, len(carried)),
        grid=(nb // nseq, 1),
        in_specs=in_specs,
        out_specs=[pl.BlockSpec((nseq * rows, D_MODEL), lambda b, c: (b, 0))] + st_out,
        out_shape=[jax.ShapeDtypeStruct((nb * rows, D_MODEL), mix_dtype)] + _state_shapes(nb),
        input_output_aliases={first_carried + i: 1 + i for i in range(len(carried))},
        scratch_shapes=_mix_scratch(rows, nseq),
        compiler_params=pltpu.CompilerParams(
            dimension_semantics=("arbitrary", "arbitrary"), vmem_limit_bytes=VMEM_LIMIT),
        name=f"mixer_rows{rows}",
    )(proj, *states, *params, *carried)
    return outs[0], list(outs[1:])


def _layer_kernel(final, n_carried, *refs):
    lead, state_in, params, (y_ref,), state_out, scratch = _split_refs(refs, 6, n_carried, 1)
    x_ref, xnext_ref, nw_ref, win_ref, wout_ref, fnw_ref = lead
    proj_ref, mix_ref = scratch[:2]
    mix_scratch = scratch[2:]
    step = pl.program_id(0) * pl.num_programs(1) + pl.program_id(1)

    @pl.when(pl.program_id(1) == 0)
    def _():
        _load_state(state_in, state_out, *mix_scratch[:3])

    @pl.when(step == 0)
    def _():
        xn0 = _rms_norm(x_ref[0], nw_ref[...]).astype(BF16)
        for start in range(0, PROJ_W, PROJ_TILE):
            stop = min(start + PROJ_TILE, PROJ_W)
            proj_ref[0, :, start:stop] = _dot(xn0, win_ref[:, start:stop])

    tiles = [(s, min(s + PROJ_TILE, PROJ_W)) for s in range(0, PROJ_W, PROJ_TILE)]
    n_pieces = 18

    def chunk_step(cur, nxt):
        x = x_ref[0]
        xn_next = _rms_norm(xnext_ref[0], nw_ref[...]).astype(BF16)

        def project_some(piece):
            lo = (piece * len(tiles)) // n_pieces
            hi = ((piece + 1) * len(tiles)) // n_pieces
            for start, stop in tiles[lo:hi]:
                proj_ref[nxt, :, start:stop] = _dot(xn_next, win_ref[:, start:stop])

        def pcols(start, width):
            return proj_ref[cur, :, start:start + width]

        def mix_store(start, val):
            mix_ref[:, start:start + val.shape[1]] = val.astype(BF16)

        def out_project_pair(group):
            lo, hi = (group - 1) * BRANCH, (group + 1) * BRANCH
            return _dot(mix_ref[:, lo:hi], wout_ref[lo:hi, :])

        project_some(0)
        groups = _seq_groups(CHUNK, 0, pcols, mix_store, state_out, mix_scratch, params)
        piece = 1
        y = x
        for gi, group in enumerate(groups):
            for mark in group:
                if mark is MAJOR:
                    project_some(piece)
                    piece += 1
            if gi % 2:
                y = y + out_project_pair(gi)
        assert piece == n_pieces, piece
        if final:
            y = _rms_norm(y, fnw_ref[...])
        y_ref[0] = y

    for parity in range(2):
        pl.when(step % 2 == parity)(functools.partial(chunk_step, parity, 1 - parity))


def _layer_prompt(x, norm_w, w_in, w_out, fnw, states, params, layer, carried, final):
    nb, t, _ = x.shape
    nc = t // CHUNK
    carried = list(carried) if carried is not None else []
    st_in, st_out = _state_specs(0, layer, carried, 1)
    resident = dict(pipeline_mode=pl.Buffered(1))

    def next_chunk(b, c):
        step = jnp.minimum(b * nc + c + 1, nb * nc - 1)
        return (step // nc, step % nc, 0)

    in_specs = [
        pl.BlockSpec((1, CHUNK, D_MODEL), lambda b, c: (b, c, 0)),
        pl.BlockSpec((1, CHUNK, D_MODEL), next_chunk),
        pl.BlockSpec((None, 1, D_MODEL), lambda b, c: (layer, 0, 0)),
        pl.BlockSpec((None, D_MODEL, PROJ_W), lambda b, c: (layer, 0, 0), **resident),
        pl.BlockSpec((None, D_MODEL, D_MODEL), lambda b, c: (layer, 0, 0), **resident),
        pl.BlockSpec((1, D_MODEL), lambda b, c: (0, 0)),
    ] + st_in
    in_specs += [pl.BlockSpec((None,) + p.shape[1:], lambda b, c: (layer, 0, 0)) for p in params]
    in_specs += [pl.BlockSpec(memory_space=pl.ANY) for _ in carried]
    first_carried = 6 + N_STATES + N_PARAMS
    outs = pl.pallas_call(
        functools.partial(_layer_kernel, final, len(carried)),
        grid=(nb, nc),
        in_specs=in_specs,
        out_specs=[pl.BlockSpec((1, CHUNK, D_MODEL), lambda b, c: (b, c, 0))] + st_out,
        out_shape=[jax.ShapeDtypeStruct((nb, t, D_MODEL), F32)] + _state_shapes(nb),
        input_output_aliases={first_carried + i: 1 + i for i in range(len(carried))},
        scratch_shapes=[pltpu.VMEM((2, CHUNK, PROJ_W), F32), pltpu.VMEM((CHUNK, D_MODEL), BF16)]
        + _mix_scratch(CHUNK, 1),
        compiler_params=pltpu.CompilerParams(
            dimension_semantics=("arbitrary", "arbitrary"), vmem_limit_bytes=VMEM_LIMIT),
        name="layer_prompt",
    )(x, x, norm_w, w_in, w_out, fnw, *states, *params, *carried)
    return outs[0], list(outs[1:])


O_I, O_SSD_Z, O_XBC, O_DT, O_SC, O_RG, O_END = 2560, 2568, 3080, 4104, 4112, 6160, 7184
REORDER_TILE = 512
GATE_TILE = C_SMALL // REORDER_TILE


def _reorder_kernel(src_ref, if_ref, dt_ref, o_ref):
    j = pl.program_id(1)

    @pl.when(j < GATE_TILE)
    def _():
        o_ref[...] = src_ref[0].T.astype(BF16)

    @pl.when(j == GATE_TILE)
    def _():
        zeros = jnp.zeros((REORDER_TILE - 24, D_MODEL), F32)
        small = jnp.concatenate([if_ref[0], dt_ref[0], dt_ref[0], zeros], axis=0)
        o_ref[...] = small.T.astype(BF16)


def _reorder_w_in(w_in):
    nl, d, n = w_in.shape
    assert n == O_END and d == D_MODEL
    assert C_SSD_Z % REORDER_TILE == 0 and C_SC_B % REORDER_TILE == 0 and C_SMALL % REORDER_TILE == 0
    w_t = jnp.swapaxes(w_in, 1, 2)

    def src_row(l, j):
        t_ssd, t_sc = C_SSD_Z // REORDER_TILE, C_SC_B // REORDER_TILE
        row = jnp.where(j < t_ssd, j * REORDER_TILE,
                        jnp.where(j < t_sc, O_SSD_Z + (j - t_ssd) * REORDER_TILE,
                                  O_SC + (j - t_sc) * REORDER_TILE))
        row = jnp.where(j < GATE_TILE, row, 0)
        return (l, pl.multiple_of(row, 8), 0)

    return pl.pallas_call(
        _reorder_kernel,
        grid=(nl, GATE_TILE + 1),
        in_specs=[
            pl.BlockSpec((pl.Element(1), pl.Element(REORDER_TILE), pl.Element(d)), src_row),
            pl.BlockSpec((pl.Element(1), pl.Element(8), pl.Element(d)), lambda l, j: (l, O_I, 0)),
            pl.BlockSpec((pl.Element(1), pl.Element(8), pl.Element(d)), lambda l, j: (l, O_DT, 0)),
        ],
        out_specs=pl.BlockSpec((None, d, REORDER_TILE), lambda l, j: (l, 0, j)),
        out_shape=jax.ShapeDtypeStruct((nl, d, PROJ_W), BF16),
        compiler_params=pltpu.CompilerParams(
            dimension_semantics=("arbitrary", "arbitrary"), vmem_limit_bytes=VMEM_LIMIT),
        name="reorder_w_in",
    )(w_t, w_t, w_t)


def _block_diag(w):
    nl, nblk, d, e = w.shape
    eye = jnp.eye(nblk, dtype=w.dtype)
    return jnp.einsum('lnde,nm->lndme', w, eye).reshape(nl, nblk * d, nblk * e)


def _lane_row(parts, width=LANES):
    nl = parts[0][1].shape[0]
    row = jnp.zeros((nl, width), F32)
    for off, p in parts:
        row = lax.dynamic_update_slice(row, p.astype(F32), (0, off))
    return row[:, None, :]


def kernel(x_prompt, x_sample, state_mlstm_C, state_mlstm_n, state_mlstm_m, state_ssd, state_ssd_conv, state_sconv_conv, state_rglru_h, state_rglru_conv, norm_w, w_in, ml_i_bias, ml_f_bias, ml_norm_w, ssd_conv_w, ssd_conv_b, ssd_dt_bias, ssd_A_log, ssd_D, ssd_norm_w, sc_conv_w, rg_conv_w, rg_conv_b, rg_wa, rg_ba, rg_wx, rg_bx, rg_lambda, w_out, final_norm_w):
    bp, tp, _ = x_prompt.shape
    bs, ts, _ = x_sample.shape

    w_in_r = _reorder_w_in(w_in)
    w_out_b = w_out.astype(BF16)
    rg_w = jnp.concatenate([_block_diag(rg_wa), _block_diag(rg_wx)], axis=-1).astype(BF16)
    bias_row = _lane_row([(0, ml_i_bias), (4, ml_f_bias), (8, ssd_dt_bias), (16, ssd_dt_bias)])
    alog_row = _lane_row([(8, ssd_A_log)])
    d_row = jnp.repeat(ssd_D, SSD_P, axis=-1)[:, None, :]
    row = lambda p: p[:, None, :]
    params = [bias_row, alog_row, row(ml_norm_w), ssd_conv_w, row(ssd_conv_b), d_row, row(ssd_norm_w),
              sc_conv_w, rg_conv_w, row(rg_conv_b), rg_w, row(rg_ba), row(rg_bx), row(rg_lambda)]
    norm_rows = row(norm_w)

    def pad_m(m):
        return jnp.pad(m, ((0, 0), (0, 0), (4, LANES - 4 - ML_HEADS)))[:, :, None, :]

    sample_states = [state_mlstm_C, state_mlstm_n, pad_m(state_mlstm_m), state_ssd, state_ssd_conv,
                     state_sconv_conv, state_rglru_h[:, :, None, :], state_rglru_conv]
    prompt_states = [
        jnp.zeros((1, bp, ML_HEADS, ML_D, ML_D), F32),
        jnp.zeros((1, bp, ML_HEADS, ML_D), F32),
        pad_m(jnp.full((1, bp, ML_HEADS), NEG_BIG, F32)),
        jnp.zeros((1, bp, SSD_HEADS, SSD_P, SSD_N), F32),
        jnp.zeros((1, bp, SSD_CONV - 1, SSD_CONV_DIM), F32),
        jnp.zeros((1, bp, SC_WIDTH - 1, BRANCH), F32),
        jnp.zeros((1, bp, 1, BRANCH), F32),
        jnp.zeros((1, bp, RG_CONV - 1, BRANCH), F32),
    ]

    xp = x_prompt
    xs = x_sample.reshape(bs * ts, D_MODEL)
    fnw = final_norm_w[None, :]
    st_p, st_s = None, None
    for l in range(DEPTH):
        final = l == DEPTH - 1
        xp, st_p = _layer_prompt(xp, norm_rows, w_in_r, w_out_b, fnw, prompt_states, params, l, st_p, final)
        proj_s = _inproj(xs, norm_rows, w_in_r, l, 512)
        mix_s, st_s = _mixer(proj_s, sample_states, l, params, l, st_s, ts, SAMPLE_SEQS, F32)
        xs = _outproj(mix_s, xs, w_out_b, fnw, l, final, 512)

    def unpack(states):
        states = list(states)
        states[2] = states[2][:, :, 0, 4:4 + ML_HEADS]
        states[6] = states[6][:, :, 0, :]
        return tuple(states)

    return (xp, xs.reshape(bs, ts, D_MODEL)) + unpack(st_p) + unpack(st_s)
```
